```python
import math
import jax, jax.numpy as jnp
from jax import lax
import numpy as np

D_MODEL = 1024
BATCH = 8
SEQ = 4096
DEPTH = 2

CTX_LEN = 256
GRID_W = 64
Q_BLOCK = 128
ROPE_BASE = 10000.0
EPS = 1e-6

GQA_HEADS = 8
GQA_KV_HEADS = 2
GQA_HEAD_DIM = 64
MLA_HEADS = 8
MLA_Q_RANK = 256
MLA_KV_RANK = 128
MLA_NOPE_DIM = 64
MLA_ROPE_DIM = 32
MLA_V_DIM = 64
ATTN_WIDTH = GQA_HEADS * GQA_HEAD_DIM + MLA_HEADS * MLA_V_DIM
ATTN_SPLITS = (GQA_HEADS * GQA_HEAD_DIM, GQA_KV_HEADS * GQA_HEAD_DIM, GQA_KV_HEADS * GQA_HEAD_DIM,
               MLA_Q_RANK, MLA_KV_RANK, MLA_ROPE_DIM, ATTN_WIDTH)
ATTN_IN = (GQA_HEADS + 2 * GQA_KV_HEADS) * GQA_HEAD_DIM + MLA_Q_RANK + MLA_KV_RANK + MLA_ROPE_DIM + ATTN_WIDTH
HY_WIDTH = D_MODEL
HY_ORDER = 2
HY_SHORT = 3
HY_BANDS = 16
HY_EMB = 1 + 2 * HY_BANDS
HY_FFN = 64
HY_FAST_DECAY = 0.3
HY_SLOW_DECAY = 1.5
HY_DECAY_TARGET = 1e-2

kernel_name = "hybrid_gqa_mla_hyena_prefix_dit"


def _f32(a):
    return a.astype(jnp.float32)


def split_cols(p, sizes):
    out, start = [], 0
    for n in sizes:
        out.append(p[..., start:start + n])
        start += n
    return out


def rms_norm(x, w):
    xf = _f32(x)
    y = xf * lax.rsqrt(jnp.mean(xf * xf, axis=-1, keepdims=True) + EPS)
    return (y * _f32(w)).astype(x.dtype)


def modulate(x, norm_w, shift, scale):
    return rms_norm(x, norm_w) * (1 + scale) + shift


def axial_rope_tables(n_tokens, rot_dim):
    n_rows = n_tokens // GRID_W
    rows = jnp.repeat(jnp.arange(n_rows, dtype=jnp.int32), GRID_W)
    cols = jnp.tile(jnp.arange(GRID_W, dtype=jnp.int32), n_rows)
    quarter = rot_dim // 4
    inv_freq = ROPE_BASE ** (-jnp.arange(quarter, dtype=jnp.float32) / quarter)
    ang_r = _f32(rows)[:, None] * inv_freq
    ang_c = _f32(cols)[:, None] * inv_freq
    return (jnp.cos(ang_r), jnp.sin(ang_r), jnp.cos(ang_c), jnp.sin(ang_c))


def rotate_pairs(x, cos, sin):
    a, b = jnp.split(x, 2, axis=-1)
    return jnp.concatenate([a * cos - b * sin, b * cos + a * sin], axis=-1)


def apply_axial_rope(x, tables):
    cr, sr, cc, sc = (t[:, None, :].astype(x.dtype) for t in tables)
    xr, xc = jnp.split(x, 2, axis=-1)
    return jnp.concatenate([rotate_pairs(xr, cr, sr), rotate_pairs(xc, cc, sc)], axis=-1)


def block_attention(q, k, v, scale):
    B, Lq, Hkv, G, d = q.shape
    nb = Lq // Q_BLOCK
    kf, vf = _f32(k), _f32(v)
    qb = jnp.moveaxis(q.reshape(B, nb, Q_BLOCK, Hkv, G, d), 1, 0)

    def one_block(qi):
        s = jnp.einsum("bqhgd,bkhd->bhgqk", _f32(qi), kf) * scale
        p = jax.nn.softmax(s, axis=-1)
        return jnp.einsum("bhgqk,bkhe->bqhge", p, vf)

    o = lax.map(one_block, qb)
    return jnp.moveaxis(o, 0, 1).reshape(B, Lq, Hkv * G, v.shape[-1]).astype(v.dtype)


def attn_project(h, w_in, q_norm_w, k_norm_w, cq_norm_w, ckv_norm_w, w_uq, w_ukv, rope_a, rope_m):
    B, L, _ = h.shape
    q_a, k_a, v_a, c_q, c_kv, k_pe, gate = split_cols(h @ w_in, ATTN_SPLITS)
    q_a = rms_norm(q_a.reshape(B, L, GQA_HEADS, GQA_HEAD_DIM), q_norm_w)
    k_a = rms_norm(k_a.reshape(B, L, GQA_KV_HEADS, GQA_HEAD_DIM), k_norm_w)
    v_a = v_a.reshape(B, L, GQA_KV_HEADS, GQA_HEAD_DIM)
    q_m = (rms_norm(c_q, cq_norm_w) @ w_uq).reshape(B, L, MLA_HEADS, MLA_NOPE_DIM + MLA_ROPE_DIM)
    kv_m = (rms_norm(c_kv, ckv_norm_w) @ w_ukv).reshape(B, L, MLA_HEADS, MLA_NOPE_DIM + MLA_V_DIM)
    q_nope, q_pe = q_m[..., :MLA_NOPE_DIM], q_m[..., MLA_NOPE_DIM:]
    k_nope, v_m = kv_m[..., :MLA_NOPE_DIM], kv_m[..., MLA_NOPE_DIM:]
    k_pe = k_pe.reshape(B, L, 1, MLA_ROPE_DIM)
    if rope_a is not None:
        q_a = apply_axial_rope(q_a, rope_a)
        k_a = apply_axial_rope(k_a, rope_a)
        q_pe = apply_axial_rope(q_pe, rope_m)
        k_pe = apply_axial_rope(k_pe, rope_m)
    q_m = jnp.concatenate([q_nope, q_pe], axis=-1)
    k_m = jnp.concatenate([k_nope, jnp.broadcast_to(k_pe, (B, L, MLA_HEADS, MLA_ROPE_DIM))], axis=-1)
    return q_a, k_a, v_a, q_m, k_m, v_m, gate


def attn_mixer(h_lat, h_ctx, w_in, q_norm_w, k_norm_w, cq_norm_w, ckv_norm_w, w_uq, w_ukv, w_out, with_ctx_out):
    L = h_lat.shape[1]
    rope_a = axial_rope_tables(L, GQA_HEAD_DIM)
    rope_m = axial_rope_tables(L, MLA_ROPE_DIM)
    lq_a, lk_a, lv_a, lq_m, lk_m, lv_m, l_gate = attn_project(
        h_lat, w_in, q_norm_w, k_norm_w, cq_norm_w, ckv_norm_w, w_uq, w_ukv, rope_a, rope_m)
    cq_a, ck_a, cv_a, cq_m, ck_m, cv_m, c_gate = attn_project(
        h_ctx, w_in, q_norm_w, k_norm_w, cq_norm_w, ckv_norm_w, w_uq, w_ukv, None, None)

    def mix(q_a, q_m, gate, k_a, v_a, k_m, v_m):
        B, Lq = q_a.shape[:2]
        o_a = block_attention(q_a.reshape(B, Lq, GQA_KV_HEADS, GQA_HEADS // GQA_KV_HEADS, GQA_HEAD_DIM),
                              k_a, v_a, GQA_HEAD_DIM ** -0.5)
        o_m = block_attention(q_m[:, :, :, None, :], k_m, v_m, (MLA_NOPE_DIM + MLA_ROPE_DIM) ** -0.5)
        o = jnp.concatenate([o_a.reshape(B, Lq, -1), o_m.reshape(B, Lq, -1)], axis=-1)
        return (o * jax.nn.silu(gate)) @ w_out

    y_lat = mix(lq_a, lq_m, l_gate,
                jnp.concatenate([ck_a, lk_a], axis=1), jnp.concatenate([cv_a, lv_a], axis=1),
                jnp.concatenate([ck_m, lk_m], axis=1), jnp.concatenate([cv_m, lv_m], axis=1))
    y_ctx = mix(cq_a, cq_m, c_gate, ck_a, cv_a, ck_m, cv_m) if with_ctx_out else None
    return y_lat, y_ctx


def hyena_filters(L, w1, b1, w2, b2, w3, b3, freq):
    t = jnp.linspace(0.0, 1.0, L, dtype=jnp.float32)[:, None]
    w = (2.0 * math.pi / L) * jnp.arange(L, dtype=jnp.float32)[:, None]
    bands = jnp.linspace(1e-4, HY_BANDS - 1, HY_BANDS, dtype=jnp.float32)
    emb = jnp.concatenate([t, jnp.cos(w * bands), -jnp.sin(w * bands)], axis=-1)
    hid = jnp.sin(_f32(freq) * (emb @ _f32(w1) + _f32(b1)))
    hid = jnp.sin(_f32(freq) * (hid @ _f32(w2) + _f32(b2)))
    h = (hid @ _f32(w3) + _f32(b3)).reshape(L, HY_ORDER, 2, HY_WIDTH)
    max_decay = math.log(HY_DECAY_TARGET) / HY_FAST_DECAY
    min_decay = math.log(HY_DECAY_TARGET) / HY_SLOW_DECAY
    deltas = jnp.linspace(min_decay, max_decay, HY_WIDTH, dtype=jnp.float32)
    h = h * jnp.exp(-t * jnp.abs(deltas))[:, None, None, :]
    fwd, bwd = h[:, :, 0], h[:, :, 1]
    kern = jnp.concatenate([fwd, jnp.zeros((1, HY_ORDER, HY_WIDTH), jnp.float32), bwd[:0:-1]], axis=0)
    kern = kern / jnp.sum(jnp.abs(kern), axis=0, keepdims=True)
    return jnp.fft.rfft(kern, axis=0)


def long_conv(u, k_f, skip):
    L = u.shape[1]
    uf = _f32(u)
    y = jnp.fft.irfft(jnp.fft.rfft(uf, n=2 * L, axis=1) * k_f, n=2 * L, axis=1)[:, :L]
    return (y + uf * _f32(skip)).astype(u.dtype)


def short_conv(u, w, b):
    L = u.shape[1]
    pad = HY_SHORT // 2
    up = jnp.pad(u, ((0, 0), (pad, pad), (0, 0)))
    return sum(up[:, j:j + L] * w[j] for j in range(HY_SHORT)) + b


def hyena_mixer(h, w_in, conv_w, conv_b, f_w1, f_b1, f_w2, f_b2, f_w3, f_b3, freq, skip, w_out):
    L = h.shape[1]
    p = h @ w_in
    u = short_conv(p[..., :(HY_ORDER + 1) * HY_WIDTH], conv_w, conv_b)
    gate = p[..., (HY_ORDER + 1) * HY_WIDTH:]
    parts = jnp.split(u, HY_ORDER + 1, axis=-1)
    k_f = hyena_filters(L, f_w1, f_b1, f_w2, f_b2, f_w3, f_b3, freq)
    z = parts[0]
    for o in range(HY_ORDER):
        z = parts[o + 1] * long_conv(z, k_f[:, o], skip[o])
    return (z * jax.nn.silu(gate)) @ w_out


def setup_inputs(seed: int = 0) -> dict:
    key = jax.random.key(seed)
    ks = jax.random.split(key, 32)
    n_attn = (DEPTH + 1) // 2
    n_hy = DEPTH // 2
    f32 = jnp.float32

    def nrm(k, shape, fan_in):
        return jax.random.normal(k, shape, f32) * fan_in ** -0.5

    def gain(k, shape):
        return 1.0 + 0.05 * jax.random.normal(k, shape, f32)

    def small(k, shape, s=0.02):
        return s * jax.random.normal(k, shape, f32)

    return {
        "x": jax.random.normal(ks[0], (BATCH, SEQ, D_MODEL), f32),
        "c": jax.random.normal(ks[1], (BATCH, D_MODEL), f32),
        "ctx": jax.random.normal(ks[2], (BATCH, CTX_LEN, D_MODEL), f32),
        "c_ctx": jax.random.normal(ks[3], (D_MODEL,), f32),
        "ada_w": nrm(ks[4], (DEPTH, D_MODEL, 3 * D_MODEL), D_MODEL),
        "ada_b": small(ks[5], (DEPTH, 3 * D_MODEL)),
        "norm_w": gain(ks[6], (DEPTH, D_MODEL)),
        "attn_w_in": nrm(ks[7], (n_attn, D_MODEL, ATTN_IN), D_MODEL),
        "attn_q_norm": gain(ks[8], (n_attn, GQA_HEAD_DIM)),
        "attn_k_norm": gain(ks[9], (n_attn, GQA_HEAD_DIM)),
        "mla_q_norm": gain(ks[10], (n_attn, MLA_Q_RANK)),
        "mla_kv_norm": gain(ks[11], (n_attn, MLA_KV_RANK)),
        "mla_w_uq": nrm(ks[12], (n_attn, MLA_Q_RANK, MLA_HEADS * (MLA_NOPE_DIM + MLA_ROPE_DIM)), MLA_Q_RANK),
        "mla_w_ukv": nrm(ks[13], (n_attn, MLA_KV_RANK, MLA_HEADS * (MLA_NOPE_DIM + MLA_V_DIM)), MLA_KV_RANK),
        "attn_w_out": nrm(ks[14], (n_attn, ATTN_WIDTH, D_MODEL), ATTN_WIDTH),
        "hy_w_in": nrm(ks[15], (n_hy, D_MODEL, (HY_ORDER + 2) * HY_WIDTH), D_MODEL),
        "hy_conv_w": nrm(ks[16], (n_hy, HY_SHORT, (HY_ORDER + 1) * HY_WIDTH), HY_SHORT),
        "hy_conv_b": small(ks[17], (n_hy, (HY_ORDER + 1) * HY_WIDTH)),
        "hy_ffn_w1": nrm(ks[18], (n_hy, HY_EMB, HY_FFN), HY_EMB),
        "hy_ffn_b1": small(ks[19], (n_hy, HY_FFN), 0.1),
        "hy_ffn_w2": nrm(ks[20], (n_hy, HY_FFN, HY_FFN), HY_FFN),
        "hy_ffn_b2": small(ks[21], (n_hy, HY_FFN), 0.1),
        "hy_ffn_w3": nrm(ks[22], (n_hy, HY_FFN, HY_ORDER * 2 * HY_WIDTH), HY_FFN),
        "hy_ffn_b3": small(ks[23], (n_hy, HY_ORDER * 2 * HY_WIDTH), 0.1),
        "hy_freq": gain(ks[24], (n_hy, HY_FFN)),
        "hy_skip": small(ks[25], (n_hy, HY_ORDER, HY_WIDTH), 0.1),
        "hy_w_out": nrm(ks[26], (n_hy, HY_WIDTH, D_MODEL), HY_WIDTH),
        "final_norm_w": gain(ks[27], (D_MODEL,)),
    }


def reference(x, c, ctx, c_ctx, ada_w, ada_b, norm_w, attn_w_in, attn_q_norm, attn_k_norm,
              mla_q_norm, mla_kv_norm, mla_w_uq, mla_w_ukv, attn_w_out, hy_w_in, hy_conv_w, hy_conv_b,
              hy_ffn_w1, hy_ffn_b1, hy_ffn_w2, hy_ffn_b2, hy_ffn_w3, hy_ffn_b3, hy_freq, hy_skip,
              hy_w_out, final_norm_w):
    last_attn = (DEPTH - 1) - ((DEPTH - 1) % 2)
    s_lat = jax.nn.silu(c)
    s_ctx = jax.nn.silu(c_ctx)
    x_lat, x_ctx = x, ctx
    for i in range(DEPTH):
        j = i // 2
        ctx_update = i < last_attn
        shift, scale, gate = jnp.split(s_lat @ ada_w[i] + ada_b[i], 3, axis=-1)
        h_lat = modulate(x_lat, norm_w[i], shift[:, None], scale[:, None])
        need_ctx = (i % 2 == 0) or ctx_update
        if need_ctx:
            shift_c, scale_c, gate_c = jnp.split(s_ctx @ ada_w[i] + ada_b[i], 3, axis=-1)
            h_ctx = modulate(x_ctx, norm_w[i], shift_c, scale_c)
        if i % 2 == 0:
            y_lat, y_ctx = attn_mixer(h_lat, h_ctx, attn_w_in[j], attn_q_norm[j], attn_k_norm[j],
                                      mla_q_norm[j], mla_kv_norm[j], mla_w_uq[j], mla_w_ukv[j],
                                      attn_w_out[j], ctx_update)
        else:
            hy_args = (hy_w_in[j], hy_conv_w[j], hy_conv_b[j], hy_ffn_w1[j], hy_ffn_b1[j], hy_ffn_w2[j],
                       hy_ffn_b2[j], hy_ffn_w3[j], hy_ffn_b3[j], hy_freq[j], hy_skip[j], hy_w_out[j])
            y_lat = hyena_mixer(h_lat, *hy_args)
            y_ctx = hyena_mixer(h_ctx, *hy_args) if ctx_update else None
        x_lat = x_lat + gate[:, None] * y_lat
        if ctx_update:
            x_ctx = x_ctx + gate_c * y_ctx
    return rms_norm(x_lat, final_norm_w)
```

```python
import functools
import math

import numpy as np
import jax
import jax.numpy as jnp
from jax import lax
from jax.experimental import pallas as pl
from jax.experimental.pallas import tpu as pltpu

EPS = 1e-6
GRID_W = 64
ROPE_BASE = 10000.0
GQA_HEADS, GQA_KV_HEADS, GQA_HEAD_DIM = 8, 2, 64
MLA_HEADS, MLA_Q_RANK, MLA_KV_RANK = 8, 256, 128
MLA_NOPE_DIM, MLA_ROPE_DIM, MLA_V_DIM = 64, 32, 64
HY_ORDER, HY_SHORT, HY_BANDS, HY_FFN = 2, 3, 16, 64
HY_FAST_DECAY, HY_SLOW_DECAY, HY_DECAY_TARGET = 0.3, 1.5, 1e-2
LANE = 128
DFT_N2 = 128
VMEM_LIMIT = 56 * 1024 * 1024
LOG2E = 1.4426950408889634
HI = lax.Precision.HIGHEST
F32 = jnp.float32
BF16 = jnp.bfloat16


def _cparams(sem):
    return pltpu.CompilerParams(dimension_semantics=sem, vmem_limit_bytes=VMEM_LIMIT)


def _per_chunk(fn, *arrs):
    width = arrs[0].shape[-1]
    outs = [fn(*[a[:, c:c + LANE] for a in arrs]) for c in range(0, width, LANE)]
    return outs[0] if len(outs) == 1 else jnp.concatenate(outs, axis=-1)


def _lane_iota(shape):
    return lax.broadcasted_iota(jnp.int32, shape, len(shape) - 1)


def _partner(x, s):
    up = pltpu.roll(x, LANE - s, axis=1)
    dn = pltpu.roll(x, s, axis=1)
    return jnp.where((_lane_iota(x.shape) & s) == 0, up, dn)


def _seg_allsum(x, seg):
    s = seg // 2
    while s >= 1:
        x = x + _partner(x, s)
        s //= 2
    return x


def _seg_rms(x, seg):
    def f(c):
        return c * lax.rsqrt(_seg_allsum(c * c, seg) * (1.0 / seg) + EPS)
    return _per_chunk(f, x)


def _rope(x, cos, sin_signed, s):
    def f(c, cs, sn):
        return c * cs + _partner(c, s) * sn
    return _per_chunk(f, x, cos, sin_signed)


def _row_rms(x):
    return x * lax.rsqrt(jnp.mean(x * x, axis=-1, keepdims=True) + EPS)


def _silu(x):
    return x * (1.0 / (1.0 + jnp.exp(-x)))


def _ada_kernel(c_ref, w_ref, b_ref, o_ref):
    s = _silu(c_ref[...])
    o_ref[0] = jnp.dot(s, w_ref[0], precision=HI, preferred_element_type=F32) + b_ref[0]


def _ada(cs, ada_w, ada_b):
    depth, d, d3 = ada_w.shape
    rows = cs.shape[0]
    nt = d3 // d
    return pl.pallas_call(
        _ada_kernel,
        grid=(depth, nt),
        in_specs=[pl.BlockSpec((rows, d), lambda i, j: (0, 0)),
                  pl.BlockSpec((1, d, d), lambda i, j: (i, 0, j)),
                  pl.BlockSpec((1, 1, d), lambda i, j: (i, 0, j))],
        out_specs=pl.BlockSpec((1, rows, d), lambda i, j: (i, 0, j)),
        out_shape=jax.ShapeDtypeStruct((depth, rows, d3), F32),
        compiler_params=_cparams(("arbitrary", "arbitrary")),
        name="ada",
    )(cs, ada_w, ada_b.reshape(depth, 1, d3))


_QA_W, _KA_W, _VA_W, _KPE_W = 512, 512, 256, 128


def _prep_kernel(*refs, latent, d):
    if latent:
        (x_ref, ada_ref, nw_ref, w_ref, qn_ref, kn_ref, cqn_ref, ckvn_ref, wuq_ref, wukv_ref,
         ca_ref, sa_ref, cm_ref, sm_ref,
         qa_o, ka_o, va_o, qm_o, km_o, vm_o, sg_o) = refs
    else:
        (x_ref, ada_ref, nw_ref, w_ref, kn_ref, ckvn_ref, wukv_ref,
         ka_o, va_o, km_o, vm_o) = refs
    ada = ada_ref[0]
    shift, scale = ada[:, :d], ada[:, d:2 * d]
    h = (_row_rms(x_ref[0]) * nw_ref[...]) * (1.0 + scale) + shift
    p = jnp.dot(h.astype(BF16), w_ref[...], preferred_element_type=F32)
    off = 0
    if latent:
        qa = p[:, off:off + _QA_W]; off += _QA_W
    ka = p[:, off:off + _KA_W]; off += _KA_W
    va = p[:, off:off + _VA_W]; off += _VA_W
    if latent:
        cq = p[:, off:off + MLA_Q_RANK]; off += MLA_Q_RANK
    ckv = p[:, off:off + MLA_KV_RANK]; off += MLA_KV_RANK
    kpe = p[:, off:off + _KPE_W]; off += _KPE_W

    ka = _seg_rms(ka, GQA_HEAD_DIM) * kn_ref[...]
    if latent:
        ka = _rope(ka, ca_ref[...], sa_ref[...], GQA_HEAD_DIM // 4)
        kpe = _rope(kpe, cm_ref[...], sm_ref[...], MLA_ROPE_DIM // 4)
    ka_o[0] = ka.astype(BF16)
    va_o[0] = va.astype(BF16)
    ckv_n = (_row_rms(ckv) * ckvn_ref[...]).astype(BF16)
    kv = jnp.dot(ckv_n, wukv_ref[...], preferred_element_type=F32)
    nk = MLA_HEADS * LANE
    km = kv[:, :nk] + jnp.concatenate([kpe] * MLA_HEADS, axis=-1)
    km_o[0] = km.astype(BF16)
    vm_o[0] = kv[:, nk:].astype(BF16)
    if latent:
        qa = _seg_rms(qa, GQA_HEAD_DIM) * qn_ref[...]
        qa = _rope(qa, ca_ref[...], sa_ref[...], GQA_HEAD_DIM // 4)
        qa_o[0] = (qa * (GQA_HEAD_DIM ** -0.5 * LOG2E)).astype(BF16)
        cq_n = (_row_rms(cq) * cqn_ref[...]).astype(BF16)
        qm = jnp.dot(cq_n, wuq_ref[...], preferred_element_type=F32)
        cm = jnp.concatenate([cm_ref[...]] * MLA_HEADS, axis=-1)
        sm = jnp.concatenate([sm_ref[...]] * MLA_HEADS, axis=-1)
        qm = _rope(qm, cm, sm, MLA_ROPE_DIM // 4)
        qm_o[0] = (qm * ((MLA_NOPE_DIM + MLA_ROPE_DIM) ** -0.5 * LOG2E)).astype(BF16)
        gate = p[:, off:off + d]
        sg_o[0] = _silu(gate).astype(BF16)


def _prep(x, ada, nw, w, qn, kn, cqn, ckvn, wuq, wukv, tabs, *, latent, tl):
    b, l, d = x.shape
    grid = (l // tl, b)
    row = lambda i, j: (j, i, 0)
    const = lambda i, j: (0, 0)
    tab = lambda i, j: (i, 0)
    xspec = pl.BlockSpec((1, tl, d), row)
    adaspec = pl.BlockSpec((1, 1, ada.shape[-1]), lambda i, j: (j, 0, 0))

    def full(a):
        return pl.BlockSpec(a.shape, const)

    def out(width):
        return (pl.BlockSpec((1, tl, width), row), jax.ShapeDtypeStruct((b, l, width), BF16))

    if latent:
        ca, sa, cm, sm = tabs
        ins = [x, ada, nw, w, qn, kn, cqn, ckvn, wuq, wukv, ca, sa, cm, sm]
        in_specs = [xspec, adaspec, full(nw), full(w), full(qn), full(kn), full(cqn), full(ckvn),
                    full(wuq), full(wukv)] + [pl.BlockSpec((tl, t.shape[1]), tab) for t in tabs]
        outs = [out(_QA_W), out(_KA_W), out(_VA_W), out(MLA_HEADS * LANE), out(MLA_HEADS * LANE),
                out(MLA_HEADS * MLA_V_DIM), out(d)]
    else:
        ins = [x, ada, nw, w, kn, ckvn, wukv]
        in_specs = [xspec, adaspec, full(nw), full(w), full(kn), full(ckvn), full(wukv)]
        outs = [out(_KA_W), out(_VA_W), out(MLA_HEADS * LANE), out(MLA_HEADS * MLA_V_DIM)]
    return pl.pallas_call(
        functools.partial(_prep_kernel, latent=latent, d=d),
        grid=grid, in_specs=in_specs,
        out_specs=[o[0] for o in outs], out_shape=[o[1] for o in outs],
        compiler_params=_cparams(("arbitrary", "arbitrary")),
        name="prep_lat" if latent else "prep_ctx",
    )(*ins)


def _attn_kernel(q_ref, kc_ref, vc_ref, kl_ref, vl_ref, o_ref, *, tk, q_shared):
    tq = q_ref.shape[1]
    nblk = kl_ref.shape[1] // tk
    dn = (((1,), (1,)), ((), ()))
    outs = []
    for e in range(2):
        q = q_ref[0, :, 0:LANE] if q_shared else q_ref[0, :, e * LANE:(e + 1) * LANE]

        def step(k, v, carry):
            m, l, acc = carry
            s = lax.dot_general(q, k, dn, preferred_element_type=F32)
            m_new = jnp.maximum(m, jnp.max(s, axis=-1, keepdims=True))
            p = jnp.exp2(s - m_new)
            alpha = jnp.exp2(m - m_new)
            l = alpha * l + jnp.sum(p, axis=-1, keepdims=True)
            acc = alpha * acc + jnp.dot(p.astype(BF16), v, preferred_element_type=F32)
            return m_new, l, acc

        carry = (jnp.full((tq, 1), -1e30, F32), jnp.zeros((tq, 1), F32), jnp.zeros((tq, LANE), F32))
        carry = step(kc_ref[0, :, e * LANE:(e + 1) * LANE], vc_ref[0], carry)

        def body(i, carry):
            r = pl.multiple_of(i * tk, tk)
            return step(kl_ref[0, pl.ds(r, tk), e * LANE:(e + 1) * LANE], vl_ref[0, pl.ds(r, tk), :], carry)

        m, l, acc = lax.fori_loop(0, nblk, body, carry)
        outs.append(acc / l)
    half = _lane_iota(outs[0].shape) < (LANE // 2)
    o_ref[0] = jnp.where(half, outs[0], outs[1]).astype(o_ref.dtype)


def _attention(q, kc, vc, kl, vl, *, q_shared, kv_group, tq, tk):
    b, l, _ = q.shape
    wq = LANE if q_shared else 2 * LANE
    pairs = q.shape[-1] // wq
    lc = kc.shape[1]
    kv = lambda bi, j, i: (bi, 0, j // kv_group)
    return pl.pallas_call(
        functools.partial(_attn_kernel, tk=tk, q_shared=q_shared),
        grid=(b, pairs, l // tq),
        in_specs=[pl.BlockSpec((1, tq, wq), lambda bi, j, i: (bi, i, j)),
                  pl.BlockSpec((1, lc, 2 * LANE), kv), pl.BlockSpec((1, lc, LANE), kv),
                  pl.BlockSpec((1, l, 2 * LANE), kv), pl.BlockSpec((1, l, LANE), kv)],
        out_specs=pl.BlockSpec((1, tq, LANE), lambda bi, j, i: (bi, i, j)),
        out_shape=jax.ShapeDtypeStruct((b, l, pairs * LANE), BF16),
        compiler_params=_cparams(("arbitrary", "arbitrary", "arbitrary")),
        name="attn_gqa" if q_shared else "attn_mla",
    )(q, kc, vc, kl, vl)


def _out0_kernel(x_ref, oa_ref, om_ref, sg_ref, w_ref, ada_ref, o_ref, *, d):
    o = jnp.concatenate([oa_ref[0], om_ref[0]], axis=-1).astype(F32) * sg_ref[0].astype(F32)
    y = jnp.dot(o.astype(BF16), w_ref[...], preferred_element_type=F32)
    o_ref[0] = x_ref[0] + ada_ref[0][:, 2 * d:3 * d] * y


def _out0(x, oa, om, sg, w, ada, *, tl):
    b, l, d = x.shape
    row = lambda bi, i: (bi, i, 0)
    return pl.pallas_call(
        functools.partial(_out0_kernel, d=d),
        grid=(b, l // tl),
        in_specs=[pl.BlockSpec((1, tl, d), row), pl.BlockSpec((1, tl, oa.shape[-1]), row),
                  pl.BlockSpec((1, tl, om.shape[-1]), row), pl.BlockSpec((1, tl, d), row),
                  pl.BlockSpec(w.shape, lambda bi, i: (0, 0)),
                  pl.BlockSpec((1, 1, ada.shape[-1]), lambda bi, i: (bi, 0, 0))],
        out_specs=pl.BlockSpec((1, tl, d), row),
        out_shape=jax.ShapeDtypeStruct((b, l, d), F32),
        compiler_params=_cparams(("arbitrary", "arbitrary")),
        name="out0",
    )(x, oa, om, sg, w, ada)


_HALO = 8


def _hyin_kernel(x_ref, xp_ref, xn_ref, ada_ref, nw_ref, w_ref, cw_ref, cb_ref, o_ref, h_sc, *, d, n_conv):
    i, n = pl.program_id(1), pl.program_id(2)
    tl = x_ref.shape[1]

    @pl.when(n == 0)
    def _():
        ada = ada_ref[0]
        shift, scale = ada[:, :d], ada[:, d:2 * d]

        def mod(x):
            return (_row_rms(x) * nw_ref[...]) * (1.0 + scale) + shift

        hp = mod(xp_ref[0]) * (i > 0).astype(F32)
        hn = mod(xn_ref[0]) * (i < pl.num_programs(1) - 1).astype(F32)
        h_sc[0:tl, :] = mod(x_ref[0]).astype(BF16)
        h_sc[tl:tl + 2 * _HALO, :] = jnp.concatenate([hp, hn], axis=0).astype(BF16)

    p = jnp.dot(h_sc[...], w_ref[...], preferred_element_type=F32)
    pm = p[0:tl]

    @pl.when(n < n_conv)
    def _():
        rows = lax.broadcasted_iota(jnp.int32, pm.shape, 0)
        prev = jnp.where(rows == 0, p[tl + _HALO - 1:tl + _HALO], pltpu.roll(pm, 1, axis=0))
        nxt = jnp.where(rows == tl - 1, p[tl + _HALO:tl + _HALO + 1], pltpu.roll(pm, tl - 1, axis=0))
        cw = cw_ref[...]
        o_ref[0, 0] = prev * cw[0:1] + pm * cw[1:2] + nxt * cw[2:3] + cb_ref[...]

    @pl.when(n >= n_conv)
    def _():
        o_ref[0, 0] = _silu(pm)


def _hyin(x, ada, nw, w, cw, cb, *, tl):
    b, l, d = x.shape
    ng = w.shape[1] // d
    n_conv = cw.shape[1] // d
    tb = tl // _HALO
    nb = l // _HALO
    return pl.pallas_call(
        functools.partial(_hyin_kernel, d=d, n_conv=n_conv),
        grid=(b, l // tl, ng),
        in_specs=[pl.BlockSpec((1, tl, d), lambda bi, i, n: (bi, i, 0)),
                  pl.BlockSpec((1, _HALO, d), lambda bi, i, n: (bi, jnp.maximum(i * tb - 1, 0), 0)),
                  pl.BlockSpec((1, _HALO, d), lambda bi, i, n: (bi, jnp.minimum((i + 1) * tb, nb - 1), 0)),
                  pl.BlockSpec((1, 1, ada.shape[-1]), lambda bi, i, n: (bi, 0, 0)),
                  pl.BlockSpec(nw.shape, lambda bi, i, n: (0, 0)),
                  pl.BlockSpec((d, d), lambda bi, i, n: (0, n)),
                  pl.BlockSpec((HY_SHORT, d), lambda bi, i, n: (0, jnp.minimum(n, n_conv - 1))),
                  pl.BlockSpec((1, d), lambda bi, i, n: (0, jnp.minimum(n, n_conv - 1)))],
        out_specs=pl.BlockSpec((1, 1, tl, d), lambda bi, i, n: (n, bi, i, 0)),
        out_shape=jax.ShapeDtypeStruct((ng, b, l, d), F32),
        scratch_shapes=[pltpu.VMEM((tl + 2 * _HALO, d), BF16)],
        compiler_params=_cparams(("arbitrary", "arbitrary", "arbitrary")),
        name="hy_in",
    )(x, x, x, ada, nw, w, cw, cb)


def _filt_kernel(emb_ref, w1_ref, b1_ref, w2_ref, b2_ref, w3_ref, b3_ref, fr_ref, dl_ref, h_o, s_o, *, reps):
    emb = emb_ref[...]
    fr = fr_ref[...]
    hid = jnp.sin(fr * (jnp.dot(emb, w1_ref[...], precision=HI, preferred_element_type=F32) + b1_ref[...]))
    hid = jnp.sin(fr * (jnp.dot(hid, w2_ref[...], precision=HI, preferred_element_type=F32) + b2_ref[...]))
    h = jnp.dot(hid, w3_ref[...], precision=HI, preferred_element_type=F32) + b3_ref[...]
    win = jnp.exp(-emb[:, 0:1] * dl_ref[...])
    hw = h * jnp.concatenate([win] * reps, axis=-1)
    h_o[...] = hw

    @pl.when(pl.program_id(0) == 0)
    def _():
        s_o[...] = jnp.zeros_like(s_o)

    s_o[...] += jnp.sum(jnp.abs(hw), axis=0, keepdims=True)


def _filters(emb, w1, b1, w2, b2, w3, b3, fr, dl, *, tl):
    l = emb.shape[0]
    wo = w3.shape[1]
    const = lambda i: (0, 0)
    full = lambda a: pl.BlockSpec(a.shape, const)
    return pl.pallas_call(
        functools.partial(_filt_kernel, reps=wo // dl.shape[1]),
        grid=(l // tl,),
        in_specs=[pl.BlockSpec((tl, emb.shape[1]), lambda i: (i, 0)), full(w1), full(b1), full(w2), full(b2),
                  full(w3), full(b3), full(fr), full(dl)],
        out_specs=[pl.BlockSpec((tl, wo), lambda i: (i, 0)), pl.BlockSpec((1, wo), const)],
        out_shape=[jax.ShapeDtypeStruct((l, wo), F32), jax.ShapeDtypeStruct((1, wo), F32)],
        compiler_params=_cparams(("arbitrary",)),
        name="hy_filter",
    )(emb, w1, b1, w2, b2, w3, b3, fr, dl)


def _lmul_kernel(m_ref, x_ref, *rest, gated):
    y = jnp.dot(m_ref[...], x_ref[0], precision=HI, preferred_element_type=F32)
    if gated:
        u_ref, g_ref, sk_ref, o_ref = rest
        o_ref[0] = g_ref[0] * (y + u_ref[0] * sk_ref[...])
    else:
        (o_ref,) = rest
        o_ref[0] = y


def _lmul(m, x, gate_args=None, *, tw, name):
    g, k, w = x.shape
    mr = m.shape[0]
    xs = lambda r: pl.BlockSpec((1, r, tw), lambda gi, j: (gi, 0, j))
    ins, in_specs = [m, x], [pl.BlockSpec(m.shape, lambda gi, j: (0, 0)), xs(k)]
    if gate_args is not None:
        u, gt, sk = gate_args
        ins += [u, gt, sk]
        in_specs += [xs(mr), xs(mr), pl.BlockSpec((1, tw), lambda gi, j: (0, 0))]
    return pl.pallas_call(
        functools.partial(_lmul_kernel, gated=gate_args is not None),
        grid=(g, w // tw), in_specs=in_specs, out_specs=xs(mr),
        out_shape=jax.ShapeDtypeStruct((g, mr, w), F32),
        compiler_params=_cparams(("arbitrary", "arbitrary")),
        name=name,
    )(*ins)


def _mid_kernel(a_ref, g_ref, *rest, spectrum):
    n2 = a_ref.shape[3]
    ct = a_ref.shape[4]
    a = a_ref[0, :, 0].reshape(2 * n2, ct)
    x = jnp.dot(g_ref[0], a, precision=HI, preferred_element_type=F32)
    if spectrum:
        sc_ref, o_ref = rest
        o_ref[0, :, 0] = (x * sc_ref[...]).reshape(2, n2, ct)
    else:
        gt_ref, kf_ref, o_ref = rest
        xr, xi = x[:n2], x[n2:]
        kr, ki = kf_ref[0, 0, 0], kf_ref[0, 1, 0]
        y = jnp.concatenate([xr * kr - xi * ki, xr * ki + xi * kr], axis=0)
        o_ref[0, :, 0] = jnp.dot(gt_ref[0], y, precision=HI, preferred_element_type=F32).reshape(2, n2, ct)


def _mid(a, g, gt=None, kf=None, scale=None, *, ct, order=0):
    p, _, n1, n2, c = a.shape
    ablk = pl.BlockSpec((1, 2, 1, n2, ct), lambda k, j, pi: (pi, 0, k, 0, j))
    gblk = pl.BlockSpec((1, 2 * n2, 2 * n2), lambda k, j, pi: (k, 0, 0))
    if kf is None:
        ins = [a, g, scale]
        in_specs = [ablk, gblk, pl.BlockSpec((1, ct), lambda k, j, pi: (0, j))]
    else:
        ins = [a, g, gt, kf]
        in_specs = [ablk, gblk, gblk,
                    pl.BlockSpec((1, 2, 1, n2, ct), lambda k, j, pi: (order, 0, k, 0, j))]
    return pl.pallas_call(
        functools.partial(_mid_kernel, spectrum=kf is None),
        grid=(n1, c // ct, p), in_specs=in_specs, out_specs=ablk,
        out_shape=jax.ShapeDtypeStruct(a.shape, F32),
        compiler_params=_cparams(("arbitrary", "arbitrary", "arbitrary")),
        name="hy_spectrum" if kf is None else "hy_mid",
    )(*ins)


def _dft_tables(l):
    n = 2 * l
    n2 = DFT_N2
    n1 = n // n2
    k1 = np.arange(n1, dtype=np.float64)
    a1 = 2.0 * np.pi * np.outer(k1, k1) / n1
    c1, s1 = np.cos(a1), np.sin(a1)
    hn = n1 // 2
    f_half = np.block([[c1[:, :hn], s1[:, :hn]], [-s1[:, :hn], c1[:, :hn]]])
    f_real = np.concatenate([c1, -s1], axis=0)
    ci, si = c1[:hn, :], s1[:hn, :]
    f_inv = np.block([[ci, -si], [si, ci]]) / n
    kk = k1[:, None, None] + n1 * np.arange(n2, dtype=np.float64)[None, :, None]
    th = 2.0 * np.pi * kk * np.arange(n2, dtype=np.float64)[None, None, :] / n
    c, s = np.cos(th), np.sin(th)
    g = np.concatenate([np.concatenate([c, s], axis=2), np.concatenate([-s, c], axis=2)], axis=1)
    gt = np.transpose(g, (0, 2, 1))
    f = lambda a: jnp.asarray(a, F32)
    return f(f_half), f(f_real), f(f_inv), f(g), f(gt)


def _to_dft_rows(u, n1h):
    b, l, c = u.shape
    return u.reshape(b // 2, 2 * n1h, (l // n1h) * c)


def _long_conv(u2, gate2, skip_t, kf, order, tabs, *, c, tw, ct):
    f_half, _, f_inv, g, gt = tabs
    p = u2.shape[0]
    n1 = f_half.shape[0] // 2
    a = _lmul(f_half, u2, tw=tw, name="hy_outer_fwd")
    a = a.reshape(p, 2, n1, DFT_N2, c)
    a = _mid(a, g, gt, kf, ct=ct, order=order)
    a = a.reshape(p, 2 * n1, DFT_N2 * c)
    return _lmul(f_inv, a, (u2, gate2, skip_t), tw=tw, name="hy_outer_inv")


def _out1_kernel(x_ref, z_ref, sg_ref, w_ref, ada_ref, fw_ref, o_ref, *, d):
    y = jnp.dot((z_ref[0] * sg_ref[0]).astype(BF16), w_ref[...], preferred_element_type=F32)
    x = x_ref[0] + ada_ref[0][:, 2 * d:3 * d] * y
    o_ref[0] = _row_rms(x) * fw_ref[...]


def _out1(x, z, sg, w, ada, fw, *, tl):
    b, l, d = x.shape
    row = lambda bi, i: (bi, i, 0)
    blk = pl.BlockSpec((1, tl, d), row)
    return pl.pallas_call(
        functools.partial(_out1_kernel, d=d),
        grid=(b, l // tl),
        in_specs=[blk, blk, blk, pl.BlockSpec(w.shape, lambda bi, i: (0, 0)),
                  pl.BlockSpec((1, 1, ada.shape[-1]), lambda bi, i: (bi, 0, 0)),
                  pl.BlockSpec(fw.shape, lambda bi, i: (0, 0))],
        out_specs=blk,
        out_shape=jax.ShapeDtypeStruct((b, l, d), F32),
        compiler_params=_cparams(("arbitrary", "arbitrary")),
        name="out1",
    )(x, z, sg, w, ada, fw)


def _pack_attn_w_in(w):
    d = w.shape[0]
    o = 0
    wq = w[:, o:o + 512]; o += 512
    wk = w[:, o:o + 128]; o += 128
    wv = w[:, o:o + 128]; o += 128
    wcq = w[:, o:o + MLA_Q_RANK]; o += MLA_Q_RANK
    wckv = w[:, o:o + MLA_KV_RANK]; o += MLA_KV_RANK
    wkpe = w[:, o:o + MLA_ROPE_DIM]; o += MLA_ROPE_DIM
    wg = w[:, o:]
    z64 = jnp.zeros((d, 64), w.dtype)
    ka, va = [], []
    for g in range(GQA_KV_HEADS):
        kg = wk[:, 64 * g:64 * (g + 1)]
        vg = wv[:, 64 * g:64 * (g + 1)]
        ka += [kg, z64, z64, kg]
        va += [vg, vg]
    kpe = jnp.concatenate([z64, wkpe, jnp.zeros((d, 32), w.dtype)], axis=1)
    kv_part = jnp.concatenate(ka + va + [wckv, kpe], axis=1)
    lat = jnp.concatenate([wq] + ka + va + [wcq, wckv, kpe, wg], axis=1)
    return lat.astype(BF16), kv_part.astype(BF16)


def _pack_mla_up(w_uq, w_ukv):
    dq = MLA_NOPE_DIM + MLA_ROPE_DIM
    pad = LANE - dq
    uq = jnp.concatenate(
        [jnp.concatenate([w_uq[:, dq * h:dq * (h + 1)], jnp.zeros((w_uq.shape[0], pad), w_uq.dtype)], axis=1)
         for h in range(MLA_HEADS)], axis=1)
    dkv = MLA_NOPE_DIM + MLA_V_DIM
    kn = jnp.concatenate(
        [jnp.concatenate([w_ukv[:, dkv * h:dkv * h + MLA_NOPE_DIM],
                          jnp.zeros((w_ukv.shape[0], LANE - MLA_NOPE_DIM), w_ukv.dtype)], axis=1)
         for h in range(MLA_HEADS)], axis=1)
    vm = jnp.concatenate([w_ukv[:, dkv * h + MLA_NOPE_DIM:dkv * (h + 1)] for h in range(MLA_HEADS)], axis=1)
    return uq.astype(BF16), jnp.concatenate([kn, vm], axis=1).astype(BF16)


def _rope_tables(l):
    rows = (jnp.arange(l, dtype=jnp.int32) // GRID_W).astype(F32)[:, None]
    cols = (jnp.arange(l, dtype=jnp.int32) % GRID_W).astype(F32)[:, None]

    def tab(rot_dim):
        q = rot_dim // 4
        inv = ROPE_BASE ** (-jnp.arange(q, dtype=F32) / q)
        ar, ac = rows * inv, cols * inv
        cos = jnp.concatenate([jnp.cos(ar)] * 2 + [jnp.cos(ac)] * 2, axis=1)
        sin = jnp.concatenate([-jnp.sin(ar), jnp.sin(ar), -jnp.sin(ac), jnp.sin(ac)], axis=1)
        return cos, sin

    ca, sa = tab(GQA_HEAD_DIM)
    ca = jnp.concatenate([ca] * (_QA_W // GQA_HEAD_DIM), axis=1)
    sa = jnp.concatenate([sa] * (_QA_W // GQA_HEAD_DIM), axis=1)
    cm, sm = tab(MLA_ROPE_DIM)
    one, zero = jnp.ones((l, 1), F32), jnp.zeros((l, 1), F32)
    cm = jnp.concatenate([jnp.tile(one, (1, 64)), cm, jnp.tile(one, (1, 32))], axis=1)
    sm = jnp.concatenate([jnp.tile(zero, (1, 64)), sm, jnp.tile(zero, (1, 32))], axis=1)
    return ca, sa, cm, sm


def _pad2(a, r, c):
    return jnp.pad(a, ((0, r - a.shape[0]), (0, c - a.shape[1])))


def kernel(x, c, ctx, c_ctx, ada_w, ada_b, norm_w, attn_w_in, attn_q_norm, attn_k_norm, mla_q_norm, mla_kv_norm, mla_w_uq, mla_w_ukv, attn_w_out, hy_w_in, hy_conv_w, hy_conv_b, hy_ffn_w1, hy_ffn_b1, hy_ffn_w2, hy_ffn_b2, hy_ffn_w3, hy_ffn_b3, hy_freq, hy_skip, hy_w_out, final_norm_w):
    b, l, d = x.shape
    lc = ctx.shape[1]
    tl = min(256, l)

    rows = -(-(b + 1) // 8) * 8
    cs = jnp.concatenate([c, c_ctx[None, :], jnp.zeros((rows - b - 1, d), F32)], axis=0)
    ada = _ada(cs, ada_w, ada_b)
    ada_lat = [ada[i, :b].reshape(b, 1, 3 * d) for i in range(ada.shape[0])]
    ada_ctx0 = jnp.broadcast_to(ada[0, b].reshape(1, 1, 3 * d), (b, 1, 3 * d))

    w_lat, w_kv = _pack_attn_w_in(attn_w_in[0])
    wuq, wukv = _pack_mla_up(mla_w_uq[0], mla_w_ukv[0])
    nw0 = norm_w[0].reshape(1, d)
    qn = jnp.tile(attn_q_norm[0], _QA_W // GQA_HEAD_DIM).reshape(1, _QA_W)
    kn = jnp.tile(attn_k_norm[0], _KA_W // GQA_HEAD_DIM).reshape(1, _KA_W)
    cqn = mla_q_norm[0].reshape(1, MLA_Q_RANK)
    ckvn = mla_kv_norm[0].reshape(1, MLA_KV_RANK)
    tabs = _rope_tables(l)
    qa, ka, va, qm, km, vm, sg = _prep(x, ada_lat[0], nw0, w_lat, qn, kn, cqn, ckvn, wuq, wukv, tabs,
                                       latent=True, tl=tl)
    kac, vac, kmc, vmc = _prep(ctx, ada_ctx0, nw0, w_kv, None, kn, None, ckvn, None, wukv, None,
                               latent=False, tl=min(tl, lc))
    tq = min(512, l)
    oa = _attention(qa, kac, vac, ka, va, q_shared=True, kv_group=2, tq=tq, tk=tq)
    om = _attention(qm, kmc, vmc, km, vm, q_shared=False, kv_group=1, tq=tq, tk=tq)
    x1 = _out0(x, oa, om, sg, attn_w_out[0].astype(BF16), ada_lat[0], tl=tl)

    nw1 = norm_w[1].reshape(1, d)
    u = _hyin(x1, ada_lat[1], nw1, hy_w_in[0].astype(BF16), hy_conv_w[0], hy_conv_b[0].reshape(1, -1), tl=tl)
    n1 = 2 * l // DFT_N2
    n1h = n1 // 2
    tabs_d = _dft_tables(l)

    t = jnp.linspace(0.0, 1.0, l, dtype=F32)[:, None]
    wpos = (2.0 * math.pi / l) * jnp.arange(l, dtype=F32)[:, None]
    bands = jnp.linspace(1e-4, HY_BANDS - 1, HY_BANDS, dtype=F32)
    emb = jnp.concatenate([t, jnp.cos(wpos * bands), -jnp.sin(wpos * bands)], axis=-1)
    deltas = jnp.abs(jnp.linspace(math.log(HY_DECAY_TARGET) / HY_SLOW_DECAY,
                                  math.log(HY_DECAY_TARGET) / HY_FAST_DECAY, d, dtype=F32)).reshape(1, d)
    wf = hy_ffn_w3.shape[-1]
    hw, asum = _filters(_pad2(emb, l, LANE), _pad2(hy_ffn_w1[0], LANE, LANE), _pad2(hy_ffn_b1[0][None], 1, LANE),
                        _pad2(hy_ffn_w2[0], LANE, LANE), _pad2(hy_ffn_b2[0][None], 1, LANE),
                        _pad2(hy_ffn_w3[0], LANE, wf), hy_ffn_b3[0][None], _pad2(hy_freq[0][None], 1, LANE),
                        deltas, tl=tl)
    hw = hw.reshape(l, HY_ORDER, 2, d)
    fwd, bwd = hw[:, :, 0], hw[:, :, 1]
    kern = jnp.concatenate([fwd, jnp.zeros((1, HY_ORDER, d), F32), bwd[:0:-1]], axis=0)
    asum = asum.reshape(HY_ORDER, 2, d)
    l1 = asum[:, 0] + asum[:, 1] - jnp.abs(bwd[0])
    oc = HY_ORDER * d
    kern2 = kern.reshape(1, n1, DFT_N2 * oc)
    tw = 16 * d
    af = _lmul(tabs_d[1], kern2, tw=tw, name="hy_outer_filt").reshape(1, 2, n1, DFT_N2, oc)
    kf = _mid(af, tabs_d[3], scale=(1.0 / l1).reshape(1, oc), ct=512)
    kf = jnp.moveaxis(kf.reshape(2, n1, DFT_N2, HY_ORDER, d), 3, 0)

    v2, x1g, x2g = (_to_dft_rows(u[i], n1h) for i in range(3))
    skip_t = jnp.tile(hy_skip[0], (1, tw // d))
    z = _long_conv(v2, x1g, skip_t[0:1], kf, 0, tabs_d, c=d, tw=tw, ct=512)
    z = _long_conv(z, x2g, skip_t[1:2], kf, 1, tabs_d, c=d, tw=tw, ct=512)
    z = z.reshape(b, l, d)
    return _out1(x1, z, u[3], hy_w_out[0].astype(BF16), ada_lat[1], final_norm_w.reshape(1, d), tl=tl)
```

```python
import functools
import math

import numpy as np
import jax
import jax.numpy as jnp
from jax import lax
from jax.experimental import pallas as pl
from jax.experimental.pallas import tpu as pltpu

EPS = 1e-6
GRID_W = 64
ROPE_BASE = 10000.0
GQA_HEADS, GQA_KV_HEADS, GQA_HEAD_DIM = 8, 2, 64
MLA_HEADS, MLA_Q_RANK, MLA_KV_RANK = 8, 256, 128
MLA_NOPE_DIM, MLA_ROPE_DIM, MLA_V_DIM = 64, 32, 64
HY_ORDER, HY_SHORT, HY_BANDS, HY_FFN = 2, 3, 16, 64
HY_FAST_DECAY, HY_SLOW_DECAY, HY_DECAY_TARGET = 0.3, 1.5, 1e-2
LANE = 128
DFT_N2 = 128
VMEM_LIMIT = 56 * 1024 * 1024
LOG2E = 1.4426950408889634
HI = lax.Precision.HIGHEST
F32 = jnp.float32
BF16 = jnp.bfloat16


def _cparams(sem):
    return pltpu.CompilerParams(dimension_semantics=sem, vmem_limit_bytes=VMEM_LIMIT)


def _per_chunk(fn, *arrs):
    width = arrs[0].shape[-1]
    outs = [fn(*[a[:, c:c + LANE] for a in arrs]) for c in range(0, width, LANE)]
    return outs[0] if len(outs) == 1 else jnp.concatenate(outs, axis=-1)


def _lane_iota(shape):
    return lax.broadcasted_iota(jnp.int32, shape, len(shape) - 1)


def _partner(x, s):
    up = pltpu.roll(x, LANE - s, axis=1)
    dn = pltpu.roll(x, s, axis=1)
    return jnp.where((_lane_iota(x.shape) & s) == 0, up, dn)


def _seg_allsum(x, seg):
    s = seg // 2
    while s >= 1:
        x = x + _partner(x, s)
        s //= 2
    return x


def _seg_rms(x, seg):
    def f(c):
        return c * lax.rsqrt(_seg_allsum(c * c, seg) * (1.0 / seg) + EPS)
    return _per_chunk(f, x)


def _rope(x, cos, sin_signed, s):
    def f(c, cs, sn):
        return c * cs + _partner(c, s) * sn
    return _per_chunk(f, x, cos, sin_signed)


def _row_rms(x):
    return x * lax.rsqrt(jnp.mean(x * x, axis=-1, keepdims=True) + EPS)


def _silu(x):
    return x * (1.0 / (1.0 + jnp.exp(-x)))


def _ada_kernel(c_ref, w_ref, b_ref, o_ref):
    s = _silu(c_ref[...])
    o_ref[0] = jnp.dot(s, w_ref[0], precision=HI, preferred_element_type=F32) + b_ref[0]


def _ada(cs, ada_w, ada_b):
    depth, d, d3 = ada_w.shape
    rows = cs.shape[0]
    nt = d3 // d
    return pl.pallas_call(
        _ada_kernel,
        grid=(depth, nt),
        in_specs=[pl.BlockSpec((rows, d), lambda i, j: (0, 0)),
                  pl.BlockSpec((1, d, d), lambda i, j: (i, 0, j)),
                  pl.BlockSpec((1, 1, d), lambda i, j: (i, 0, j))],
        out_specs=pl.BlockSpec((1, rows, d), lambda i, j: (i, 0, j)),
        out_shape=jax.ShapeDtypeStruct((depth, rows, d3), F32),
        compiler_params=_cparams(("arbitrary", "arbitrary")),
        name="ada",
    )(cs, ada_w, ada_b.reshape(depth, 1, d3))


_QA_W, _KA_W, _VA_W, _KPE_W = 512, 512, 256, 128


def _prep_kernel(*refs, latent, d):
    if latent:
        (x_ref, ada_ref, nw_ref, w_ref, qn_ref, kn_ref, cqn_ref, ckvn_ref, wuq_ref, wukv_ref,
         ca_ref, sa_ref, cm_ref, sm_ref,
         qa_o, ka_o, va_o, qm_o, km_o, vm_o, sg_o) = refs
    else:
        (x_ref, ada_ref, nw_ref, w_ref, kn_ref, ckvn_ref, wukv_ref,
         ka_o, va_o, km_o, vm_o) = refs
    ada = ada_ref[0]
    shift, scale = ada[:, :d], ada[:, d:2 * d]
    h = (_row_rms(x_ref[0]) * nw_ref[...]) * (1.0 + scale) + shift
    p = jnp.dot(h.astype(BF16), w_ref[...], preferred_element_type=F32)
    off = 0
    if latent:
        qa = p[:, off:off + _QA_W]; off += _QA_W
    ka = p[:, off:off + _KA_W]; off += _KA_W
    va = p[:, off:off + _VA_W]; off += _VA_W
    if latent:
        cq = p[:, off:off + MLA_Q_RANK]; off += MLA_Q_RANK
    ckv = p[:, off:off + MLA_KV_RANK]; off += MLA_KV_RANK
    kpe = p[:, off:off + _KPE_W]; off += _KPE_W

    ka = _seg_rms(ka, GQA_HEAD_DIM) * kn_ref[...]
    if latent:
        ka = _rope(ka, ca_ref[...], sa_ref[...], GQA_HEAD_DIM // 4)
        kpe = _rope(kpe, cm_ref[...], sm_ref[...], MLA_ROPE_DIM // 4)
    ka_o[0] = ka.astype(BF16)
    va_o[0] = va.T.astype(BF16)
    ckv_n = (_row_rms(ckv) * ckvn_ref[...]).astype(BF16)
    kv = jnp.dot(ckv_n, wukv_ref[...], preferred_element_type=F32)
    nk = MLA_HEADS * LANE
    km = kv[:, :nk] + jnp.concatenate([kpe] * MLA_HEADS, axis=-1)
    km_o[0] = km.astype(BF16)
    vm_o[0] = kv[:, nk:].T.astype(BF16)
    if latent:
        qa = _seg_rms(qa, GQA_HEAD_DIM) * qn_ref[...]
        qa = _rope(qa, ca_ref[...], sa_ref[...], GQA_HEAD_DIM // 4)
        qa_o[0] = (qa * (GQA_HEAD_DIM ** -0.5 * LOG2E)).astype(BF16)
        cq_n = (_row_rms(cq) * cqn_ref[...]).astype(BF16)
        qm = jnp.dot(cq_n, wuq_ref[...], preferred_element_type=F32)
        cm = jnp.concatenate([cm_ref[...]] * MLA_HEADS, axis=-1)
        sm = jnp.concatenate([sm_ref[...]] * MLA_HEADS, axis=-1)
        qm = _rope(qm, cm, sm, MLA_ROPE_DIM // 4)
        qm_o[0] = (qm * ((MLA_NOPE_DIM + MLA_ROPE_DIM) ** -0.5 * LOG2E)).astype(BF16)
        gate = p[:, off:off + d]
        sg_o[0] = _silu(gate).astype(BF16)


def _prep(x, ada, nw, w, qn, kn, cqn, ckvn, wuq, wukv, tabs, *, latent, tl):
    b, l, d = x.shape
    grid = (l // tl, b)
    row = lambda i, j: (j, i, 0)
    const = lambda i, j: (0, 0)
    tab = lambda i, j: (i, 0)
    xspec = pl.BlockSpec((1, tl, d), row)
    adaspec = pl.BlockSpec((1, 1, ada.shape[-1]), lambda i, j: (j, 0, 0))

    def full(a):
        return pl.BlockSpec(a.shape, const)

    def out(width):
        return (pl.BlockSpec((1, tl, width), row), jax.ShapeDtypeStruct((b, l, width), BF16))

    def out_t(width):
        return (pl.BlockSpec((1, width, tl), lambda i, j: (j, 0, i)), jax.ShapeDtypeStruct((b, width, l), BF16))

    if latent:
        ca, sa, cm, sm = tabs
        ins = [x, ada, nw, w, qn, kn, cqn, ckvn, wuq, wukv, ca, sa, cm, sm]
        in_specs = [xspec, adaspec, full(nw), full(w), full(qn), full(kn), full(cqn), full(ckvn),
                    full(wuq), full(wukv)] + [pl.BlockSpec((tl, t.shape[1]), tab) for t in tabs]
        outs = [out(_QA_W), out(_KA_W), out_t(_VA_W), out(MLA_HEADS * LANE), out(MLA_HEADS * LANE),
                out_t(MLA_HEADS * MLA_V_DIM), out(d)]
    else:
        ins = [x, ada, nw, w, kn, ckvn, wukv]
        in_specs = [xspec, adaspec, full(nw), full(w), full(kn), full(ckvn), full(wukv)]
        outs = [out(_KA_W), out_t(_VA_W), out(MLA_HEADS * LANE), out_t(MLA_HEADS * MLA_V_DIM)]
    return pl.pallas_call(
        functools.partial(_prep_kernel, latent=latent, d=d),
        grid=grid, in_specs=in_specs,
        out_specs=[o[0] for o in outs], out_shape=[o[1] for o in outs],
        compiler_params=_cparams(("arbitrary", "arbitrary")),
        name="prep_lat" if latent else "prep_ctx",
    )(*ins)


def _attn_kernel(q_ref, kc_ref, vc_ref, kl_ref, vl_ref, o_ref, s_a, s_b, *, tk, q_shared):
    tq = q_ref.shape[1]
    lk = kl_ref.shape[1]
    nblk = lk // tk
    dn = (((1,), (1,)), ((), ()))
    qs = [q_ref[0, :, 0:LANE] if q_shared else q_ref[0, :, e * LANE:(e + 1) * LANE] for e in range(2)]

    def scores(e, k):
        return lax.dot_general(k, qs[e], dn, preferred_element_type=F32)

    def scores_at(e, r):
        return scores(e, kl_ref[0, pl.ds(r, tk), e * LANE:(e + 1) * LANE])

    def update(s, vt, carry):
        m, l, acc = carry
        m_new = jnp.maximum(m, jnp.max(s, axis=0, keepdims=True))
        p = jnp.exp2(s - m_new)
        alpha = jnp.exp2(m - m_new)
        l = alpha * l + jnp.sum(p, axis=0, keepdims=True)
        acc = alpha * acc + jnp.dot(vt, p.astype(BF16), preferred_element_type=F32)
        return m_new, l, acc

    init = (jnp.full((1, tq), -1e30, F32), jnp.zeros((1, tq), F32), jnp.zeros((LANE, tq), F32))
    s_ctx = [scores(e, kc_ref[0, :, e * LANE:(e + 1) * LANE]) for e in range(2)]
    for e in range(2):
        s_a[e] = scores_at(e, 0)
    carry = tuple(update(s_ctx[e], vc_ref[0], init) for e in range(2))

    def body(j, carry):
        r0 = pl.multiple_of(2 * j * tk, tk)
        r1 = pl.multiple_of(r0 + tk, tk)
        r2 = pl.multiple_of(jnp.minimum(r1 + tk, lk - tk), tk)
        for e in range(2):
            s_b[e] = scores_at(e, r1)
        vt = vl_ref[0, :, pl.ds(r0, tk)]
        carry = tuple(update(s_a[e], vt, carry[e]) for e in range(2))
        for e in range(2):
            s_a[e] = scores_at(e, r2)
        vt = vl_ref[0, :, pl.ds(r1, tk)]
        return tuple(update(s_b[e], vt, carry[e]) for e in range(2))

    carry = lax.fori_loop(0, nblk // 2, body, carry)
    outs = [acc / l for (_, l, acc) in carry]
    half = lax.broadcasted_iota(jnp.int32, outs[0].shape, 0) < (LANE // 2)
    o_ref[0] = jnp.where(half, outs[0], outs[1]).T.astype(o_ref.dtype)


def _attention(q, kc, vc, kl, vl, *, q_shared, kv_group, tq, tk):
    b, l, _ = q.shape
    wq = LANE if q_shared else 2 * LANE
    pairs = q.shape[-1] // wq
    lc = kc.shape[1]
    assert (l // tk) % 2 == 0, "key blocks are consumed two per loop trip"
    kv = lambda bi, j, i: (bi, 0, j // kv_group)
    vt = lambda bi, j, i: (bi, j // kv_group, 0)
    return pl.pallas_call(
        functools.partial(_attn_kernel, tk=tk, q_shared=q_shared),
        grid=(b, pairs, l // tq),
        in_specs=[pl.BlockSpec((1, tq, wq), lambda bi, j, i: (bi, i, j)),
                  pl.BlockSpec((1, lc, 2 * LANE), kv), pl.BlockSpec((1, LANE, lc), vt),
                  pl.BlockSpec((1, l, 2 * LANE), kv), pl.BlockSpec((1, LANE, l), vt)],
        out_specs=pl.BlockSpec((1, tq, LANE), lambda bi, j, i: (bi, i, j)),
        out_shape=jax.ShapeDtypeStruct((b, l, pairs * LANE), BF16),
        scratch_shapes=[pltpu.VMEM((2, tk, tq), F32), pltpu.VMEM((2, tk, tq), F32)],
        compiler_params=_cparams(("arbitrary", "arbitrary", "arbitrary")),
        name="attn_gqa" if q_shared else "attn_mla",
    )(q, kc, vc, kl, vl)


def _out0_kernel(x_ref, oa_ref, om_ref, sg_ref, w_ref, ada_ref, o_ref, *, d):
    o = jnp.concatenate([oa_ref[0], om_ref[0]], axis=-1).astype(F32) * sg_ref[0].astype(F32)
    y = jnp.dot(o.astype(BF16), w_ref[...], preferred_element_type=F32)
    o_ref[0] = x_ref[0] + ada_ref[0][:, 2 * d:3 * d] * y


def _out0(x, oa, om, sg, w, ada, *, tl):
    b, l, d = x.shape
    row = lambda bi, i: (bi, i, 0)
    return pl.pallas_call(
        functools.partial(_out0_kernel, d=d),
        grid=(b, l // tl),
        in_specs=[pl.BlockSpec((1, tl, d), row), pl.BlockSpec((1, tl, oa.shape[-1]), row),
                  pl.BlockSpec((1, tl, om.shape[-1]), row), pl.BlockSpec((1, tl, d), row),
                  pl.BlockSpec(w.shape, lambda bi, i: (0, 0)),
                  pl.BlockSpec((1, 1, ada.shape[-1]), lambda bi, i: (bi, 0, 0))],
        out_specs=pl.BlockSpec((1, tl, d), row),
        out_shape=jax.ShapeDtypeStruct((b, l, d), F32),
        compiler_params=_cparams(("arbitrary", "arbitrary")),
        name="out0",
    )(x, oa, om, sg, w, ada)


_HALO = 8


def _hyin_kernel(x_ref, xp_ref, xn_ref, ada_ref, nw_ref, w_ref, cw_ref, cb_ref, o_ref, h_sc, *, d, n_conv):
    i, n = pl.program_id(1), pl.program_id(2)
    tl = x_ref.shape[1]

    @pl.when(n == 0)
    def _():
        ada = ada_ref[0]
        shift, scale = ada[:, :d], ada[:, d:2 * d]

        def mod(x):
            return (_row_rms(x) * nw_ref[...]) * (1.0 + scale) + shift

        hp = mod(xp_ref[0]) * (i > 0).astype(F32)
        hn = mod(xn_ref[0]) * (i < pl.num_programs(1) - 1).astype(F32)
        h_sc[0:tl, :] = mod(x_ref[0]).astype(BF16)
        h_sc[tl:tl + 2 * _HALO, :] = jnp.concatenate([hp, hn], axis=0).astype(BF16)

    p = jnp.dot(h_sc[...], w_ref[...], preferred_element_type=F32)
    pm = p[0:tl]

    @pl.when(n < n_conv)
    def _():
        rows = lax.broadcasted_iota(jnp.int32, pm.shape, 0)
        prev = jnp.where(rows == 0, p[tl + _HALO - 1:tl + _HALO], pltpu.roll(pm, 1, axis=0))
        nxt = jnp.where(rows == tl - 1, p[tl + _HALO:tl + _HALO + 1], pltpu.roll(pm, tl - 1, axis=0))
        cw = cw_ref[...]
        o_ref[0, 0] = prev * cw[0:1] + pm * cw[1:2] + nxt * cw[2:3] + cb_ref[...]

    @pl.when(n >= n_conv)
    def _():
        o_ref[0, 0] = _silu(pm)


def _hyin(x, ada, nw, w, cw, cb, *, tl):
    b, l, d = x.shape
    ng = w.shape[1] // d
    n_conv = cw.shape[1] // d
    tb = tl // _HALO
    nb = l // _HALO
    return pl.pallas_call(
        functools.partial(_hyin_kernel, d=d, n_conv=n_conv),
        grid=(b, l // tl, ng),
        in_specs=[pl.BlockSpec((1, tl, d), lambda bi, i, n: (bi, i, 0)),
                  pl.BlockSpec((1, _HALO, d), lambda bi, i, n: (bi, jnp.maximum(i * tb - 1, 0), 0)),
                  pl.BlockSpec((1, _HALO, d), lambda bi, i, n: (bi, jnp.minimum((i + 1) * tb, nb - 1), 0)),
                  pl.BlockSpec((1, 1, ada.shape[-1]), lambda bi, i, n: (bi, 0, 0)),
                  pl.BlockSpec(nw.shape, lambda bi, i, n: (0, 0)),
                  pl.BlockSpec((d, d), lambda bi, i, n: (0, n)),
                  pl.BlockSpec((HY_SHORT, d), lambda bi, i, n: (0, jnp.minimum(n, n_conv - 1))),
                  pl.BlockSpec((1, d), lambda bi, i, n: (0, jnp.minimum(n, n_conv - 1)))],
        out_specs=pl.BlockSpec((1, 1, tl, d), lambda bi, i, n: (n, bi, i, 0)),
        out_shape=jax.ShapeDtypeStruct((ng, b, l, d), F32),
        scratch_shapes=[pltpu.VMEM((tl + 2 * _HALO, d), BF16)],
        compiler_params=_cparams(("arbitrary", "arbitrary", "arbitrary")),
        name="hy_in",
    )(x, x, x, ada, nw, w, cw, cb)


def _filt_kernel(emb_ref, w1_ref, b1_ref, w2_ref, b2_ref, w3_ref, b3_ref, fr_ref, dl_ref, h_o, s_o, *, reps):
    emb = emb_ref[...]
    fr = fr_ref[...]
    hid = jnp.sin(fr * (jnp.dot(emb, w1_ref[...], precision=HI, preferred_element_type=F32) + b1_ref[...]))
    hid = jnp.sin(fr * (jnp.dot(hid, w2_ref[...], precision=HI, preferred_element_type=F32) + b2_ref[...]))
    h = jnp.dot(hid, w3_ref[...], precision=HI, preferred_element_type=F32) + b3_ref[...]
    win = jnp.exp(-emb[:, 0:1] * dl_ref[...])
    hw = h * jnp.concatenate([win] * reps, axis=-1)
    h_o[...] = hw

    @pl.when(pl.program_id(0) == 0)
    def _():
        s_o[...] = jnp.zeros_like(s_o)

    s_o[...] += jnp.sum(jnp.abs(hw), axis=0, keepdims=True)


def _filters(emb, w1, b1, w2, b2, w3, b3, fr, dl, *, tl):
    l = emb.shape[0]
    wo = w3.shape[1]
    const = lambda i: (0, 0)
    full = lambda a: pl.BlockSpec(a.shape, const)
    return pl.pallas_call(
        functools.partial(_filt_kernel, reps=wo // dl.shape[1]),
        grid=(l // tl,),
        in_specs=[pl.BlockSpec((tl, emb.shape[1]), lambda i: (i, 0)), full(w1), full(b1), full(w2), full(b2),
                  full(w3), full(b3), full(fr), full(dl)],
        out_specs=[pl.BlockSpec((tl, wo), lambda i: (i, 0)), pl.BlockSpec((1, wo), const)],
        out_shape=[jax.ShapeDtypeStruct((l, wo), F32), jax.ShapeDtypeStruct((1, wo), F32)],
        compiler_params=_cparams(("arbitrary",)),
        name="hy_filter",
    )(emb, w1, b1, w2, b2, w3, b3, fr, dl)


def _cmul_const(a, ang):
    ar, ai = a
    q = ang / (0.5 * math.pi)
    if abs(q - round(q)) < 1e-12:
        return [(ar, ai), (-ai, ar), (-ar, -ai), (ai, -ar)][int(round(q)) % 4]
    c, s = math.cos(ang), math.sin(ang)
    return (ar * c - ai * s, ar * s + ai * c)


def _fft_dif(x):
    x = list(x)
    n = len(x)
    half = n // 2
    while half >= 1:
        for base in range(0, n, 2 * half):
            for j in range(half):
                a, b = x[base + j], x[base + j + half]
                ang = -math.pi * j / half
                if b is None:
                    x[base + j + half] = None if a is None else _cmul_const(a, ang)
                else:
                    x[base + j] = (a[0] + b[0], a[1] + b[1])
                    x[base + j + half] = _cmul_const((a[0] - b[0], a[1] - b[1]), ang)
        half //= 2
    return x


def _ifft_dit(x, keep):
    x = list(x)
    n = len(x)
    half = 1
    while half <= n // 2:
        last = half == n // 2
        for base in range(0, n, 2 * half):
            for j in range(half):
                a = x[base + j]
                b = _cmul_const(x[base + j + half], math.pi * j / half)
                x[base + j] = (a[0] + b[0], a[1] + b[1])
                if not last or base + j + half < keep:
                    x[base + j + half] = (a[0] - b[0], a[1] - b[1])
        half *= 2
    return x[:keep]


def _outer_fwd_kernel(u_ref, o_ref, *, real_input):
    n1 = o_ref.shape[2]
    if real_input:
        x = [(u_ref[0, i], jnp.zeros_like(u_ref[0, i])) for i in range(n1)]
    else:
        x = [(u_ref[0, 0, i], u_ref[0, 1, i]) for i in range(n1 // 2)] + [None] * (n1 // 2)
    for s, (re, im) in enumerate(_fft_dif(x)):
        o_ref[0, 0, s] = re
        o_ref[0, 1, s] = im


def _outer_fwd(u, *, real_input, tn2, ct):
    p, c = u.shape[0], u.shape[-1]
    n2 = u.shape[-2]
    n1 = u.shape[1] if real_input else 2 * u.shape[2]
    if real_input:
        in_spec = pl.BlockSpec((1, n1, tn2, ct), lambda pi, r, j: (pi, 0, r, j))
    else:
        in_spec = pl.BlockSpec((1, 2, n1 // 2, tn2, ct), lambda pi, r, j: (pi, 0, 0, r, j))
    return pl.pallas_call(
        functools.partial(_outer_fwd_kernel, real_input=real_input),
        grid=(p, n2 // tn2, c // ct), in_specs=[in_spec],
        out_specs=pl.BlockSpec((1, 2, n1, tn2, ct), lambda pi, r, j: (pi, 0, 0, r, j)),
        out_shape=jax.ShapeDtypeStruct((p, 2, n1, n2, c), F32),
        compiler_params=_cparams(("arbitrary", "arbitrary", "arbitrary")),
        name="hy_outer_filt" if real_input else "hy_outer_fwd",
    )(u)


def _outer_inv_kernel(a_ref, u_ref, g_ref, sk_ref, o_ref):
    n1 = a_ref.shape[2]
    y = _ifft_dit([(a_ref[0, 0, s], a_ref[0, 1, s]) for s in range(n1)], n1 // 2)
    sk = sk_ref[...]
    for i, (re, im) in enumerate(y):
        o_ref[0, 0, i] = g_ref[0, 0, i] * (re + u_ref[0, 0, i] * sk)
        o_ref[0, 1, i] = g_ref[0, 1, i] * (im + u_ref[0, 1, i] * sk)


def _outer_inv(a, u, gate, skip, *, tn2, ct):
    p, _, n1, n2, c = a.shape
    blk = lambda rows: pl.BlockSpec((1, 2, rows, tn2, ct), lambda pi, r, j: (pi, 0, 0, r, j))
    return pl.pallas_call(
        _outer_inv_kernel,
        grid=(p, n2 // tn2, c // ct),
        in_specs=[blk(n1), blk(n1 // 2), blk(n1 // 2), pl.BlockSpec((1, ct), lambda pi, r, j: (0, j))],
        out_specs=blk(n1 // 2),
        out_shape=jax.ShapeDtypeStruct(u.shape, F32),
        compiler_params=_cparams(("arbitrary", "arbitrary", "arbitrary")),
        name="hy_outer_inv",
    )(a, u, gate, skip)


def _dot3(m3, x):
    hi = x.astype(BF16)
    lo = (x - hi.astype(F32)).astype(BF16)
    return jnp.dot(m3, jnp.concatenate([hi, hi, lo], axis=0), preferred_element_type=F32)


def _mid_kernel(a_ref, g_ref, *rest, spectrum):
    n2 = a_ref.shape[3]
    ct = a_ref.shape[4]
    x = _dot3(g_ref[0], a_ref[0, :, 0].reshape(2 * n2, ct))
    if spectrum:
        sc_ref, o_ref = rest
        o_ref[0, :, 0] = (x * sc_ref[...]).reshape(2, n2, ct)
    else:
        gt_ref, kf_ref, o_ref = rest
        xr, xi = x[:n2], x[n2:]
        kr, ki = kf_ref[0, 0, 0], kf_ref[0, 1, 0]
        y = jnp.concatenate([xr * kr - xi * ki, xr * ki + xi * kr], axis=0)
        o_ref[0, :, 0] = _dot3(gt_ref[0], y).reshape(2, n2, ct)


def _mid(a, g, gt=None, kf=None, scale=None, *, ct, kf_col0=0):
    p, _, n1, n2, c = a.shape
    ablk = pl.BlockSpec((1, 2, 1, n2, ct), lambda k, j, pi: (pi, 0, k, 0, j))
    gblk = pl.BlockSpec((1,) + g.shape[1:], lambda k, j, pi: (k, 0, 0))
    if kf is None:
        ins = [a, g, scale]
        in_specs = [ablk, gblk, pl.BlockSpec((1, ct), lambda k, j, pi: (0, j))]
    else:
        ins = [a, g, gt, kf]
        in_specs = [ablk, gblk, gblk,
                    pl.BlockSpec((1, 2, 1, n2, ct), lambda k, j, pi: (0, 0, k, 0, kf_col0 + j))]
    return pl.pallas_call(
        functools.partial(_mid_kernel, spectrum=kf is None),
        grid=(n1, c // ct, p), in_specs=in_specs, out_specs=ablk,
        out_shape=jax.ShapeDtypeStruct(a.shape, F32),
        compiler_params=_cparams(("arbitrary", "arbitrary", "arbitrary")),
        name="hy_spectrum" if kf is None else "hy_mid",
    )(*ins)


def _dft_tables(l):
    n = 2 * l
    n2 = DFT_N2
    n1 = n // n2
    bits = n1.bit_length() - 1
    k1 = np.array([int(format(s, "0%db" % bits)[::-1], 2) for s in range(n1)], dtype=np.float64)
    kk = k1[:, None, None] + n1 * np.arange(n2, dtype=np.float64)[None, :, None]
    th = 2.0 * np.pi * kk * np.arange(n2, dtype=np.float64)[None, None, :] / n
    c, s = np.cos(th), np.sin(th)
    g = np.concatenate([np.concatenate([c, s], axis=2), np.concatenate([-s, c], axis=2)], axis=1)

    def split3(m):
        m = jnp.asarray(m, F32)
        hi = m.astype(BF16)
        lo = (m - hi.astype(F32)).astype(BF16)
        return jnp.concatenate([hi, lo, hi], axis=2)

    return split3(g), split3(np.transpose(g, (0, 2, 1)))


def _pair_view(u, n1h):
    b, l, c = u.shape
    return u.reshape(b // 2, 2, n1h, l // n1h, c)


def _long_conv(u, gate, skip, kf, kf_col0, tabs, *, ct):
    g, gt = tabs
    a = _outer_fwd(u, real_input=False, tn2=32, ct=256)
    a = _mid(a, g, gt, kf, ct=ct, kf_col0=kf_col0)
    return _outer_inv(a, u, gate, skip, tn2=32, ct=256)


def _out1_kernel(x_ref, z_ref, sg_ref, w_ref, ada_ref, fw_ref, o_ref, *, d):
    y = jnp.dot((z_ref[0] * sg_ref[0]).astype(BF16), w_ref[...], preferred_element_type=F32)
    x = x_ref[0] + ada_ref[0][:, 2 * d:3 * d] * y
    o_ref[0] = _row_rms(x) * fw_ref[...]


def _out1(x, z, sg, w, ada, fw, *, tl):
    b, l, d = x.shape
    row = lambda bi, i: (bi, i, 0)
    blk = pl.BlockSpec((1, tl, d), row)
    return pl.pallas_call(
        functools.partial(_out1_kernel, d=d),
        grid=(b, l // tl),
        in_specs=[blk, blk, blk, pl.BlockSpec(w.shape, lambda bi, i: (0, 0)),
                  pl.BlockSpec((1, 1, ada.shape[-1]), lambda bi, i: (bi, 0, 0)),
                  pl.BlockSpec(fw.shape, lambda bi, i: (0, 0))],
        out_specs=blk,
        out_shape=jax.ShapeDtypeStruct((b, l, d), F32),
        compiler_params=_cparams(("arbitrary", "arbitrary")),
        name="out1",
    )(x, z, sg, w, ada, fw)


def _pack_attn_w_in(w):
    d = w.shape[0]
    o = 0
    wq = w[:, o:o + 512]; o += 512
    wk = w[:, o:o + 128]; o += 128
    wv = w[:, o:o + 128]; o += 128
    wcq = w[:, o:o + MLA_Q_RANK]; o += MLA_Q_RANK
    wckv = w[:, o:o + MLA_KV_RANK]; o += MLA_KV_RANK
    wkpe = w[:, o:o + MLA_ROPE_DIM]; o += MLA_ROPE_DIM
    wg = w[:, o:]
    z64 = jnp.zeros((d, 64), w.dtype)
    ka, va = [], []
    for g in range(GQA_KV_HEADS):
        kg = wk[:, 64 * g:64 * (g + 1)]
        vg = wv[:, 64 * g:64 * (g + 1)]
        ka += [kg, z64, z64, kg]
        va += [vg, vg]
    kpe = jnp.concatenate([z64, wkpe, jnp.zeros((d, 32), w.dtype)], axis=1)
    kv_part = jnp.concatenate(ka + va + [wckv, kpe], axis=1)
    lat = jnp.concatenate([wq] + ka + va + [wcq, wckv, kpe, wg], axis=1)
    return lat.astype(BF16), kv_part.astype(BF16)


def _pack_mla_up(w_uq, w_ukv):
    dq = MLA_NOPE_DIM + MLA_ROPE_DIM
    pad = LANE - dq
    uq = jnp.concatenate(
        [jnp.concatenate([w_uq[:, dq * h:dq * (h + 1)], jnp.zeros((w_uq.shape[0], pad), w_uq.dtype)], axis=1)
         for h in range(MLA_HEADS)], axis=1)
    dkv = MLA_NOPE_DIM + MLA_V_DIM
    kn = jnp.concatenate(
        [jnp.concatenate([w_ukv[:, dkv * h:dkv * h + MLA_NOPE_DIM],
                          jnp.zeros((w_ukv.shape[0], LANE - MLA_NOPE_DIM), w_ukv.dtype)], axis=1)
         for h in range(MLA_HEADS)], axis=1)
    vm = jnp.concatenate([w_ukv[:, dkv * h + MLA_NOPE_DIM:dkv * (h + 1)] for h in range(MLA_HEADS)], axis=1)
    return uq.astype(BF16), jnp.concatenate([kn, vm], axis=1).astype(BF16)


def _rope_tables(l):
    rows = (jnp.arange(l, dtype=jnp.int32) // GRID_W).astype(F32)[:, None]
    cols = (jnp.arange(l, dtype=jnp.int32) % GRID_W).astype(F32)[:, None]

    def tab(rot_dim):
        q = rot_dim // 4
        inv = ROPE_BASE ** (-jnp.arange(q, dtype=F32) / q)
        ar, ac = rows * inv, cols * inv
        cos = jnp.concatenate([jnp.cos(ar)] * 2 + [jnp.cos(ac)] * 2, axis=1)
        sin = jnp.concatenate([-jnp.sin(ar), jnp.sin(ar), -jnp.sin(ac), jnp.sin(ac)], axis=1)
        return cos, sin

    ca, sa = tab(GQA_HEAD_DIM)
    ca = jnp.concatenate([ca] * (_QA_W // GQA_HEAD_DIM), axis=1)
    sa = jnp.concatenate([sa] * (_QA_W // GQA_HEAD_DIM), axis=1)
    cm, sm = tab(MLA_ROPE_DIM)
    one, zero = jnp.ones((l, 1), F32), jnp.zeros((l, 1), F32)
    cm = jnp.concatenate([jnp.tile(one, (1, 64)), cm, jnp.tile(one, (1, 32))], axis=1)
    sm = jnp.concatenate([jnp.tile(zero, (1, 64)), sm, jnp.tile(zero, (1, 32))], axis=1)
    return ca, sa, cm, sm


def _pad2(a, r, c):
    return jnp.pad(a, ((0, r - a.shape[0]), (0, c - a.shape[1])))


def kernel(x, c, ctx, c_ctx, ada_w, ada_b, norm_w, attn_w_in, attn_q_norm, attn_k_norm, mla_q_norm, mla_kv_norm, mla_w_uq, mla_w_ukv, attn_w_out, hy_w_in, hy_conv_w, hy_conv_b, hy_ffn_w1, hy_ffn_b1, hy_ffn_w2, hy_ffn_b2, hy_ffn_w3, hy_ffn_b3, hy_freq, hy_skip, hy_w_out, final_norm_w):
    b, l, d = x.shape
    lc = ctx.shape[1]
    tl = min(256, l)

    rows = -(-(b + 1) // 8) * 8
    cs = jnp.concatenate([c, c_ctx[None, :], jnp.zeros((rows - b - 1, d), F32)], axis=0)
    ada = _ada(cs, ada_w, ada_b)
    ada_lat = [ada[i, :b].reshape(b, 1, 3 * d) for i in range(ada.shape[0])]
    ada_ctx0 = jnp.broadcast_to(ada[0, b].reshape(1, 1, 3 * d), (b, 1, 3 * d))

    w_lat, w_kv = _pack_attn_w_in(attn_w_in[0])
    wuq, wukv = _pack_mla_up(mla_w_uq[0], mla_w_ukv[0])
    nw0 = norm_w[0].reshape(1, d)
    qn = jnp.tile(attn_q_norm[0], _QA_W // GQA_HEAD_DIM).reshape(1, _QA_W)
    kn = jnp.tile(attn_k_norm[0], _KA_W // GQA_HEAD_DIM).reshape(1, _KA_W)
    cqn = mla_q_norm[0].reshape(1, MLA_Q_RANK)
    ckvn = mla_kv_norm[0].reshape(1, MLA_KV_RANK)
    tabs = _rope_tables(l)
    qa, ka, va, qm, km, vm, sg = _prep(x, ada_lat[0], nw0, w_lat, qn, kn, cqn, ckvn, wuq, wukv, tabs,
                                       latent=True, tl=tl)
    kac, vac, kmc, vmc = _prep(ctx, ada_ctx0, nw0, w_kv, None, kn, None, ckvn, None, wukv, None,
                               latent=False, tl=min(tl, lc))
    tq, tk = min(1024, l), min(512, l // 2)
    oa = _attention(qa, kac, vac, ka, va, q_shared=True, kv_group=2, tq=tq, tk=tk)
    om = _attention(qm, kmc, vmc, km, vm, q_shared=False, kv_group=1, tq=tq, tk=tk)
    x1 = _out0(x, oa, om, sg, attn_w_out[0].astype(BF16), ada_lat[0], tl=tl)

    nw1 = norm_w[1].reshape(1, d)
    u = _hyin(x1, ada_lat[1], nw1, hy_w_in[0].astype(BF16), hy_conv_w[0], hy_conv_b[0].reshape(1, -1), tl=tl)
    n1 = 2 * l // DFT_N2
    n1h = n1 // 2
    tabs_d = _dft_tables(l)

    t = jnp.linspace(0.0, 1.0, l, dtype=F32)[:, None]
    wpos = (2.0 * math.pi / l) * jnp.arange(l, dtype=F32)[:, None]
    bands = jnp.linspace(1e-4, HY_BANDS - 1, HY_BANDS, dtype=F32)
    emb = jnp.concatenate([t, jnp.cos(wpos * bands), -jnp.sin(wpos * bands)], axis=-1)
    deltas = jnp.abs(jnp.linspace(math.log(HY_DECAY_TARGET) / HY_SLOW_DECAY,
                                  math.log(HY_DECAY_TARGET) / HY_FAST_DECAY, d, dtype=F32)).reshape(1, d)
    wf = hy_ffn_w3.shape[-1]
    hw, asum = _filters(_pad2(emb, l, LANE), _pad2(hy_ffn_w1[0], LANE, LANE), _pad2(hy_ffn_b1[0][None], 1, LANE),
                        _pad2(hy_ffn_w2[0], LANE, LANE), _pad2(hy_ffn_b2[0][None], 1, LANE),
                        _pad2(hy_ffn_w3[0], LANE, wf), hy_ffn_b3[0][None], _pad2(hy_freq[0][None], 1, LANE),
                        deltas, tl=tl)
    hw = hw.reshape(l, HY_ORDER, 2, d)
    fwd, bwd = hw[:, :, 0], hw[:, :, 1]
    kern = jnp.concatenate([fwd, jnp.zeros((1, HY_ORDER, d), F32), bwd[:0:-1]], axis=0)
    asum = asum.reshape(HY_ORDER, 2, d)
    l1 = asum[:, 0] + asum[:, 1] - jnp.abs(bwd[0])
    oc = HY_ORDER * d
    ct = 512
    af = _outer_fwd(kern.reshape(1, n1, DFT_N2, oc), real_input=True, tn2=32, ct=256)
    kf = _mid(af, tabs_d[0], scale=(1.0 / (l1 * (2 * l))).reshape(1, oc), ct=ct)

    v2, x1g, x2g = (_pair_view(u[i], n1h) for i in range(3))
    z = _long_conv(v2, x1g, hy_skip[0, 0:1], kf, 0, tabs_d, ct=ct)
    z = _long_conv(z, x2g, hy_skip[0, 1:2], kf, d // ct, tabs_d, ct=ct)
    z = z.reshape(b, l, d)
    return _out1(x1, z, u[3], hy_w_out[0].astype(BF16), ada_lat[1], final_norm_w.reshape(1, d), tl=tl)
```

```python
import functools
import math

import numpy as np
import jax
import jax.numpy as jnp
from jax import lax
from jax.experimental import pallas as pl
from jax.experimental.pallas import tpu as pltpu

EPS = 1e-6
GRID_W = 64
ROPE_BASE = 10000.0
GQA_HEADS, GQA_KV_HEADS, GQA_HEAD_DIM = 8, 2, 64
MLA_HEADS, MLA_Q_RANK, MLA_KV_RANK = 8, 256, 128
MLA_NOPE_DIM, MLA_ROPE_DIM, MLA_V_DIM = 64, 32, 64
HY_ORDER, HY_SHORT, HY_BANDS, HY_FFN = 2, 3, 16, 64
HY_FAST_DECAY, HY_SLOW_DECAY, HY_DECAY_TARGET = 0.3, 1.5, 1e-2
LANE = 128
DFT_N2 = 128
VMEM_LIMIT = 56 * 1024 * 1024
LOG2E = 1.4426950408889634
HI = lax.Precision.HIGHEST
F32 = jnp.float32
BF16 = jnp.bfloat16


def _cparams(sem):
    return pltpu.CompilerParams(dimension_semantics=sem, vmem_limit_bytes=VMEM_LIMIT)


def _per_chunk(fn, *arrs):
    width = arrs[0].shape[-1]
    outs = [fn(*[a[:, c:c + LANE] for a in arrs]) for c in range(0, width, LANE)]
    return outs[0] if len(outs) == 1 else jnp.concatenate(outs, axis=-1)


def _lane_iota(shape):
    return lax.broadcasted_iota(jnp.int32, shape, len(shape) - 1)


def _head_rsqrt(x, ones2):
    ss = x * x
    hi = ss.astype(BF16)
    lo = (ss - hi.astype(F32)).astype(BF16)
    tot = jnp.dot(jnp.concatenate([hi, lo], axis=-1), ones2, preferred_element_type=F32)
    return lax.rsqrt(tot * (1.0 / GQA_HEAD_DIM) + EPS)


def _rope(x, x_sw, cos, sin_signed):
    return _per_chunk(lambda c, w: c * cos + w * sin_signed, x, x_sw)


def _row_rms(x):
    return x * lax.rsqrt(jnp.mean(x * x, axis=-1, keepdims=True) + EPS)


def _silu(x):
    return x * (1.0 / (1.0 + jnp.exp(-x)))


def _ada_kernel(c_ref, w_ref, b_ref, o_ref):
    s = _silu(c_ref[...])
    o_ref[0] = jnp.dot(s, w_ref[0], precision=HI, preferred_element_type=F32) + b_ref[0]


def _ada(cs, ada_w, ada_b):
    depth, d, d3 = ada_w.shape
    rows = cs.shape[0]
    nt = d3 // d
    return pl.pallas_call(
        _ada_kernel,
        grid=(depth, nt),
        in_specs=[pl.BlockSpec((rows, d), lambda i, j: (0, 0)),
                  pl.BlockSpec((1, d, d), lambda i, j: (i, 0, j)),
                  pl.BlockSpec((1, 1, d), lambda i, j: (i, 0, j))],
        out_specs=pl.BlockSpec((1, rows, d), lambda i, j: (i, 0, j)),
        out_shape=jax.ShapeDtypeStruct((depth, rows, d3), F32),
        compiler_params=_cparams(("arbitrary", "arbitrary")),
        name="ada",
    )(cs, ada_w, ada_b.reshape(depth, 1, d3))


_QA_W, _KA_W, _VA_W, _KPE_W = 512, 512, 256, 128


def _prep_kernel(*refs, latent, d):
    if latent:
        (x_ref, ada_ref, nw_ref, w_ref, qn_ref, kn_ref, cqn_ref, ckvn_ref, wuq_ref, wukv_ref, o2q_ref, o2k_ref,
         ca_ref, sa_ref, cm_ref, sm_ref,
         qa_o, ka_o, va_o, qm_o, km_o, vm_o, sg_o) = refs
    else:
        (x_ref, ada_ref, nw_ref, w_ref, kn_ref, ckvn_ref, wukv_ref, o2k_ref,
         ka_o, va_o, km_o, vm_o) = refs
    ada = ada_ref[0]
    shift, scale = ada[:, :d], ada[:, d:2 * d]
    h = (_row_rms(x_ref[0]) * nw_ref[...]) * (1.0 + scale) + shift
    p = jnp.dot(h.astype(BF16), w_ref[...], preferred_element_type=F32)
    off = 0

    def take(width):
        nonlocal off
        off += width
        return p[:, off - width:off]

    if latent:
        qa, qa_sw = take(_QA_W), take(_QA_W)
        k, k_sw = take(LANE), take(LANE)
    else:
        k = take(LANE)
    v = take(LANE)
    if latent:
        cq = take(MLA_Q_RANK)
    ckv = take(MLA_KV_RANK)
    kpe = take(LANE)

    kn = kn_ref[...]
    rk = _head_rsqrt(k, o2k_ref[...])
    k = k * rk * kn[0:1]
    if latent:
        k = _rope(k, k_sw * rk * kn[1:2], ca_ref[...], sa_ref[...])
        kpe = _rope(kpe, take(LANE), cm_ref[...], sm_ref[...])
    low = _lane_iota(k.shape) < (LANE // 2)
    k_x = pltpu.roll(k, LANE // 2, axis=1)
    v_x = pltpu.roll(v, LANE // 2, axis=1)
    zero = jnp.zeros_like(k)
    ka = [jnp.where(low, k, zero), jnp.where(low, zero, k_x), jnp.where(low, k_x, zero), jnp.where(low, zero, k)]
    ka_o[0] = jnp.concatenate(ka, axis=-1).astype(BF16)
    one64 = (_lane_iota(k.shape) == LANE // 2).astype(F32)
    va = jnp.concatenate([jnp.where(low, v, one64), jnp.where(low, v_x, one64)], axis=-1)
    va_o[0] = va.T.astype(BF16)
    ckv_n = (_row_rms(ckv) * ckvn_ref[...]).astype(BF16)
    kv = jnp.dot(ckv_n, wukv_ref[...], preferred_element_type=F32)
    nk = MLA_HEADS * LANE
    km_o[0] = _per_chunk(lambda c: c + kpe, kv[:, :nk]).astype(BF16)
    vm_o[0] = _per_chunk(lambda c: c + one64, kv[:, nk:]).T.astype(BF16)
    if latent:
        qn = qn_ref[...]
        rq = _head_rsqrt(qa, o2q_ref[...])
        qa = _rope(qa * rq * qn[0:1], qa_sw * rq * qn[1:2], ca_ref[...], sa_ref[...])
        qa_o[0] = (qa * (GQA_HEAD_DIM ** -0.5 * LOG2E)).astype(BF16)
        cq_n = (_row_rms(cq) * cqn_ref[...]).astype(BF16)
        qm = jnp.dot(cq_n, wuq_ref[...], preferred_element_type=F32)
        qm = _rope(qm[:, :nk], qm[:, nk:], cm_ref[...], sm_ref[...])
        qm_o[0] = (qm * ((MLA_NOPE_DIM + MLA_ROPE_DIM) ** -0.5 * LOG2E)).astype(BF16)
        sg_o[0] = _silu(take(d)).astype(BF16)


def _prep(x, ada, nw, w, qn, kn, cqn, ckvn, wuq, wukv, o2q, o2k, tabs, *, latent, tl):
    b, l, d = x.shape
    grid = (l // tl, b)
    row = lambda i, j: (j, i, 0)
    const = lambda i, j: (0, 0)
    tab = lambda i, j: (i, 0)
    xspec = pl.BlockSpec((1, tl, d), row)
    adaspec = pl.BlockSpec((1, 1, ada.shape[-1]), lambda i, j: (j, 0, 0))

    def full(a):
        return pl.BlockSpec(a.shape, const)

    def out(width):
        return (pl.BlockSpec((1, tl, width), row), jax.ShapeDtypeStruct((b, l, width), BF16))

    def out_t(width):
        return (pl.BlockSpec((1, width, tl), lambda i, j: (j, 0, i)), jax.ShapeDtypeStruct((b, width, l), BF16))

    if latent:
        ca, sa, cm, sm = tabs
        ins = [x, ada, nw, w, qn, kn, cqn, ckvn, wuq, wukv, o2q, o2k, ca, sa, cm, sm]
        in_specs = [xspec, adaspec, full(nw), full(w), full(qn), full(kn), full(cqn), full(ckvn),
                    full(wuq), full(wukv), full(o2q), full(o2k)] + [pl.BlockSpec((tl, t.shape[1]), tab) for t in tabs]
        outs = [out(_QA_W), out(_KA_W), out_t(_VA_W), out(MLA_HEADS * LANE), out(MLA_HEADS * LANE),
                out_t(MLA_HEADS * LANE), out(d)]
    else:
        ins = [x, ada, nw, w, kn, ckvn, wukv, o2k]
        in_specs = [xspec, adaspec, full(nw), full(w), full(kn), full(ckvn), full(wukv), full(o2k)]
        outs = [out(_KA_W), out_t(_VA_W), out(MLA_HEADS * LANE), out_t(MLA_HEADS * LANE)]
    return pl.pallas_call(
        functools.partial(_prep_kernel, latent=latent, d=d),
        grid=grid, in_specs=in_specs,
        out_specs=[o[0] for o in outs], out_shape=[o[1] for o in outs],
        compiler_params=_cparams(("arbitrary", "arbitrary")),
        name="prep_lat" if latent else "prep_ctx",
    )(*ins)


def _attn_kernel(q_ref, kc_ref, vc_ref, kl_ref, vl_ref, o_ref, s_a, s_b, *, tk, q_shared):
    tq = q_ref.shape[1]
    lk = kl_ref.shape[1]
    nblk = lk // tk
    dn = (((1,), (1,)), ((), ()))
    qs = [q_ref[0, :, 0:LANE] if q_shared else q_ref[0, :, e * LANE:(e + 1) * LANE] for e in range(2)]

    def scores(e, k):
        return lax.dot_general(k, qs[e], dn, preferred_element_type=F32)

    def scores_at(e, r):
        return scores(e, kl_ref[0, pl.ds(r, tk), e * LANE:(e + 1) * LANE])

    def vrows(v_ref, e, cols):
        return v_ref[0, :, cols] if q_shared else v_ref[0, e * LANE:(e + 1) * LANE, cols]

    def update(s, vt, carry):
        m, acc = carry
        m_new = jnp.maximum(m, jnp.max(s, axis=0, keepdims=True))
        p = jnp.exp2(s - m_new)
        acc = jnp.exp2(m - m_new) * acc + jnp.dot(vt, p.astype(BF16), preferred_element_type=F32)
        return m_new, acc

    init = (jnp.full((1, tq), -1e30, F32), jnp.zeros((LANE, tq), F32))
    s_ctx = [scores(e, kc_ref[0, :, e * LANE:(e + 1) * LANE]) for e in range(2)]
    for e in range(2):
        s_a[e] = scores_at(e, 0)
    carry = tuple(update(s_ctx[e], vrows(vc_ref, e, slice(None)), init) for e in range(2))

    def body(j, carry):
        r0 = pl.multiple_of(2 * j * tk, tk)
        r1 = pl.multiple_of(r0 + tk, tk)
        r2 = pl.multiple_of(jnp.minimum(r1 + tk, lk - tk), tk)
        for e in range(2):
            s_b[e] = scores_at(e, r1)
        carry = tuple(update(s_a[e], vrows(vl_ref, e, pl.ds(r0, tk)), carry[e]) for e in range(2))
        for e in range(2):
            s_a[e] = scores_at(e, r2)
        return tuple(update(s_b[e], vrows(vl_ref, e, pl.ds(r1, tk)), carry[e]) for e in range(2))

    carry = lax.fori_loop(0, nblk // 2, body, carry)
    dv = LANE // 2
    outs = [acc[:dv] / acc[dv:dv + 1] for (_, acc) in carry]
    o_ref[0] = jnp.concatenate(outs, axis=0).T.astype(o_ref.dtype)


def _attention(q, kc, vc, kl, vl, *, q_shared, kv_group, tq, tk):
    b, l, _ = q.shape
    wq = LANE if q_shared else 2 * LANE
    pairs = q.shape[-1] // wq
    lc = kc.shape[1]
    assert (l // tk) % 2 == 0, "key blocks are consumed two per loop trip"
    vw = LANE if q_shared else 2 * LANE
    kv = lambda bi, j, i: (bi, 0, j // kv_group)
    vt = lambda bi, j, i: (bi, j // kv_group, 0)
    return pl.pallas_call(
        functools.partial(_attn_kernel, tk=tk, q_shared=q_shared),
        grid=(b, pairs, l // tq),
        in_specs=[pl.BlockSpec((1, tq, wq), lambda bi, j, i: (bi, i, j)),
                  pl.BlockSpec((1, lc, 2 * LANE), kv), pl.BlockSpec((1, vw, lc), vt),
                  pl.BlockSpec((1, l, 2 * LANE), kv), pl.BlockSpec((1, vw, l), vt)],
        out_specs=pl.BlockSpec((1, tq, LANE), lambda bi, j, i: (bi, i, j)),
        out_shape=jax.ShapeDtypeStruct((b, l, pairs * LANE), BF16),
        scratch_shapes=[pltpu.VMEM((2, tk, tq), F32), pltpu.VMEM((2, tk, tq), F32)],
        compiler_params=_cparams(("arbitrary", "arbitrary", "arbitrary")),
        name="attn_gqa" if q_shared else "attn_mla",
    )(q, kc, vc, kl, vl)


def _out0_kernel(x_ref, oa_ref, om_ref, sg_ref, w_ref, ada_ref, o_ref, *, d):
    o = jnp.concatenate([oa_ref[0], om_ref[0]], axis=-1).astype(F32) * sg_ref[0].astype(F32)
    y = jnp.dot(o.astype(BF16), w_ref[...], preferred_element_type=F32)
    o_ref[0] = x_ref[0] + ada_ref[0][:, 2 * d:3 * d] * y


def _out0(x, oa, om, sg, w, ada, *, tl):
    b, l, d = x.shape
    row = lambda bi, i: (bi, i, 0)
    return pl.pallas_call(
        functools.partial(_out0_kernel, d=d),
        grid=(b, l // tl),
        in_specs=[pl.BlockSpec((1, tl, d), row), pl.BlockSpec((1, tl, oa.shape[-1]), row),
                  pl.BlockSpec((1, tl, om.shape[-1]), row), pl.BlockSpec((1, tl, d), row),
                  pl.BlockSpec(w.shape, lambda bi, i: (0, 0)),
                  pl.BlockSpec((1, 1, ada.shape[-1]), lambda bi, i: (bi, 0, 0))],
        out_specs=pl.BlockSpec((1, tl, d), row),
        out_shape=jax.ShapeDtypeStruct((b, l, d), F32),
        compiler_params=_cparams(("arbitrary", "arbitrary")),
        name="out0",
    )(x, oa, om, sg, w, ada)


_HALO = 8


def _hyin_kernel(x_ref, xp_ref, xn_ref, ada_ref, nw_ref, w_ref, cw_ref, cb_ref, *o_refs, d):
    i = pl.program_id(1)
    tl = x_ref.shape[1]
    n_conv = len(o_refs) - 1
    ada = ada_ref[0]
    shift, scale = ada[:, :d], ada[:, d:2 * d]

    def mod(x):
        return (_row_rms(x) * nw_ref[...]) * (1.0 + scale) + shift

    hp = mod(xp_ref[0]) * (i > 0).astype(F32)
    hn = mod(xn_ref[0]) * (i < pl.num_programs(1) - 1).astype(F32)
    h = jnp.concatenate([mod(x_ref[0]), hp, hn], axis=0).astype(BF16)
    rows = lax.broadcasted_iota(jnp.int32, (tl, d), 0)
    cw = cw_ref[...]
    for n in range(n_conv + 1):
        p = jnp.dot(h, w_ref[:, n * d:(n + 1) * d], preferred_element_type=F32)
        pm = p[0:tl]
        if n < n_conv:
            prev = jnp.where(rows == 0, p[tl + _HALO - 1:tl + _HALO], pltpu.roll(pm, 1, axis=0))
            nxt = jnp.where(rows == tl - 1, p[tl + _HALO:tl + _HALO + 1], pltpu.roll(pm, tl - 1, axis=0))
            c0, c1, c2 = (cw[j:j + 1, n * d:(n + 1) * d] for j in range(HY_SHORT))
            o_refs[n][0] = prev * c0 + pm * c1 + nxt * c2 + cb_ref[:, n * d:(n + 1) * d]
        else:
            o_refs[n][0] = _silu(pm)


def _hyin(x, ada, nw, w, cw, cb, *, tl):
    b, l, d = x.shape
    ng = w.shape[1] // d
    tb = tl // _HALO
    nb = l // _HALO
    const = lambda bi, i: (0, 0)
    blk = pl.BlockSpec((1, tl, d), lambda bi, i: (bi, i, 0))
    return pl.pallas_call(
        functools.partial(_hyin_kernel, d=d),
        grid=(b, l // tl),
        in_specs=[blk,
                  pl.BlockSpec((1, _HALO, d), lambda bi, i: (bi, jnp.maximum(i * tb - 1, 0), 0)),
                  pl.BlockSpec((1, _HALO, d), lambda bi, i: (bi, jnp.minimum((i + 1) * tb, nb - 1), 0)),
                  pl.BlockSpec((1, 1, ada.shape[-1]), lambda bi, i: (bi, 0, 0)),
                  pl.BlockSpec(nw.shape, const), pl.BlockSpec(w.shape, const),
                  pl.BlockSpec(cw.shape, const), pl.BlockSpec(cb.shape, const)],
        out_specs=[blk] * ng,
        out_shape=[jax.ShapeDtypeStruct((b, l, d), F32)] * ng,
        compiler_params=_cparams(("arbitrary", "arbitrary")),
        name="hy_in",
    )(x, x, x, ada, nw, w, cw, cb)


def _filt_kernel(emb_ref, w1_ref, b1_ref, w2_ref, b2_ref, w3_ref, b3_ref, fr_ref, dl_ref, h_o, s_o, *, reps):
    emb = emb_ref[...]
    fr = fr_ref[...]
    hid = jnp.sin(fr * (jnp.dot(emb, w1_ref[...], precision=HI, preferred_element_type=F32) + b1_ref[...]))
    hid = jnp.sin(fr * (jnp.dot(hid, w2_ref[...], precision=HI, preferred_element_type=F32) + b2_ref[...]))
    h = jnp.dot(hid, w3_ref[...], precision=HI, preferred_element_type=F32) + b3_ref[...]
    win = jnp.exp(-emb[:, 0:1] * dl_ref[...])
    hw = h * jnp.concatenate([win] * reps, axis=-1)
    h_o[...] = hw

    @pl.when(pl.program_id(0) == 0)
    def _():
        s_o[...] = jnp.zeros_like(s_o)

    s_o[...] += jnp.sum(jnp.abs(hw), axis=0, keepdims=True)


def _filters(emb, w1, b1, w2, b2, w3, b3, fr, dl, *, tl):
    l = emb.shape[0]
    wo = w3.shape[1]
    const = lambda i: (0, 0)
    full = lambda a: pl.BlockSpec(a.shape, const)
    return pl.pallas_call(
        functools.partial(_filt_kernel, reps=wo // dl.shape[1]),
        grid=(l // tl,),
        in_specs=[pl.BlockSpec((tl, emb.shape[1]), lambda i: (i, 0)), full(w1), full(b1), full(w2), full(b2),
                  full(w3), full(b3), full(fr), full(dl)],
        out_specs=[pl.BlockSpec((tl, wo), lambda i: (i, 0)), pl.BlockSpec((1, wo), const)],
        out_shape=[jax.ShapeDtypeStruct((l, wo), F32), jax.ShapeDtypeStruct((1, wo), F32)],
        compiler_params=_cparams(("arbitrary",)),
        name="hy_filter",
    )(emb, w1, b1, w2, b2, w3, b3, fr, dl)


def _cmul_const(a, ang):
    ar, ai = a
    q = ang / (0.5 * math.pi)
    if abs(q - round(q)) < 1e-12:
        return [(ar, ai), (-ai, ar), (-ar, -ai), (ai, -ar)][int(round(q)) % 4]
    c, s = math.cos(ang), math.sin(ang)
    return (ar * c - ai * s, ar * s + ai * c)


def _fft_dif(x):
    x = list(x)
    n = len(x)
    half = n // 2
    while half >= 1:
        for base in range(0, n, 2 * half):
            for j in range(half):
                a, b = x[base + j], x[base + j + half]
                ang = -math.pi * j / half
                if b is None:
                    x[base + j + half] = None if a is None else _cmul_const(a, ang)
                else:
                    x[base + j] = (a[0] + b[0], a[1] + b[1])
                    x[base + j + half] = _cmul_const((a[0] - b[0], a[1] - b[1]), ang)
        half //= 2
    return x


def _ifft_dit(x, keep):
    x = list(x)
    n = len(x)
    half = 1
    while half <= n // 2:
        last = half == n // 2
        for base in range(0, n, 2 * half):
            for j in range(half):
                a = x[base + j]
                b = _cmul_const(x[base + j + half], math.pi * j / half)
                x[base + j] = (a[0] + b[0], a[1] + b[1])
                if not last or base + j + half < keep:
                    x[base + j + half] = (a[0] - b[0], a[1] - b[1])
        half *= 2
    return x[:keep]


def _outer_fwd_kernel(u_ref, o_ref, *, real_input):
    n1 = o_ref.shape[2]
    if real_input:
        x = [(u_ref[0, i], jnp.zeros_like(u_ref[0, i])) for i in range(n1)]
    else:
        x = [(u_ref[0, 0, i], u_ref[0, 1, i]) for i in range(n1 // 2)] + [None] * (n1 // 2)
    for s, (re, im) in enumerate(_fft_dif(x)):
        o_ref[0, 0, s] = re
        o_ref[0, 1, s] = im


def _outer_fwd(u, *, real_input, tn2, ct):
    p, c = u.shape[0], u.shape[-1]
    n2 = u.shape[-2]
    n1 = u.shape[1] if real_input else 2 * u.shape[2]
    if real_input:
        in_spec = pl.BlockSpec((1, n1, tn2, ct), lambda pi, r, j: (pi, 0, r, j))
    else:
        in_spec = pl.BlockSpec((1, 2, n1 // 2, tn2, ct), lambda pi, r, j: (pi, 0, 0, r, j))
    return pl.pallas_call(
        functools.partial(_outer_fwd_kernel, real_input=real_input),
        grid=(p, n2 // tn2, c // ct), in_specs=[in_spec],
        out_specs=pl.BlockSpec((1, 2, n1, tn2, ct), lambda pi, r, j: (pi, 0, 0, r, j)),
        out_shape=jax.ShapeDtypeStruct((p, 2, n1, n2, c), F32),
        compiler_params=_cparams(("arbitrary", "arbitrary", "arbitrary")),
        name="hy_outer_filt" if real_input else "hy_outer_fwd",
    )(u)


def _outer_inv_kernel(a_ref, u_ref, g_ref, sk_ref, o_ref):
    n1 = a_ref.shape[2]
    y = _ifft_dit([(a_ref[0, 0, s], a_ref[0, 1, s]) for s in range(n1)], n1 // 2)
    sk = sk_ref[...]
    for i, (re, im) in enumerate(y):
        o_ref[0, 0, i] = g_ref[0, 0, i] * (re + u_ref[0, 0, i] * sk)
        o_ref[0, 1, i] = g_ref[0, 1, i] * (im + u_ref[0, 1, i] * sk)


def _outer_inv(a, u, gate, skip, *, tn2, ct):
    p, _, n1, n2, c = a.shape
    blk = lambda rows: pl.BlockSpec((1, 2, rows, tn2, ct), lambda pi, r, j: (pi, 0, 0, r, j))
    return pl.pallas_call(
        _outer_inv_kernel,
        grid=(p, n2 // tn2, c // ct),
        in_specs=[blk(n1), blk(n1 // 2), blk(n1 // 2), pl.BlockSpec((1, ct), lambda pi, r, j: (0, j))],
        out_specs=blk(n1 // 2),
        out_shape=jax.ShapeDtypeStruct(u.shape, F32),
        compiler_params=_cparams(("arbitrary", "arbitrary", "arbitrary")),
        name="hy_outer_inv",
    )(a, u, gate, skip)


def _dot3(m3, x):
    hi = x.astype(BF16)
    lo = (x - hi.astype(F32)).astype(BF16)
    return jnp.dot(m3, jnp.concatenate([hi, hi, lo], axis=0), preferred_element_type=F32)


def _mid_kernel(a_ref, g_ref, *rest, spectrum):
    np_, n2, ct = a_ref.shape[0], a_ref.shape[3], a_ref.shape[4]
    xs = [_dot3(g_ref[0], a_ref[p, :, 0].reshape(2 * n2, ct)) for p in range(np_)]
    if spectrum:
        sc_ref, o_ref = rest
        for p in range(np_):
            o_ref[p, :, 0] = (xs[p] * sc_ref[...]).reshape(2, n2, ct)
    else:
        gt_ref, kf_ref, o_ref = rest
        kr, ki = kf_ref[0, 0, 0], kf_ref[0, 1, 0]
        ys = [jnp.concatenate([x[:n2] * kr - x[n2:] * ki, x[:n2] * ki + x[n2:] * kr], axis=0) for x in xs]
        for p in range(np_):
            o_ref[p, :, 0] = _dot3(gt_ref[0], ys[p]).reshape(2, n2, ct)


def _mid(a, g, gt=None, kf=None, scale=None, *, ct, kf_col0=0):
    p, _, n1, n2, c = a.shape
    ablk = pl.BlockSpec((p, 2, 1, n2, ct), lambda k, j: (0, 0, k, 0, j))
    gblk = pl.BlockSpec((1,) + g.shape[1:], lambda k, j: (k, 0, 0))
    if kf is None:
        ins = [a, g, scale]
        in_specs = [ablk, gblk, pl.BlockSpec((1, ct), lambda k, j: (0, j))]
    else:
        ins = [a, g, gt, kf]
        in_specs = [ablk, gblk, gblk,
                    pl.BlockSpec((1, 2, 1, n2, ct), lambda k, j: (0, 0, k, 0, kf_col0 + j))]
    return pl.pallas_call(
        functools.partial(_mid_kernel, spectrum=kf is None),
        grid=(n1, c // ct), in_specs=in_specs, out_specs=ablk,
        out_shape=jax.ShapeDtypeStruct(a.shape, F32),
        compiler_params=_cparams(("arbitrary", "arbitrary")),
        name="hy_spectrum" if kf is None else "hy_mid",
    )(*ins)


def _dft_tables(l):
    n = 2 * l
    n2 = DFT_N2
    n1 = n // n2
    bits = n1.bit_length() - 1
    k1 = np.array([int(format(s, "0%db" % bits)[::-1], 2) for s in range(n1)], dtype=np.float64)
    kk = k1[:, None, None] + n1 * np.arange(n2, dtype=np.float64)[None, :, None]
    th = 2.0 * np.pi * kk * np.arange(n2, dtype=np.float64)[None, None, :] / n
    c, s = np.cos(th), np.sin(th)
    g = np.concatenate([np.concatenate([c, s], axis=2), np.concatenate([-s, c], axis=2)], axis=1)

    def split3(m):
        m = jnp.asarray(m, F32)
        hi = m.astype(BF16)
        lo = (m - hi.astype(F32)).astype(BF16)
        return jnp.concatenate([hi, lo, hi], axis=2)

    return split3(g), split3(np.transpose(g, (0, 2, 1)))


def _pair_view(u, n1h):
    b, l, c = u.shape
    return u.reshape(b // 2, 2, n1h, l // n1h, c)


def _long_conv(u, gate, skip, kf, kf_col0, tabs, *, ct):
    g, gt = tabs
    a = _outer_fwd(u, real_input=False, tn2=32, ct=256)
    a = _mid(a, g, gt, kf, ct=ct, kf_col0=kf_col0)
    return _outer_inv(a, u, gate, skip, tn2=32, ct=256)


def _out1_kernel(x_ref, z_ref, sg_ref, w_ref, ada_ref, fw_ref, o_ref, *, d):
    y = jnp.dot((z_ref[0] * sg_ref[0]).astype(BF16), w_ref[...], preferred_element_type=F32)
    x = x_ref[0] + ada_ref[0][:, 2 * d:3 * d] * y
    o_ref[0] = _row_rms(x) * fw_ref[...]


def _out1(x, z, sg, w, ada, fw, *, tl):
    b, l, d = x.shape
    row = lambda bi, i: (bi, i, 0)
    blk = pl.BlockSpec((1, tl, d), row)
    return pl.pallas_call(
        functools.partial(_out1_kernel, d=d),
        grid=(b, l // tl),
        in_specs=[blk, blk, blk, pl.BlockSpec(w.shape, lambda bi, i: (0, 0)),
                  pl.BlockSpec((1, 1, ada.shape[-1]), lambda bi, i: (bi, 0, 0)),
                  pl.BlockSpec(fw.shape, lambda bi, i: (0, 0))],
        out_specs=blk,
        out_shape=jax.ShapeDtypeStruct((b, l, d), F32),
        compiler_params=_cparams(("arbitrary", "arbitrary")),
        name="out1",
    )(x, z, sg, w, ada, fw)


def _swap_cols(w, q):
    return w[..., np.arange(w.shape[-1]) ^ q]


def _pack_attn_w_in(w):
    d = w.shape[0]
    o = 0
    wq = w[:, o:o + 512]; o += 512
    wk = w[:, o:o + 128]; o += 128
    wv = w[:, o:o + 128]; o += 128
    wcq = w[:, o:o + MLA_Q_RANK]; o += MLA_Q_RANK
    wckv = w[:, o:o + MLA_KV_RANK]; o += MLA_KV_RANK
    wkpe = w[:, o:o + MLA_ROPE_DIM]; o += MLA_ROPE_DIM
    wg = w[:, o:]
    qa, qm = GQA_HEAD_DIM // 4, MLA_ROPE_DIM // 4

    def pe_chunk(wp):
        return jnp.concatenate([jnp.zeros((d, 64), w.dtype), wp, jnp.zeros((d, 32), w.dtype)], axis=1)

    kpe, kpe_sw = pe_chunk(wkpe), pe_chunk(_swap_cols(wkpe, qm))
    kv_part = jnp.concatenate([wk, wv, wckv, kpe], axis=1)
    lat = jnp.concatenate([wq, _swap_cols(wq, qa), wk, _swap_cols(wk, qa), wv, wcq, wckv, kpe, kpe_sw, wg], axis=1)
    return lat.astype(BF16), kv_part.astype(BF16)


def _pack_mla_up(w_uq, w_ukv):
    dq = MLA_NOPE_DIM + MLA_ROPE_DIM
    r = w_uq.shape[0]
    z = lambda n: jnp.zeros((r, n), w_uq.dtype)
    uq, uq_sw = [], []
    for h in range(MLA_HEADS):
        nope, pe = w_uq[:, dq * h:dq * h + MLA_NOPE_DIM], w_uq[:, dq * h + MLA_NOPE_DIM:dq * (h + 1)]
        uq += [nope, pe, z(LANE - dq)]
        uq_sw += [z(MLA_NOPE_DIM), _swap_cols(pe, MLA_ROPE_DIM // 4), z(LANE - dq)]
    dkv = MLA_NOPE_DIM + MLA_V_DIM
    kn = jnp.concatenate(
        [jnp.concatenate([w_ukv[:, dkv * h:dkv * h + MLA_NOPE_DIM],
                          jnp.zeros((w_ukv.shape[0], LANE - MLA_NOPE_DIM), w_ukv.dtype)], axis=1)
         for h in range(MLA_HEADS)], axis=1)
    vm = jnp.concatenate(
        [jnp.concatenate([w_ukv[:, dkv * h + MLA_NOPE_DIM:dkv * (h + 1)],
                          jnp.zeros((w_ukv.shape[0], LANE - MLA_V_DIM), w_ukv.dtype)], axis=1)
         for h in range(MLA_HEADS)], axis=1)
    return jnp.concatenate(uq + uq_sw, axis=1).astype(BF16), jnp.concatenate([kn, vm], axis=1).astype(BF16)


def _head_ones2(width):
    i = np.arange(width) // GQA_HEAD_DIM
    blk = (i[:, None] == i[None, :]).astype(np.float32)
    return jnp.asarray(np.concatenate([blk, blk], axis=0), BF16)


def _rope_tables(l):
    rows = (jnp.arange(l, dtype=jnp.int32) // GRID_W).astype(F32)[:, None]
    cols = (jnp.arange(l, dtype=jnp.int32) % GRID_W).astype(F32)[:, None]

    def tab(rot_dim):
        q = rot_dim // 4
        inv = ROPE_BASE ** (-jnp.arange(q, dtype=F32) / q)
        ar, ac = rows * inv, cols * inv
        cos = jnp.concatenate([jnp.cos(ar)] * 2 + [jnp.cos(ac)] * 2, axis=1)
        sin = jnp.concatenate([-jnp.sin(ar), jnp.sin(ar), -jnp.sin(ac), jnp.sin(ac)], axis=1)
        return cos, sin

    ca, sa = tab(GQA_HEAD_DIM)
    ca = jnp.concatenate([ca] * (LANE // GQA_HEAD_DIM), axis=1)
    sa = jnp.concatenate([sa] * (LANE // GQA_HEAD_DIM), axis=1)
    cm, sm = tab(MLA_ROPE_DIM)
    one, zero = jnp.ones((l, 1), F32), jnp.zeros((l, 1), F32)
    cm = jnp.concatenate([jnp.tile(one, (1, 64)), cm, jnp.tile(one, (1, 32))], axis=1)
    sm = jnp.concatenate([jnp.tile(zero, (1, 64)), sm, jnp.tile(zero, (1, 32))], axis=1)
    return ca, sa, cm, sm


def _pad2(a, r, c):
    return jnp.pad(a, ((0, r - a.shape[0]), (0, c - a.shape[1])))


def kernel(x, c, ctx, c_ctx, ada_w, ada_b, norm_w, attn_w_in, attn_q_norm, attn_k_norm, mla_q_norm, mla_kv_norm, mla_w_uq, mla_w_ukv, attn_w_out, hy_w_in, hy_conv_w, hy_conv_b, hy_ffn_w1, hy_ffn_b1, hy_ffn_w2, hy_ffn_b2, hy_ffn_w3, hy_ffn_b3, hy_freq, hy_skip, hy_w_out, final_norm_w):
    b, l, d = x.shape
    lc = ctx.shape[1]
    tl = min(256, l)

    rows = -(-(b + 1) // 8) * 8
    cs = jnp.concatenate([c, c_ctx[None, :], jnp.zeros((rows - b - 1, d), F32)], axis=0)
    ada = _ada(cs, ada_w, ada_b)
    ada_lat = [ada[i, :b].reshape(b, 1, 3 * d) for i in range(ada.shape[0])]
    ada_ctx0 = jnp.broadcast_to(ada[0, b].reshape(1, 1, 3 * d), (b, 1, 3 * d))

    w_lat, w_kv = _pack_attn_w_in(attn_w_in[0])
    wuq, wukv = _pack_mla_up(mla_w_uq[0], mla_w_ukv[0])
    nw0 = norm_w[0].reshape(1, d)
    def norm_rows(wn, width):
        sw = _swap_cols(wn, GQA_HEAD_DIM // 4)
        return jnp.stack([jnp.tile(wn, width // GQA_HEAD_DIM), jnp.tile(sw, width // GQA_HEAD_DIM)])

    qn, kn = norm_rows(attn_q_norm[0], _QA_W), norm_rows(attn_k_norm[0], LANE)
    cqn = mla_q_norm[0].reshape(1, MLA_Q_RANK)
    ckvn = mla_kv_norm[0].reshape(1, MLA_KV_RANK)
    o2q, o2k = _head_ones2(_QA_W), _head_ones2(LANE)
    tabs = _rope_tables(l)
    qa, ka, va, qm, km, vm, sg = _prep(x, ada_lat[0], nw0, w_lat, qn, kn, cqn, ckvn, wuq, wukv, o2q, o2k, tabs,
                                       latent=True, tl=tl)
    kac, vac, kmc, vmc = _prep(ctx, ada_ctx0, nw0, w_kv, None, kn, None, ckvn, None, wukv, None, o2k, None,
                               latent=False, tl=min(tl, lc))
    tq, tk = min(1024, l), min(512, l // 2)
    oa = _attention(qa, kac, vac, ka, va, q_shared=True, kv_group=2, tq=tq, tk=tk)
    om = _attention(qm, kmc, vmc, km, vm, q_shared=False, kv_group=1, tq=tq, tk=tk)
    x1 = _out0(x, oa, om, sg, attn_w_out[0].astype(BF16), ada_lat[0], tl=tl)

    nw1 = norm_w[1].reshape(1, d)
    u = _hyin(x1, ada_lat[1], nw1, hy_w_in[0].astype(BF16), hy_conv_w[0], hy_conv_b[0].reshape(1, -1),
              tl=min(512, l))
    n1 = 2 * l // DFT_N2
    n1h = n1 // 2
    tabs_d = _dft_tables(l)

    t = jnp.linspace(0.0, 1.0, l, dtype=F32)[:, None]
    wpos = (2.0 * math.pi / l) * jnp.arange(l, dtype=F32)[:, None]
    bands = jnp.linspace(1e-4, HY_BANDS - 1, HY_BANDS, dtype=F32)
    emb = jnp.concatenate([t, jnp.cos(wpos * bands), -jnp.sin(wpos * bands)], axis=-1)
    deltas = jnp.abs(jnp.linspace(math.log(HY_DECAY_TARGET) / HY_SLOW_DECAY,
                                  math.log(HY_DECAY_TARGET) / HY_FAST_DECAY, d, dtype=F32)).reshape(1, d)
    wf = hy_ffn_w3.shape[-1]
    hw, asum = _filters(_pad2(emb, l, LANE), _pad2(hy_ffn_w1[0], LANE, LANE), _pad2(hy_ffn_b1[0][None], 1, LANE),
                        _pad2(hy_ffn_w2[0], LANE, LANE), _pad2(hy_ffn_b2[0][None], 1, LANE),
                        _pad2(hy_ffn_w3[0], LANE, wf), hy_ffn_b3[0][None], _pad2(hy_freq[0][None], 1, LANE),
                        deltas, tl=tl)
    hw = hw.reshape(l, HY_ORDER, 2, d)
    fwd, bwd = hw[:, :, 0], hw[:, :, 1]
    kern = jnp.concatenate([fwd, jnp.zeros((1, HY_ORDER, d), F32), bwd[:0:-1]], axis=0)
    asum = asum.reshape(HY_ORDER, 2, d)
    l1 = asum[:, 0] + asum[:, 1] - jnp.abs(bwd[0])
    oc = HY_ORDER * d
    ct = 512
    af = _outer_fwd(kern.reshape(1, n1, DFT_N2, oc), real_input=True, tn2=32, ct=256)
    kf = _mid(af, tabs_d[0], scale=(1.0 / (l1 * (2 * l))).reshape(1, oc), ct=ct)

    v2, x1g, x2g = (_pair_view(u[i], n1h) for i in range(3))
    z = _long_conv(v2, x1g, hy_skip[0, 0:1], kf, 0, tabs_d, ct=ct)
    z = _long_conv(z, x2g, hy_skip[0, 1:2], kf, d // ct, tabs_d, ct=ct)
    z = z.reshape(b, l, d)
    return _out1(x1, z, u[3], hy_w_out[0].astype(BF16), ada_lat[1], final_norm_w.reshape(1, d), tl=tl)
```

```python
import functools
import math

import numpy as np
import jax
import jax.numpy as jnp
from jax import lax
from jax.experimental import pallas as pl
from jax.experimental.pallas import tpu as pltpu

EPS = 1e-6
GRID_W = 64
ROPE_BASE = 10000.0
GQA_HEADS, GQA_KV_HEADS, GQA_HEAD_DIM = 8, 2, 64
MLA_HEADS, MLA_Q_RANK, MLA_KV_RANK = 8, 256, 128
MLA_NOPE_DIM, MLA_ROPE_DIM, MLA_V_DIM = 64, 32, 64
HY_ORDER, HY_SHORT, HY_BANDS, HY_FFN = 2, 3, 16, 64
HY_FAST_DECAY, HY_SLOW_DECAY, HY_DECAY_TARGET = 0.3, 1.5, 1e-2
LANE = 128
DFT_N2 = 128
VMEM_LIMIT = 56 * 1024 * 1024
LOG2E = 1.4426950408889634
HI = lax.Precision.HIGHEST
F32 = jnp.float32
BF16 = jnp.bfloat16


def _cparams(sem):
    return pltpu.CompilerParams(dimension_semantics=sem, vmem_limit_bytes=VMEM_LIMIT)


def _per_chunk(fn, *arrs):
    width = arrs[0].shape[-1]
    outs = [fn(*[a[:, c:c + LANE] for a in arrs]) for c in range(0, width, LANE)]
    return outs[0] if len(outs) == 1 else jnp.concatenate(outs, axis=-1)


def _lane_iota(shape):
    return lax.broadcasted_iota(jnp.int32, shape, len(shape) - 1)


def _head_rsqrt(x, ones2):
    ss = x * x
    hi = ss.astype(BF16)
    lo = (ss - hi.astype(F32)).astype(BF16)
    tot = jnp.dot(jnp.concatenate([hi, lo], axis=-1), ones2, preferred_element_type=F32)
    return lax.rsqrt(tot * (1.0 / GQA_HEAD_DIM) + EPS)


def _rope(x, x_sw, cos, sin_signed):
    return _per_chunk(lambda c, w: c * cos + w * sin_signed, x, x_sw)


def _row_rms(x):
    return x * lax.rsqrt(jnp.mean(x * x, axis=-1, keepdims=True) + EPS)


def _silu(x):
    return x * (1.0 / (1.0 + jnp.exp(-x)))


def _ada_kernel(c_ref, w_ref, b_ref, o_ref):
    s = _silu(c_ref[...])
    o_ref[0] = jnp.dot(s, w_ref[0], precision=HI, preferred_element_type=F32) + b_ref[0]


def _ada(cs, ada_w, ada_b):
    depth, d, d3 = ada_w.shape
    rows = cs.shape[0]
    nt = d3 // d
    return pl.pallas_call(
        _ada_kernel,
        grid=(depth, nt),
        in_specs=[pl.BlockSpec((rows, d), lambda i, j: (0, 0)),
                  pl.BlockSpec((1, d, d), lambda i, j: (i, 0, j)),
                  pl.BlockSpec((1, 1, d), lambda i, j: (i, 0, j))],
        out_specs=pl.BlockSpec((1, rows, d), lambda i, j: (i, 0, j)),
        out_shape=jax.ShapeDtypeStruct((depth, rows, d3), F32),
        compiler_params=_cparams(("arbitrary", "arbitrary")),
        name="ada",
    )(cs, ada_w, ada_b.reshape(depth, 1, d3))


_QA_W, _KA_W, _VA_W, _KPE_W = 512, 512, 256, 128


def _prep_kernel(*refs, latent, d):
    if latent:
        (x_ref, ada_ref, nw_ref, w_ref, qn_ref, kn_ref, cqn_ref, ckvn_ref, wuq_ref, wukv_ref, o2q_ref, o2k_ref,
         ca_ref, sa_ref, cm_ref, sm_ref,
         qa_o, ka_o, va_o, qm_o, km_o, vm_o, sg_o) = refs
    else:
        (x_ref, ada_ref, nw_ref, w_ref, kn_ref, ckvn_ref, wukv_ref, o2k_ref,
         ka_o, va_o, km_o, vm_o) = refs
    ada = ada_ref[0]
    shift, scale = ada[:, :d], ada[:, d:2 * d]
    h = (_row_rms(x_ref[0]) * nw_ref[...]) * (1.0 + scale) + shift
    p = jnp.dot(h.astype(BF16), w_ref[...], preferred_element_type=F32)
    off = 0

    def take(width):
        nonlocal off
        off += width
        return p[:, off - width:off]

    if latent:
        qa, qa_sw = take(_QA_W), take(_QA_W)
        k, k_sw = take(LANE), take(LANE)
    else:
        k = take(LANE)
    v = take(LANE)
    if latent:
        cq = take(MLA_Q_RANK)
    ckv = take(MLA_KV_RANK)
    kpe = take(LANE)

    kn = kn_ref[...]
    rk = _head_rsqrt(k, o2k_ref[...])
    k = k * rk * kn[0:1]
    if latent:
        k = _rope(k, k_sw * rk * kn[1:2], ca_ref[...], sa_ref[...])
        kpe = _rope(kpe, take(LANE), cm_ref[...], sm_ref[...])
    low = _lane_iota(k.shape) < (LANE // 2)
    k_x = pltpu.roll(k, LANE // 2, axis=1)
    v_x = pltpu.roll(v, LANE // 2, axis=1)
    zero = jnp.zeros_like(k)
    ka = [jnp.where(low, k, zero), jnp.where(low, zero, k_x), jnp.where(low, k_x, zero), jnp.where(low, zero, k)]
    ka_o[0] = jnp.concatenate(ka, axis=-1).astype(BF16)
    one64 = (_lane_iota(k.shape) == LANE // 2).astype(F32)
    va = jnp.concatenate([jnp.where(low, v, one64), jnp.where(low, v_x, one64)], axis=-1)
    va_o[0] = va.T.astype(BF16)
    ckv_n = (_row_rms(ckv) * ckvn_ref[...]).astype(BF16)
    kv = jnp.dot(ckv_n, wukv_ref[...], preferred_element_type=F32)
    nk = MLA_HEADS * LANE
    km_o[0] = _per_chunk(lambda c: c + kpe, kv[:, :nk]).astype(BF16)
    vm_o[0] = _per_chunk(lambda c: c + one64, kv[:, nk:]).T.astype(BF16)
    if latent:
        qn = qn_ref[...]
        rq = _head_rsqrt(qa, o2q_ref[...])
        qa = _rope(qa * rq * qn[0:1], qa_sw * rq * qn[1:2], ca_ref[...], sa_ref[...])
        qa_o[0] = (qa * (GQA_HEAD_DIM ** -0.5 * LOG2E)).astype(BF16)
        cq_n = (_row_rms(cq) * cqn_ref[...]).astype(BF16)
        qm = jnp.dot(cq_n, wuq_ref[...], preferred_element_type=F32)
        qm = _rope(qm[:, :nk], qm[:, nk:], cm_ref[...], sm_ref[...])
        qm_o[0] = (qm * ((MLA_NOPE_DIM + MLA_ROPE_DIM) ** -0.5 * LOG2E)).astype(BF16)
        sg_o[0] = _silu(take(d)).astype(BF16)


def _prep(x, ada, nw, w, qn, kn, cqn, ckvn, wuq, wukv, o2q, o2k, tabs, *, latent, tl):
    b, l, d = x.shape
    grid = (l // tl, b)
    row = lambda i, j: (j, i, 0)
    const = lambda i, j: (0, 0)
    tab = lambda i, j: (i, 0)
    xspec = pl.BlockSpec((1, tl, d), row)
    adaspec = pl.BlockSpec((1, 1, ada.shape[-1]), lambda i, j: (j, 0, 0))

    def full(a):
        return pl.BlockSpec(a.shape, const)

    def out(width):
        return (pl.BlockSpec((1, tl, width), row), jax.ShapeDtypeStruct((b, l, width), BF16))

    def out_t(width):
        return (pl.BlockSpec((1, width, tl), lambda i, j: (j, 0, i)), jax.ShapeDtypeStruct((b, width, l), BF16))

    if latent:
        ca, sa, cm, sm = tabs
        ins = [x, ada, nw, w, qn, kn, cqn, ckvn, wuq, wukv, o2q, o2k, ca, sa, cm, sm]
        in_specs = [xspec, adaspec, full(nw), full(w), full(qn), full(kn), full(cqn), full(ckvn),
                    full(wuq), full(wukv), full(o2q), full(o2k)] + [pl.BlockSpec((tl, t.shape[1]), tab) for t in tabs]
        outs = [out(_QA_W), out(_KA_W), out_t(_VA_W), out(MLA_HEADS * LANE), out(MLA_HEADS * LANE),
                out_t(MLA_HEADS * LANE), out(d)]
    else:
        ins = [x, ada, nw, w, kn, ckvn, wukv, o2k]
        in_specs = [xspec, adaspec, full(nw), full(w), full(kn), full(ckvn), full(wukv), full(o2k)]
        outs = [out(_KA_W), out_t(_VA_W), out(MLA_HEADS * LANE), out_t(MLA_HEADS * LANE)]
    return pl.pallas_call(
        functools.partial(_prep_kernel, latent=latent, d=d),
        grid=grid, in_specs=in_specs,
        out_specs=[o[0] for o in outs], out_shape=[o[1] for o in outs],
        compiler_params=_cparams(("arbitrary", "arbitrary")),
        name="prep_lat" if latent else "prep_ctx",
    )(*ins)


def _attn_kernel(q_ref, kc_ref, vc_ref, kl_ref, vl_ref, o_ref, s_a, s_b, *, tk, q_shared):
    tq = q_ref.shape[1]
    lk = kl_ref.shape[1]
    nblk = lk // tk
    dn = (((1,), (1,)), ((), ()))
    qs = [q_ref[0, :, 0:LANE] if q_shared else q_ref[0, :, e * LANE:(e + 1) * LANE] for e in range(2)]

    def scores(e, k):
        return lax.dot_general(k, qs[e], dn, preferred_element_type=F32)

    def scores_at(e, r):
        return scores(e, kl_ref[0, pl.ds(r, tk), e * LANE:(e + 1) * LANE])

    def vrows(v_ref, e, cols):
        return v_ref[0, :, cols] if q_shared else v_ref[0, e * LANE:(e + 1) * LANE, cols]

    def update(s, vt, carry):
        m, acc = carry
        m_new = jnp.maximum(m, jnp.max(s, axis=0, keepdims=True))
        p = jnp.exp2(s - m_new)
        acc = jnp.exp2(m - m_new) * acc + jnp.dot(vt, p.astype(BF16), preferred_element_type=F32)
        return m_new, acc

    init = (jnp.full((1, tq), -1e30, F32), jnp.zeros((LANE, tq), F32))
    s_ctx = [scores(e, kc_ref[0, :, e * LANE:(e + 1) * LANE]) for e in range(2)]
    for e in range(2):
        s_a[e] = scores_at(e, 0)
    carry = tuple(update(s_ctx[e], vrows(vc_ref, e, slice(None)), init) for e in range(2))

    def body(j, carry):
        r0 = pl.multiple_of(2 * j * tk, tk)
        r1 = pl.multiple_of(r0 + tk, tk)
        r2 = pl.multiple_of(jnp.minimum(r1 + tk, lk - tk), tk)
        for e in range(2):
            s_b[e] = scores_at(e, r1)
        carry = tuple(update(s_a[e], vrows(vl_ref, e, pl.ds(r0, tk)), carry[e]) for e in range(2))
        for e in range(2):
            s_a[e] = scores_at(e, r2)
        return tuple(update(s_b[e], vrows(vl_ref, e, pl.ds(r1, tk)), carry[e]) for e in range(2))

    carry = lax.fori_loop(0, nblk // 2, body, carry)
    dv = LANE // 2
    outs = [acc[:dv] / acc[dv:dv + 1] for (_, acc) in carry]
    o_ref[0] = jnp.concatenate(outs, axis=0).T.astype(o_ref.dtype)


def _attention(q, kc, vc, kl, vl, *, q_shared, kv_group, tq, tk):
    b, l, _ = q.shape
    wq = LANE if q_shared else 2 * LANE
    pairs = q.shape[-1] // wq
    lc = kc.shape[1]
    assert (l // tk) % 2 == 0, "key blocks are consumed two per loop trip"
    vw = LANE if q_shared else 2 * LANE
    kv = lambda bi, j, i: (bi, 0, j // kv_group)
    vt = lambda bi, j, i: (bi, j // kv_group, 0)
    return pl.pallas_call(
        functools.partial(_attn_kernel, tk=tk, q_shared=q_shared),
        grid=(b, pairs, l // tq),
        in_specs=[pl.BlockSpec((1, tq, wq), lambda bi, j, i: (bi, i, j)),
                  pl.BlockSpec((1, lc, 2 * LANE), kv), pl.BlockSpec((1, vw, lc), vt),
                  pl.BlockSpec((1, l, 2 * LANE), kv), pl.BlockSpec((1, vw, l), vt)],
        out_specs=pl.BlockSpec((1, tq, LANE), lambda bi, j, i: (bi, i, j)),
        out_shape=jax.ShapeDtypeStruct((b, l, pairs * LANE), BF16),
        scratch_shapes=[pltpu.VMEM((2, tk, tq), F32), pltpu.VMEM((2, tk, tq), F32)],
        compiler_params=_cparams(("arbitrary", "arbitrary", "arbitrary")),
        name="attn_gqa" if q_shared else "attn_mla",
    )(q, kc, vc, kl, vl)


def _out0_kernel(x_ref, oa_ref, om_ref, sg_ref, w_ref, ada_ref, o_ref, *, d):
    o = jnp.concatenate([oa_ref[0], om_ref[0]], axis=-1).astype(F32) * sg_ref[0].astype(F32)
    y = jnp.dot(o.astype(BF16), w_ref[...], preferred_element_type=F32)
    o_ref[0] = x_ref[0] + ada_ref[0][:, 2 * d:3 * d] * y


def _out0(x, oa, om, sg, w, ada, *, tl):
    b, l, d = x.shape
    row = lambda bi, i: (bi, i, 0)
    return pl.pallas_call(
        functools.partial(_out0_kernel, d=d),
        grid=(b, l // tl),
        in_specs=[pl.BlockSpec((1, tl, d), row), pl.BlockSpec((1, tl, oa.shape[-1]), row),
                  pl.BlockSpec((1, tl, om.shape[-1]), row), pl.BlockSpec((1, tl, d), row),
                  pl.BlockSpec(w.shape, lambda bi, i: (0, 0)),
                  pl.BlockSpec((1, 1, ada.shape[-1]), lambda bi, i: (bi, 0, 0))],
        out_specs=pl.BlockSpec((1, tl, d), row),
        out_shape=jax.ShapeDtypeStruct((b, l, d), F32),
        compiler_params=_cparams(("arbitrary", "arbitrary")),
        name="out0",
    )(x, oa, om, sg, w, ada)


_HALO = 8


def _hyin_kernel(x_ref, xp_ref, xn_ref, ada_ref, nw_ref, w_ref, cw_ref, cb_ref, *o_refs, d):
    i = pl.program_id(1)
    tl = x_ref.shape[1]
    n_conv = len(o_refs) - 1
    ada = ada_ref[0]
    shift, scale = ada[:, :d], ada[:, d:2 * d]

    def mod(x):
        return (_row_rms(x) * nw_ref[...]) * (1.0 + scale) + shift

    hp = mod(xp_ref[0]) * (i > 0).astype(F32)
    hn = mod(xn_ref[0]) * (i < pl.num_programs(1) - 1).astype(F32)
    h = jnp.concatenate([mod(x_ref[0]), hp, hn], axis=0).astype(BF16)
    rows = lax.broadcasted_iota(jnp.int32, (tl, d), 0)
    cw = cw_ref[...]
    for n in range(n_conv + 1):
        p = jnp.dot(h, w_ref[:, n * d:(n + 1) * d], preferred_element_type=F32)
        pm = p[0:tl]
        if n < n_conv:
            prev = jnp.where(rows == 0, p[tl + _HALO - 1:tl + _HALO], pltpu.roll(pm, 1, axis=0))
            nxt = jnp.where(rows == tl - 1, p[tl + _HALO:tl + _HALO + 1], pltpu.roll(pm, tl - 1, axis=0))
            c0, c1, c2 = (cw[j:j + 1, n * d:(n + 1) * d] for j in range(HY_SHORT))
            o_refs[n][0] = prev * c0 + pm * c1 + nxt * c2 + cb_ref[:, n * d:(n + 1) * d]
        else:
            o_refs[n][0] = _silu(pm)


def _hyin(x, ada, nw, w, cw, cb, *, tl):
    b, l, d = x.shape
    ng = w.shape[1] // d
    tb = tl // _HALO
    nb = l // _HALO
    const = lambda bi, i: (0, 0)
    blk = pl.BlockSpec((1, tl, d), lambda bi, i: (bi, i, 0))
    return pl.pallas_call(
        functools.partial(_hyin_kernel, d=d),
        grid=(b, l // tl),
        in_specs=[blk,
                  pl.BlockSpec((1, _HALO, d), lambda bi, i: (bi, jnp.maximum(i * tb - 1, 0), 0)),
                  pl.BlockSpec((1, _HALO, d), lambda bi, i: (bi, jnp.minimum((i + 1) * tb, nb - 1), 0)),
                  pl.BlockSpec((1, 1, ada.shape[-1]), lambda bi, i: (bi, 0, 0)),
                  pl.BlockSpec(nw.shape, const), pl.BlockSpec(w.shape, const),
                  pl.BlockSpec(cw.shape, const), pl.BlockSpec(cb.shape, const)],
        out_specs=[blk] * ng,
        out_shape=[jax.ShapeDtypeStruct((b, l, d), F32)] * ng,
        compiler_params=_cparams(("arbitrary", "arbitrary")),
        name="hy_in",
    )(x, x, x, ada, nw, w, cw, cb)


def _filt_kernel(emb_ref, w1_ref, b1_ref, w2_ref, b2_ref, w3_ref, b3_ref, fr_ref, dl_ref, h_o, s_o, *, reps):
    emb = emb_ref[...]
    fr = fr_ref[...]
    hid = jnp.sin(fr * (jnp.dot(emb, w1_ref[...], precision=HI, preferred_element_type=F32) + b1_ref[...]))
    hid = jnp.sin(fr * (jnp.dot(hid, w2_ref[...], precision=HI, preferred_element_type=F32) + b2_ref[...]))
    h = jnp.dot(hid, w3_ref[...], precision=HI, preferred_element_type=F32) + b3_ref[...]
    win = jnp.exp(-emb[:, 0:1] * dl_ref[...])
    hw = h * jnp.concatenate([win] * reps, axis=-1)
    rows = lax.broadcasted_iota(jnp.int32, hw.shape, 0) + pl.program_id(0) * hw.shape[0]
    hw = jnp.where((rows == 0) & (_lane_iota(hw.shape) >= hw.shape[1] // 2), 0.0, hw)
    h_o[...] = hw

    @pl.when(pl.program_id(0) == 0)
    def _():
        s_o[...] = jnp.zeros_like(s_o)

    s_o[...] += jnp.sum(jnp.abs(hw), axis=0, keepdims=True)


def _filters(emb, w1, b1, w2, b2, w3, b3, fr, dl, *, tl):
    l = emb.shape[0]
    wo = w3.shape[1]
    const = lambda i: (0, 0)
    full = lambda a: pl.BlockSpec(a.shape, const)
    return pl.pallas_call(
        functools.partial(_filt_kernel, reps=wo // dl.shape[1]),
        grid=(l // tl,),
        in_specs=[pl.BlockSpec((tl, emb.shape[1]), lambda i: (i, 0)), full(w1), full(b1), full(w2), full(b2),
                  full(w3), full(b3), full(fr), full(dl)],
        out_specs=[pl.BlockSpec((tl, wo), lambda i: (i, 0)), pl.BlockSpec((1, wo), const)],
        out_shape=[jax.ShapeDtypeStruct((l, wo), F32), jax.ShapeDtypeStruct((1, wo), F32)],
        compiler_params=_cparams(("arbitrary",)),
        name="hy_filter",
    )(emb, w1, b1, w2, b2, w3, b3, fr, dl)


def _cmul_const(a, ang):
    ar, ai = a
    q = ang / (0.5 * math.pi)
    if abs(q - round(q)) < 1e-12:
        return [(ar, ai), (-ai, ar), (-ar, -ai), (ai, -ar)][int(round(q)) % 4]
    c, s = math.cos(ang), math.sin(ang)
    return (ar * c - ai * s, ar * s + ai * c)


def _fft_dif(x):
    x = list(x)
    n = len(x)
    half = n // 2
    while half >= 1:
        for base in range(0, n, 2 * half):
            for j in range(half):
                a, b = x[base + j], x[base + j + half]
                ang = -math.pi * j / half
                if b is None:
                    x[base + j + half] = None if a is None else _cmul_const(a, ang)
                else:
                    x[base + j] = (a[0] + b[0], a[1] + b[1])
                    x[base + j + half] = _cmul_const((a[0] - b[0], a[1] - b[1]), ang)
        half //= 2
    return x


def _ifft_dit(x, keep):
    x = list(x)
    n = len(x)
    half = 1
    while half <= n // 2:
        last = half == n // 2
        for base in range(0, n, 2 * half):
            for j in range(half):
                a = x[base + j]
                b = _cmul_const(x[base + j + half], math.pi * j / half)
                x[base + j] = (a[0] + b[0], a[1] + b[1])
                if not last or base + j + half < keep:
                    x[base + j + half] = (a[0] - b[0], a[1] - b[1])
        half *= 2
    return x[:keep]


def _outer_fwd_kernel(u_ref, o_ref, *, real_input):
    n1 = o_ref.shape[2]
    if real_input:
        x = [(u_ref[i], jnp.zeros_like(u_ref[i])) for i in range(n1 // 2)]
    else:
        x = [(u_ref[0, 0, i], u_ref[0, 1, i]) for i in range(n1 // 2)]
    for s, (re, im) in enumerate(_fft_dif(x + [None] * (n1 // 2))):
        o_ref[0, 0, s] = re
        o_ref[0, 1, s] = im


def _outer_fwd(u, *, real_input, tn2, ct, nseq=1):
    n2 = u.shape[-2]
    if real_input:
        p, c, n1 = nseq, u.shape[-1] // nseq, 2 * u.shape[0]
        in_spec = pl.BlockSpec((n1 // 2, tn2, ct), lambda pi, r, j: (0, r, pi * (c // ct) + j))
    else:
        p, c, n1 = u.shape[0], u.shape[-1], 2 * u.shape[2]
        in_spec = pl.BlockSpec((1, 2, n1 // 2, tn2, ct), lambda pi, r, j: (pi, 0, 0, r, j))
    return pl.pallas_call(
        functools.partial(_outer_fwd_kernel, real_input=real_input),
        grid=(p, n2 // tn2, c // ct), in_specs=[in_spec],
        out_specs=pl.BlockSpec((1, 2, n1, tn2, ct), lambda pi, r, j: (pi, 0, 0, r, j)),
        out_shape=jax.ShapeDtypeStruct((p, 2, n1, n2, c), F32),
        compiler_params=_cparams(("arbitrary", "arbitrary", "arbitrary")),
        name="hy_outer_filt" if real_input else "hy_outer_fwd",
    )(u)


def _outer_inv_kernel(a_ref, u_ref, g_ref, sk_ref, o_ref, *next_ref):
    n1 = a_ref.shape[2]
    y = _ifft_dit([(a_ref[0, 0, s], a_ref[0, 1, s]) for s in range(n1)], n1 // 2)
    sk = sk_ref[...]
    z = [(g_ref[0, 0, i] * (re + u_ref[0, 0, i] * sk), g_ref[0, 1, i] * (im + u_ref[0, 1, i] * sk))
         for i, (re, im) in enumerate(y)]
    for i, (re, im) in enumerate(z):
        o_ref[0, 0, i] = re
        o_ref[0, 1, i] = im
    if next_ref:
        for s, (re, im) in enumerate(_fft_dif(z + [None] * (n1 // 2))):
            next_ref[0][0, 0, s] = re
            next_ref[0][0, 1, s] = im


def _outer_inv(a, u, gate, skip, *, tn2, ct, with_next):
    p, _, n1, n2, c = a.shape
    blk = lambda rows: pl.BlockSpec((1, 2, rows, tn2, ct), lambda pi, r, j: (pi, 0, 0, r, j))
    out_specs, out_shape = [blk(n1 // 2)], [jax.ShapeDtypeStruct(u.shape, F32)]
    if with_next:
        out_specs.append(blk(n1))
        out_shape.append(jax.ShapeDtypeStruct(a.shape, F32))
    return pl.pallas_call(
        _outer_inv_kernel,
        grid=(p, n2 // tn2, c // ct),
        in_specs=[blk(n1), blk(n1 // 2), blk(n1 // 2), pl.BlockSpec((1, ct), lambda pi, r, j: (0, j))],
        out_specs=out_specs, out_shape=out_shape,
        compiler_params=_cparams(("arbitrary", "arbitrary", "arbitrary")),
        name="hy_outer_inv_fwd" if with_next else "hy_outer_inv",
    )(a, u, gate, skip)


def _dot3(m3, x):
    hi = x.astype(BF16)
    lo = (x - hi.astype(F32)).astype(BF16)
    return jnp.dot(m3, jnp.concatenate([hi, hi, lo], axis=0), preferred_element_type=F32)


def _mid_kernel(a_ref, g_ref, *rest, spectrum):
    np_, n2, ct = a_ref.shape[0], a_ref.shape[3], a_ref.shape[4]
    xs = [_dot3(g_ref[0], a_ref[p, :, 0].reshape(2 * n2, ct)) for p in range(np_)]
    if spectrum:
        sc_ref, o_ref = rest
        f, bw = xs
        sc = sc_ref[...]
        o_ref[0, 0, 0] = (f[:n2] + bw[:n2]) * sc
        o_ref[0, 1, 0] = (f[n2:] - bw[n2:]) * sc
    else:
        gt_ref, kf_ref, o_ref = rest
        kr, ki = kf_ref[0, 0, 0], kf_ref[0, 1, 0]
        ys = [jnp.concatenate([x[:n2] * kr - x[n2:] * ki, x[:n2] * ki + x[n2:] * kr], axis=0) for x in xs]
        for p in range(np_):
            o_ref[p, :, 0] = _dot3(gt_ref[0], ys[p]).reshape(2, n2, ct)


def _mid(a, g, gt=None, kf=None, scale=None, *, ct, kf_col0=0):
    p, _, n1, n2, c = a.shape
    ablk = pl.BlockSpec((p, 2, 1, n2, ct), lambda k, j: (0, 0, k, 0, j))
    gblk = pl.BlockSpec((1,) + g.shape[1:], lambda k, j: (k, 0, 0))
    oblk, oshape = ablk, a.shape
    if kf is None:
        ins = [a, g, scale]
        in_specs = [ablk, gblk, pl.BlockSpec((1, ct), lambda k, j: (0, j))]
        oblk, oshape = pl.BlockSpec((1, 2, 1, n2, ct), lambda k, j: (0, 0, k, 0, j)), (1,) + a.shape[1:]
    else:
        ins = [a, g, gt, kf]
        in_specs = [ablk, gblk, gblk,
                    pl.BlockSpec((1, 2, 1, n2, ct), lambda k, j: (0, 0, k, 0, kf_col0 + j))]
    return pl.pallas_call(
        functools.partial(_mid_kernel, spectrum=kf is None),
        grid=(n1, c // ct), in_specs=in_specs, out_specs=oblk,
        out_shape=jax.ShapeDtypeStruct(oshape, F32),
        compiler_params=_cparams(("arbitrary", "arbitrary")),
        name="hy_spectrum" if kf is None else "hy_mid",
    )(*ins)


def _dft_tables(l):
    n = 2 * l
    n2 = DFT_N2
    n1 = n // n2
    bits = n1.bit_length() - 1
    k1 = np.array([int(format(s, "0%db" % bits)[::-1], 2) for s in range(n1)], dtype=np.float64)
    kk = k1[:, None, None] + n1 * np.arange(n2, dtype=np.float64)[None, :, None]
    th = 2.0 * np.pi * kk * np.arange(n2, dtype=np.float64)[None, None, :] / n
    c, s = np.cos(th), np.sin(th)
    g = np.concatenate([np.concatenate([c, s], axis=2), np.concatenate([-s, c], axis=2)], axis=1)

    def split3(m):
        m = jnp.asarray(m, F32)
        hi = m.astype(BF16)
        lo = (m - hi.astype(F32)).astype(BF16)
        return jnp.concatenate([hi, lo, hi], axis=2)

    return split3(g), split3(np.transpose(g, (0, 2, 1)))


def _pair_view(u, n1h):
    b, l, c = u.shape
    return u.reshape(b // 2, 2, n1h, l // n1h, c)


def _long_convs(u, gates, skips, kf, tabs, *, ct):
    g, gt = tabs
    c = u.shape[-1]
    a = _outer_fwd(u, real_input=False, tn2=32, ct=256)
    z = u
    for o, (gate, skip) in enumerate(zip(gates, skips)):
        a = _mid(a, g, gt, kf, ct=ct, kf_col0=o * (c // ct))
        last = o == len(gates) - 1
        res = _outer_inv(a, z, gate, skip, tn2=32, ct=256, with_next=not last)
        z, a = (res[0], None) if last else res
    return z


def _out1_kernel(x_ref, z_ref, sg_ref, w_ref, ada_ref, fw_ref, o_ref, *, d):
    y = jnp.dot((z_ref[0] * sg_ref[0]).astype(BF16), w_ref[...], preferred_element_type=F32)
    x = x_ref[0] + ada_ref[0][:, 2 * d:3 * d] * y
    o_ref[0] = _row_rms(x) * fw_ref[...]


def _out1(x, z, sg, w, ada, fw, *, tl):
    b, l, d = x.shape
    row = lambda bi, i: (bi, i, 0)
    blk = pl.BlockSpec((1, tl, d), row)
    return pl.pallas_call(
        functools.partial(_out1_kernel, d=d),
        grid=(b, l // tl),
        in_specs=[blk, blk, blk, pl.BlockSpec(w.shape, lambda bi, i: (0, 0)),
                  pl.BlockSpec((1, 1, ada.shape[-1]), lambda bi, i: (bi, 0, 0)),
                  pl.BlockSpec(fw.shape, lambda bi, i: (0, 0))],
        out_specs=blk,
        out_shape=jax.ShapeDtypeStruct((b, l, d), F32),
        compiler_params=_cparams(("arbitrary", "arbitrary")),
        name="out1",
    )(x, z, sg, w, ada, fw)


def _swap_cols(w, q):
    return w[..., np.arange(w.shape[-1]) ^ q]


def _pack_attn_w_in(w):
    d = w.shape[0]
    o = 0
    wq = w[:, o:o + 512]; o += 512
    wk = w[:, o:o + 128]; o += 128
    wv = w[:, o:o + 128]; o += 128
    wcq = w[:, o:o + MLA_Q_RANK]; o += MLA_Q_RANK
    wckv = w[:, o:o + MLA_KV_RANK]; o += MLA_KV_RANK
    wkpe = w[:, o:o + MLA_ROPE_DIM]; o += MLA_ROPE_DIM
    wg = w[:, o:]
    qa, qm = GQA_HEAD_DIM // 4, MLA_ROPE_DIM // 4

    def pe_chunk(wp):
        return jnp.concatenate([jnp.zeros((d, 64), w.dtype), wp, jnp.zeros((d, 32), w.dtype)], axis=1)

    kpe, kpe_sw = pe_chunk(wkpe), pe_chunk(_swap_cols(wkpe, qm))
    kv_part = jnp.concatenate([wk, wv, wckv, kpe], axis=1)
    lat = jnp.concatenate([wq, _swap_cols(wq, qa), wk, _swap_cols(wk, qa), wv, wcq, wckv, kpe, kpe_sw, wg], axis=1)
    return lat.astype(BF16), kv_part.astype(BF16)


def _pack_mla_up(w_uq, w_ukv):
    dq = MLA_NOPE_DIM + MLA_ROPE_DIM
    r = w_uq.shape[0]
    z = lambda n: jnp.zeros((r, n), w_uq.dtype)
    uq, uq_sw = [], []
    for h in range(MLA_HEADS):
        nope, pe = w_uq[:, dq * h:dq * h + MLA_NOPE_DIM], w_uq[:, dq * h + MLA_NOPE_DIM:dq * (h + 1)]
        uq += [nope, pe, z(LANE - dq)]
        uq_sw += [z(MLA_NOPE_DIM), _swap_cols(pe, MLA_ROPE_DIM // 4), z(LANE - dq)]
    dkv = MLA_NOPE_DIM + MLA_V_DIM
    kn = jnp.concatenate(
        [jnp.concatenate([w_ukv[:, dkv * h:dkv * h + MLA_NOPE_DIM],
                          jnp.zeros((w_ukv.shape[0], LANE - MLA_NOPE_DIM), w_ukv.dtype)], axis=1)
         for h in range(MLA_HEADS)], axis=1)
    vm = jnp.concatenate(
        [jnp.concatenate([w_ukv[:, dkv * h + MLA_NOPE_DIM:dkv * (h + 1)],
                          jnp.zeros((w_ukv.shape[0], LANE - MLA_V_DIM), w_ukv.dtype)], axis=1)
         for h in range(MLA_HEADS)], axis=1)
    return jnp.concatenate(uq + uq_sw, axis=1).astype(BF16), jnp.concatenate([kn, vm], axis=1).astype(BF16)


def _head_ones2(width):
    i = np.arange(width) // GQA_HEAD_DIM
    blk = (i[:, None] == i[None, :]).astype(np.float32)
    return jnp.asarray(np.concatenate([blk, blk], axis=0), BF16)


def _rope_tables(l):
    rows = (jnp.arange(l, dtype=jnp.int32) // GRID_W).astype(F32)[:, None]
    cols = (jnp.arange(l, dtype=jnp.int32) % GRID_W).astype(F32)[:, None]

    def tab(rot_dim):
        q = rot_dim // 4
        inv = ROPE_BASE ** (-jnp.arange(q, dtype=F32) / q)
        ar, ac = rows * inv, cols * inv
        cos = jnp.concatenate([jnp.cos(ar)] * 2 + [jnp.cos(ac)] * 2, axis=1)
        sin = jnp.concatenate([-jnp.sin(ar), jnp.sin(ar), -jnp.sin(ac), jnp.sin(ac)], axis=1)
        return cos, sin

    ca, sa = tab(GQA_HEAD_DIM)
    ca = jnp.concatenate([ca] * (LANE // GQA_HEAD_DIM), axis=1)
    sa = jnp.concatenate([sa] * (LANE // GQA_HEAD_DIM), axis=1)
    cm, sm = tab(MLA_ROPE_DIM)
    one, zero = jnp.ones((l, 1), F32), jnp.zeros((l, 1), F32)
    cm = jnp.concatenate([jnp.tile(one, (1, 64)), cm, jnp.tile(one, (1, 32))], axis=1)
    sm = jnp.concatenate([jnp.tile(zero, (1, 64)), sm, jnp.tile(zero, (1, 32))], axis=1)
    return ca, sa, cm, sm


def _pad2(a, r, c):
    return jnp.pad(a, ((0, r - a.shape[0]), (0, c - a.shape[1])))


def kernel(x, c, ctx, c_ctx, ada_w, ada_b, norm_w, attn_w_in, attn_q_norm, attn_k_norm, mla_q_norm, mla_kv_norm, mla_w_uq, mla_w_ukv, attn_w_out, hy_w_in, hy_conv_w, hy_conv_b, hy_ffn_w1, hy_ffn_b1, hy_ffn_w2, hy_ffn_b2, hy_ffn_w3, hy_ffn_b3, hy_freq, hy_skip, hy_w_out, final_norm_w):
    b, l, d = x.shape
    lc = ctx.shape[1]
    tl = min(256, l)

    rows = -(-(b + 1) // 8) * 8
    cs = jnp.concatenate([c, c_ctx[None, :], jnp.zeros((rows - b - 1, d), F32)], axis=0)
    ada = _ada(cs, ada_w, ada_b)
    ada_lat = [ada[i, :b].reshape(b, 1, 3 * d) for i in range(ada.shape[0])]
    ada_ctx0 = jnp.broadcast_to(ada[0, b].reshape(1, 1, 3 * d), (b, 1, 3 * d))

    w_lat, w_kv = _pack_attn_w_in(attn_w_in[0])
    wuq, wukv = _pack_mla_up(mla_w_uq[0], mla_w_ukv[0])
    nw0 = norm_w[0].reshape(1, d)
    def norm_rows(wn, width):
        sw = _swap_cols(wn, GQA_HEAD_DIM // 4)
        return jnp.stack([jnp.tile(wn, width // GQA_HEAD_DIM), jnp.tile(sw, width // GQA_HEAD_DIM)])

    qn, kn = norm_rows(attn_q_norm[0], _QA_W), norm_rows(attn_k_norm[0], LANE)
    cqn = mla_q_norm[0].reshape(1, MLA_Q_RANK)
    ckvn = mla_kv_norm[0].reshape(1, MLA_KV_RANK)
    o2q, o2k = _head_ones2(_QA_W), _head_ones2(LANE)
    tabs = _rope_tables(l)
    qa, ka, va, qm, km, vm, sg = _prep(x, ada_lat[0], nw0, w_lat, qn, kn, cqn, ckvn, wuq, wukv, o2q, o2k, tabs,
                                       latent=True, tl=tl)
    kac, vac, kmc, vmc = _prep(ctx, ada_ctx0, nw0, w_kv, None, kn, None, ckvn, None, wukv, None, o2k, None,
                               latent=False, tl=min(tl, lc))
    tq, tk = min(1024, l), min(512, l // 2)
    oa = _attention(qa, kac, vac, ka, va, q_shared=True, kv_group=2, tq=tq, tk=tk)
    om = _attention(qm, kmc, vmc, km, vm, q_shared=False, kv_group=1, tq=tq, tk=tk)
    x1 = _out0(x, oa, om, sg, attn_w_out[0].astype(BF16), ada_lat[0], tl=tl)

    nw1 = norm_w[1].reshape(1, d)
    u = _hyin(x1, ada_lat[1], nw1, hy_w_in[0].astype(BF16), hy_conv_w[0], hy_conv_b[0].reshape(1, -1),
              tl=min(512, l))
    n1 = 2 * l // DFT_N2
    n1h = n1 // 2
    tabs_d = _dft_tables(l)

    t = jnp.linspace(0.0, 1.0, l, dtype=F32)[:, None]
    wpos = (2.0 * math.pi / l) * jnp.arange(l, dtype=F32)[:, None]
    bands = jnp.linspace(1e-4, HY_BANDS - 1, HY_BANDS, dtype=F32)
    emb = jnp.concatenate([t, jnp.cos(wpos * bands), -jnp.sin(wpos * bands)], axis=-1)
    deltas = jnp.abs(jnp.linspace(math.log(HY_DECAY_TARGET) / HY_SLOW_DECAY,
                                  math.log(HY_DECAY_TARGET) / HY_FAST_DECAY, d, dtype=F32)).reshape(1, d)
    wf = hy_ffn_w3.shape[-1]
    oc = HY_ORDER * d

    def by_direction(a):
        return a.reshape(-1, HY_ORDER, 2, d).transpose(0, 2, 1, 3).reshape(-1, wf)

    hw, asum = _filters(_pad2(emb, l, LANE), _pad2(hy_ffn_w1[0], LANE, LANE), _pad2(hy_ffn_b1[0][None], 1, LANE),
                        _pad2(hy_ffn_w2[0], LANE, LANE), _pad2(hy_ffn_b2[0][None], 1, LANE),
                        _pad2(by_direction(hy_ffn_w3[0]), LANE, wf), by_direction(hy_ffn_b3[0][None]),
                        _pad2(hy_freq[0][None], 1, LANE), deltas, tl=tl)
    l1 = asum[:, :oc] + asum[:, oc:]
    ct = 512
    af = _outer_fwd(hw.reshape(n1h, DFT_N2, wf), real_input=True, nseq=2, tn2=32, ct=256)
    kf = _mid(af, tabs_d[0], scale=1.0 / (l1 * (2 * l)), ct=oc)

    v2, x1g, x2g = (_pair_view(u[i], n1h) for i in range(3))
    z = _long_convs(v2, [x1g, x2g], [hy_skip[0, o:o + 1] for o in range(HY_ORDER)], kf, tabs_d, ct=ct)
    z = z.reshape(b, l, d)
    return _out1(x1, z, u[3], hy_w_out[0].astype(BF16), ada_lat[1], final_norm_w.reshape(1, d), tl=tl)
```

```python
import functools
import math

import numpy as np
import jax
import jax.numpy as jnp
from jax import lax
from jax.experimental import pallas as pl
from jax.experimental.pallas import tpu as pltpu

EPS = 1e-6
GRID_W = 64
ROPE_BASE = 10000.0
GQA_HEADS, GQA_KV_HEADS, GQA_HEAD_DIM = 8, 2, 64
MLA_HEADS, MLA_Q_RANK, MLA_KV_RANK = 8, 256, 128
MLA_NOPE_DIM, MLA_ROPE_DIM, MLA_V_DIM = 64, 32, 64
HY_ORDER, HY_SHORT, HY_BANDS, HY_FFN = 2, 3, 16, 64
HY_FAST_DECAY, HY_SLOW_DECAY, HY_DECAY_TARGET = 0.3, 1.5, 1e-2
LANE = 128
DFT_N2 = 128
VMEM_LIMIT = 56 * 1024 * 1024
LOG2E = 1.4426950408889634
HI = lax.Precision.HIGHEST
F32 = jnp.float32
BF16 = jnp.bfloat16


def _cparams(sem):
    return pltpu.CompilerParams(dimension_semantics=sem, vmem_limit_bytes=VMEM_LIMIT)


def _per_chunk(fn, *arrs):
    width = arrs[0].shape[-1]
    outs = [fn(*[a[:, c:c + LANE] for a in arrs]) for c in range(0, width, LANE)]
    return outs[0] if len(outs) == 1 else jnp.concatenate(outs, axis=-1)


def _lane_iota(shape):
    return lax.broadcasted_iota(jnp.int32, shape, len(shape) - 1)


def _head_rsqrt(x, ones2):
    ss = x * x
    hi = ss.astype(BF16)
    lo = (ss - hi.astype(F32)).astype(BF16)
    tot = jnp.dot(jnp.concatenate([hi, lo], axis=-1), ones2, preferred_element_type=F32)
    return lax.rsqrt(tot * (1.0 / GQA_HEAD_DIM) + EPS)


def _rope(x, x_sw, cos, sin_signed):
    return _per_chunk(lambda c, w: c * cos + w * sin_signed, x, x_sw)


def _row_rms(x):
    return x * lax.rsqrt(jnp.mean(x * x, axis=-1, keepdims=True) + EPS)


def _silu(x):
    return x * (1.0 / (1.0 + jnp.exp(-x)))


def _ada_kernel(c_ref, w_ref, b_ref, o_ref):
    s = _silu(c_ref[...])
    o_ref[0] = jnp.dot(s, w_ref[0], precision=HI, preferred_element_type=F32) + b_ref[0]


def _ada(cs, ada_w, ada_b):
    depth, d, d3 = ada_w.shape
    rows = cs.shape[0]
    nt = d3 // d
    return pl.pallas_call(
        _ada_kernel,
        grid=(depth, nt),
        in_specs=[pl.BlockSpec((rows, d), lambda i, j: (0, 0)),
                  pl.BlockSpec((1, d, d), lambda i, j: (i, 0, j)),
                  pl.BlockSpec((1, 1, d), lambda i, j: (i, 0, j))],
        out_specs=pl.BlockSpec((1, rows, d), lambda i, j: (i, 0, j)),
        out_shape=jax.ShapeDtypeStruct((depth, rows, d3), F32),
        compiler_params=_cparams(("arbitrary", "arbitrary")),
        name="ada",
    )(cs, ada_w, ada_b.reshape(depth, 1, d3))


_QA_W, _KA_W, _VA_W, _KPE_W = 512, 512, 256, 128


def _prep_kernel(*refs, latent, d):
    if latent:
        (x_ref, ada_ref, nw_ref, w_ref, qn_ref, kn_ref, cqn_ref, ckvn_ref, wuq_ref, wukv_ref, o2q_ref, o2k_ref,
         ca_ref, sa_ref, cm_ref, sm_ref,
         qa_o, ka_o, va_o, qm_o, km_o, vm_o, sg_o) = refs
    else:
        (x_ref, ada_ref, nw_ref, w_ref, kn_ref, ckvn_ref, wukv_ref, o2k_ref,
         ka_o, va_o, km_o, vm_o) = refs
    ada = ada_ref[0]
    shift, scale = ada[:, :d], ada[:, d:2 * d]
    h = (_row_rms(x_ref[0]) * nw_ref[...]) * (1.0 + scale) + shift
    p = jnp.dot(h.astype(BF16), w_ref[...], preferred_element_type=F32)
    off = 0

    def take(width):
        nonlocal off
        off += width
        return p[:, off - width:off]

    if latent:
        qa, qa_sw = take(_QA_W), take(_QA_W)
        k, k_sw = take(LANE), take(LANE)
    else:
        k = take(LANE)
    v = take(LANE)
    if latent:
        cq = take(MLA_Q_RANK)
    ckv = take(MLA_KV_RANK)
    kpe = take(LANE)

    kn = kn_ref[...]
    rk = _head_rsqrt(k, o2k_ref[...])
    k = k * rk * kn[0:1]
    if latent:
        k = _rope(k, k_sw * rk * kn[1:2], ca_ref[...], sa_ref[...])
        kpe = _rope(kpe, take(LANE), cm_ref[...], sm_ref[...])
    low = _lane_iota(k.shape) < (LANE // 2)
    k_x = pltpu.roll(k, LANE // 2, axis=1)
    v_x = pltpu.roll(v, LANE // 2, axis=1)
    zero = jnp.zeros_like(k)
    ka = [jnp.where(low, k, zero), jnp.where(low, zero, k_x), jnp.where(low, k_x, zero), jnp.where(low, zero, k)]
    ka_o[0] = jnp.concatenate(ka, axis=-1).astype(BF16)
    one64 = (_lane_iota(k.shape) == LANE // 2).astype(F32)
    va = jnp.concatenate([jnp.where(low, v, one64), jnp.where(low, v_x, one64)], axis=-1)
    va_o[0] = va.T.astype(BF16)
    ckv_n = (_row_rms(ckv) * ckvn_ref[...]).astype(BF16)
    kv = jnp.dot(ckv_n, wukv_ref[...], preferred_element_type=F32)
    nk = MLA_HEADS * LANE
    km_o[0] = _per_chunk(lambda c: c + kpe, kv[:, :nk]).astype(BF16)
    vm_o[0] = _per_chunk(lambda c: c + one64, kv[:, nk:]).T.astype(BF16)
    if latent:
        qn = qn_ref[...]
        rq = _head_rsqrt(qa, o2q_ref[...])
        qa = _rope(qa * rq * qn[0:1], qa_sw * rq * qn[1:2], ca_ref[...], sa_ref[...])
        qa_o[0] = (qa * (GQA_HEAD_DIM ** -0.5 * LOG2E)).astype(BF16)
        cq_n = (_row_rms(cq) * cqn_ref[...]).astype(BF16)
        qm = jnp.dot(cq_n, wuq_ref[...], preferred_element_type=F32)
        qm = _rope(qm[:, :nk], qm[:, nk:], cm_ref[...], sm_ref[...])
        qm_o[0] = (qm * ((MLA_NOPE_DIM + MLA_ROPE_DIM) ** -0.5 * LOG2E)).astype(BF16)
        sg_o[0] = _silu(take(d)).astype(BF16)


def _prep(x, ada, nw, w, qn, kn, cqn, ckvn, wuq, wukv, o2q, o2k, tabs, *, latent, tl):
    b, l, d = x.shape
    grid = (l // tl, b)
    row = lambda i, j: (j, i, 0)
    const = lambda i, j: (0, 0)
    tab = lambda i, j: (i, 0)
    xspec = pl.BlockSpec((1, tl, d), row)
    adaspec = pl.BlockSpec((1, 1, ada.shape[-1]), lambda i, j: (j, 0, 0))

    def full(a):
        return pl.BlockSpec(a.shape, const)

    def out(width):
        return (pl.BlockSpec((1, tl, width), row), jax.ShapeDtypeStruct((b, l, width), BF16))

    def out_t(width):
        return (pl.BlockSpec((1, width, tl), lambda i, j: (j, 0, i)), jax.ShapeDtypeStruct((b, width, l), BF16))

    if latent:
        ca, sa, cm, sm = tabs
        ins = [x, ada, nw, w, qn, kn, cqn, ckvn, wuq, wukv, o2q, o2k, ca, sa, cm, sm]
        in_specs = [xspec, adaspec, full(nw), full(w), full(qn), full(kn), full(cqn), full(ckvn),
                    full(wuq), full(wukv), full(o2q), full(o2k)] + [pl.BlockSpec((tl, t.shape[1]), tab) for t in tabs]
        outs = [out(_QA_W), out(_KA_W), out_t(_VA_W), out(MLA_HEADS * LANE), out(MLA_HEADS * LANE),
                out_t(MLA_HEADS * LANE), out(d)]
    else:
        ins = [x, ada, nw, w, kn, ckvn, wukv, o2k]
        in_specs = [xspec, adaspec, full(nw), full(w), full(kn), full(ckvn), full(wukv), full(o2k)]
        outs = [out(_KA_W), out_t(_VA_W), out(MLA_HEADS * LANE), out_t(MLA_HEADS * LANE)]
    return pl.pallas_call(
        functools.partial(_prep_kernel, latent=latent, d=d),
        grid=grid, in_specs=in_specs,
        out_specs=[o[0] for o in outs], out_shape=[o[1] for o in outs],
        compiler_params=_cparams(("arbitrary", "arbitrary")),
        name="prep_lat" if latent else "prep_ctx",
    )(*ins)


_ATT_SUB = 256


def _attn_kernel(q_ref, kc_ref, vc_ref, kl_ref, vl_ref, o_ref, s0, s1, p0, p1, acc_ref, *, tk, q_shared):
    tq = q_ref.shape[1]
    lk, lc = kl_ref.shape[1], kc_ref.shape[1]
    nblk = lk // tk
    t = _ATT_SUB
    dn = (((1,), (1,)), ((), ()))
    lanes = lambda e: slice(e * LANE, (e + 1) * LANE)

    def q_sub(e, qc):
        return q_ref[0, qc * t:(qc + 1) * t, lanes(0 if q_shared else e)]

    def vrows(v_ref, e, cols):
        return v_ref[0, :, cols] if q_shared else v_ref[0, lanes(e), cols]

    def mxu_phase(s_next, k_next, p_prev, v_prev, nk_prev, alpha):
        for qc in range(tq // t):
            cols = slice(qc * t, (qc + 1) * t)
            pv = [None, None]
            for kr in range(tk // t):
                rows = slice(kr * t, (kr + 1) * t)
                for e in range(2):
                    if s_next is not None:
                        s_next[e, rows, cols] = lax.dot_general(k_next(e, kr), q_sub(e, qc), dn,
                                                                preferred_element_type=F32)
                    if p_prev is not None and kr < nk_prev:
                        d = jnp.dot(v_prev(e, kr), p_prev(e, rows, cols), preferred_element_type=F32)
                        pv[e] = d if pv[e] is None else pv[e] + d
            if p_prev is not None:
                for e in range(2):
                    acc_ref[e, :, cols] = alpha[e][:, cols] * acc_ref[e, :, cols] + pv[e]

    def softmax_phase(s_cur, p_cur, m):
        m_new, alpha, ps = [], [], []
        for e in range(2):
            s = s_cur[e]
            mn = jnp.maximum(m[e], jnp.max(s, axis=0, keepdims=True))
            p = jnp.exp2(s - mn).astype(BF16)
            if p_cur is not None:
                p_cur[e] = p
            m_new.append(mn)
            alpha.append(jnp.exp2(m[e] - mn))
            ps.append(p)
        return m_new, alpha, ps

    def from_ref(ref):
        return lambda e, rows, cols: ref[e, rows, cols]

    def k_lat(r):
        return lambda e, kr: kl_ref[0, pl.ds(pl.multiple_of(r + kr * t, t), t), lanes(e)]

    def v_lat(r):
        return lambda e, kr: vrows(vl_ref, e, pl.ds(pl.multiple_of(r + kr * t, t), t))

    acc_ref[...] = jnp.zeros_like(acc_ref)
    ones = [jnp.ones((1, tq), F32)] * 2
    m = [jnp.full((1, tq), -1e30, F32)] * 2

    s_ctx = [lax.dot_general(kc_ref[0, :, lanes(e)], q_ref[0, :, lanes(0 if q_shared else e)], dn,
                             preferred_element_type=F32) for e in range(2)]
    mxu_phase(s0, k_lat(0), None, None, 0, None)
    m, _, p_ctx = softmax_phase(s_ctx, None, m)
    mxu_phase(s1, k_lat(tk), lambda e, rows, cols: p_ctx[e][rows, cols],
              lambda e, kr: vrows(vc_ref, e, slice(kr * t, (kr + 1) * t)), lc // t, ones)
    m, alpha, _ = softmax_phase(s0, p0, m)

    def body(j, carry):
        m, alpha = list(carry[0]), list(carry[1])
        r = pl.multiple_of(2 * j * tk, tk)
        mxu_phase(s0, k_lat(r + 2 * tk), from_ref(p0), v_lat(r), tk // t, alpha)
        m, alpha, _ = softmax_phase(s1, p1, m)
        mxu_phase(s1, k_lat(r + 3 * tk), from_ref(p1), v_lat(r + tk), tk // t, alpha)
        m, alpha, _ = softmax_phase(s0, p0, m)
        return tuple(m), tuple(alpha)

    m, alpha = lax.fori_loop(0, (nblk - 2) // 2, body, (tuple(m), tuple(alpha)))
    mxu_phase(None, None, from_ref(p0), v_lat((nblk - 2) * tk), tk // t, alpha)
    m, alpha, _ = softmax_phase(s1, p1, list(m))
    mxu_phase(None, None, from_ref(p1), v_lat((nblk - 1) * tk), tk // t, alpha)
    dv = LANE // 2
    outs = [acc_ref[e, :dv, :] / acc_ref[e, dv:dv + 1, :] for e in range(2)]
    o_ref[0] = jnp.concatenate(outs, axis=0).T.astype(o_ref.dtype)


def _attention(q, kc, vc, kl, vl, *, q_shared, kv_group, tq, tk):
    b, l, _ = q.shape
    wq = LANE if q_shared else 2 * LANE
    pairs = q.shape[-1] // wq
    lc = kc.shape[1]
    assert (l // tk) % 2 == 0, "key blocks are consumed two per loop trip"
    vw = LANE if q_shared else 2 * LANE
    kv = lambda bi, j, i: (bi, 0, j // kv_group)
    vt = lambda bi, j, i: (bi, j // kv_group, 0)
    return pl.pallas_call(
        functools.partial(_attn_kernel, tk=tk, q_shared=q_shared),
        grid=(b, pairs, l // tq),
        in_specs=[pl.BlockSpec((1, tq, wq), lambda bi, j, i: (bi, i, j)),
                  pl.BlockSpec((1, lc, 2 * LANE), kv), pl.BlockSpec((1, vw, lc), vt),
                  pl.BlockSpec((1, l, 2 * LANE), kv), pl.BlockSpec((1, vw, l), vt)],
        out_specs=pl.BlockSpec((1, tq, LANE), lambda bi, j, i: (bi, i, j)),
        out_shape=jax.ShapeDtypeStruct((b, l, pairs * LANE), BF16),
        scratch_shapes=[pltpu.VMEM((2, tk, tq), F32), pltpu.VMEM((2, tk, tq), F32),
                        pltpu.VMEM((2, tk, tq), BF16), pltpu.VMEM((2, tk, tq), BF16),
                        pltpu.VMEM((2, LANE, tq), F32)],
        compiler_params=_cparams(("arbitrary", "arbitrary", "arbitrary")),
        name="attn_gqa" if q_shared else "attn_mla",
    )(q, kc, vc, kl, vl)


def _out0_kernel(x_ref, oa_ref, om_ref, sg_ref, w_ref, ada_ref, o_ref, *, d):
    o = jnp.concatenate([oa_ref[0], om_ref[0]], axis=-1).astype(F32) * sg_ref[0].astype(F32)
    y = jnp.dot(o.astype(BF16), w_ref[...], preferred_element_type=F32)
    o_ref[0] = x_ref[0] + ada_ref[0][:, 2 * d:3 * d] * y


def _out0(x, oa, om, sg, w, ada, *, tl):
    b, l, d = x.shape
    row = lambda bi, i: (bi, i, 0)
    return pl.pallas_call(
        functools.partial(_out0_kernel, d=d),
        grid=(b, l // tl),
        in_specs=[pl.BlockSpec((1, tl, d), row), pl.BlockSpec((1, tl, oa.shape[-1]), row),
                  pl.BlockSpec((1, tl, om.shape[-1]), row), pl.BlockSpec((1, tl, d), row),
                  pl.BlockSpec(w.shape, lambda bi, i: (0, 0)),
                  pl.BlockSpec((1, 1, ada.shape[-1]), lambda bi, i: (bi, 0, 0))],
        out_specs=pl.BlockSpec((1, tl, d), row),
        out_shape=jax.ShapeDtypeStruct((b, l, d), F32),
        compiler_params=_cparams(("arbitrary", "arbitrary")),
        name="out0",
    )(x, oa, om, sg, w, ada)


_HALO = 8


def _hyin_kernel(x_ref, xp_ref, xn_ref, ada_ref, nw_ref, w_ref, cw_ref, cb_ref, *o_refs, d):
    i = pl.program_id(1)
    tl = x_ref.shape[1]
    n_conv = len(o_refs) - 1
    ada = ada_ref[0]
    shift, scale = ada[:, :d], ada[:, d:2 * d]

    def mod(x):
        return (_row_rms(x) * nw_ref[...]) * (1.0 + scale) + shift

    hp = mod(xp_ref[0]) * (i > 0).astype(F32)
    hn = mod(xn_ref[0]) * (i < pl.num_programs(1) - 1).astype(F32)
    h = jnp.concatenate([mod(x_ref[0]), hp, hn], axis=0).astype(BF16)
    rows = lax.broadcasted_iota(jnp.int32, (tl, d), 0)
    cw = cw_ref[...]
    for n in range(n_conv + 1):
        p = jnp.dot(h, w_ref[:, n * d:(n + 1) * d], preferred_element_type=F32)
        pm = p[0:tl]
        if n < n_conv:
            prev = jnp.where(rows == 0, p[tl + _HALO - 1:tl + _HALO], pltpu.roll(pm, 1, axis=0))
            nxt = jnp.where(rows == tl - 1, p[tl + _HALO:tl + _HALO + 1], pltpu.roll(pm, tl - 1, axis=0))
            c0, c1, c2 = (cw[j:j + 1, n * d:(n + 1) * d] for j in range(HY_SHORT))
            out = prev * c0 + pm * c1 + nxt * c2 + cb_ref[:, n * d:(n + 1) * d]
        else:
            out = _silu(pm)
        o_refs[n][0] = out.astype(o_refs[n].dtype)


def _hyin(x, ada, nw, w, cw, cb, *, tl):
    b, l, d = x.shape
    ng = w.shape[1] // d
    tb = tl // _HALO
    nb = l // _HALO
    const = lambda bi, i: (0, 0)
    blk = pl.BlockSpec((1, tl, d), lambda bi, i: (bi, i, 0))
    return pl.pallas_call(
        functools.partial(_hyin_kernel, d=d),
        grid=(b, l // tl),
        in_specs=[blk,
                  pl.BlockSpec((1, _HALO, d), lambda bi, i: (bi, jnp.maximum(i * tb - 1, 0), 0)),
                  pl.BlockSpec((1, _HALO, d), lambda bi, i: (bi, jnp.minimum((i + 1) * tb, nb - 1), 0)),
                  pl.BlockSpec((1, 1, ada.shape[-1]), lambda bi, i: (bi, 0, 0)),
                  pl.BlockSpec(nw.shape, const), pl.BlockSpec(w.shape, const),
                  pl.BlockSpec(cw.shape, const), pl.BlockSpec(cb.shape, const)],
        out_specs=[blk] * ng,
        out_shape=[jax.ShapeDtypeStruct((b, l, d), F32)] * ng,
        compiler_params=_cparams(("arbitrary", "arbitrary")),
        name="hy_in",
    )(x, x, x, ada, nw, w, cw, cb)


def _filt_kernel(emb_ref, w1_ref, b1_ref, w2_ref, b2_ref, w3_ref, b3_ref, fr_ref, dl_ref, h_o, s_o, *, reps):
    emb = emb_ref[...]
    fr = fr_ref[...]
    hid = jnp.sin(fr * (jnp.dot(emb, w1_ref[...], precision=HI, preferred_element_type=F32) + b1_ref[...]))
    hid = jnp.sin(fr * (jnp.dot(hid, w2_ref[...], precision=HI, preferred_element_type=F32) + b2_ref[...]))
    h = jnp.dot(hid, w3_ref[...], precision=HI, preferred_element_type=F32) + b3_ref[...]
    win = jnp.exp(-emb[:, 0:1] * dl_ref[...])
    hw = h * jnp.concatenate([win] * reps, axis=-1)
    rows = lax.broadcasted_iota(jnp.int32, hw.shape, 0) + pl.program_id(0) * hw.shape[0]
    hw = jnp.where((rows == 0) & (_lane_iota(hw.shape) >= hw.shape[1] // 2), 0.0, hw)
    h_o[...] = hw

    @pl.when(pl.program_id(0) == 0)
    def _():
        s_o[...] = jnp.zeros_like(s_o)

    s_o[...] += jnp.sum(jnp.abs(hw), axis=0, keepdims=True)


def _filters(emb, w1, b1, w2, b2, w3, b3, fr, dl, *, tl):
    l = emb.shape[0]
    wo = w3.shape[1]
    const = lambda i: (0, 0)
    full = lambda a: pl.BlockSpec(a.shape, const)
    return pl.pallas_call(
        functools.partial(_filt_kernel, reps=wo // dl.shape[1]),
        grid=(l // tl,),
        in_specs=[pl.BlockSpec((tl, emb.shape[1]), lambda i: (i, 0)), full(w1), full(b1), full(w2), full(b2),
                  full(w3), full(b3), full(fr), full(dl)],
        out_specs=[pl.BlockSpec((tl, wo), lambda i: (i, 0)), pl.BlockSpec((1, wo), const)],
        out_shape=[jax.ShapeDtypeStruct((l, wo), F32), jax.ShapeDtypeStruct((1, wo), F32)],
        compiler_params=_cparams(("arbitrary",)),
        name="hy_filter",
    )(emb, w1, b1, w2, b2, w3, b3, fr, dl)


def _cmul_const(a, ang):
    ar, ai = a
    q = ang / (0.5 * math.pi)
    if abs(q - round(q)) < 1e-12:
        return [(ar, ai), (-ai, ar), (-ar, -ai), (ai, -ar)][int(round(q)) % 4]
    c, s = math.cos(ang), math.sin(ang)
    return (ar * c - ai * s, ar * s + ai * c)


def _fft_dif(x):
    x = list(x)
    n = len(x)
    half = n // 2
    while half >= 1:
        for base in range(0, n, 2 * half):
            for j in range(half):
                a, b = x[base + j], x[base + j + half]
                ang = -math.pi * j / half
                if b is None:
                    x[base + j + half] = None if a is None else _cmul_const(a, ang)
                else:
                    x[base + j] = (a[0] + b[0], a[1] + b[1])
                    x[base + j + half] = _cmul_const((a[0] - b[0], a[1] - b[1]), ang)
        half //= 2
    return x


def _ifft_dit(x, keep):
    x = list(x)
    n = len(x)
    half = 1
    while half <= n // 2:
        last = half == n // 2
        for base in range(0, n, 2 * half):
            for j in range(half):
                a = x[base + j]
                b = _cmul_const(x[base + j + half], math.pi * j / half)
                x[base + j] = (a[0] + b[0], a[1] + b[1])
                if not last or base + j + half < keep:
                    x[base + j + half] = (a[0] - b[0], a[1] - b[1])
        half *= 2
    return x[:keep]


def _outer_fwd_kernel(u_ref, o_ref, *, real_input):
    n1 = o_ref.shape[2]
    if real_input:
        x = [(u_ref[i], jnp.zeros_like(u_ref[i])) for i in range(n1 // 2)]
    else:
        x = [(u_ref[0, 0, i], u_ref[0, 1, i]) for i in range(n1 // 2)]
    for s, (re, im) in enumerate(_fft_dif(x + [None] * (n1 // 2))):
        o_ref[0, 0, s] = re
        o_ref[0, 1, s] = im


def _outer_fwd(u, *, real_input, tn2, ct, nseq=1):
    n2 = u.shape[-2]
    if real_input:
        p, c, n1 = nseq, u.shape[-1] // nseq, 2 * u.shape[0]
        in_spec = pl.BlockSpec((n1 // 2, tn2, ct), lambda pi, r, j: (0, r, pi * (c // ct) + j))
    else:
        p, c, n1 = u.shape[0], u.shape[-1], 2 * u.shape[2]
        in_spec = pl.BlockSpec((1, 2, n1 // 2, tn2, ct), lambda pi, r, j: (pi, 0, 0, r, j))
    return pl.pallas_call(
        functools.partial(_outer_fwd_kernel, real_input=real_input),
        grid=(p, n2 // tn2, c // ct), in_specs=[in_spec],
        out_specs=pl.BlockSpec((1, 2, n1, tn2, ct), lambda pi, r, j: (pi, 0, 0, r, j)),
        out_shape=jax.ShapeDtypeStruct((p, 2, n1, n2, c), F32),
        compiler_params=_cparams(("arbitrary", "arbitrary", "arbitrary")),
        name="hy_outer_filt" if real_input else "hy_outer_fwd",
    )(u)


def _outer_inv_kernel(a_ref, u_ref, g_ref, sk_ref, o_ref, *next_ref):
    n1 = a_ref.shape[2]
    y = _ifft_dit([(a_ref[0, 0, s], a_ref[0, 1, s]) for s in range(n1)], n1 // 2)
    sk = sk_ref[...]
    z = [(g_ref[0, 0, i] * (re + u_ref[0, 0, i] * sk), g_ref[0, 1, i] * (im + u_ref[0, 1, i] * sk))
         for i, (re, im) in enumerate(y)]
    for i, (re, im) in enumerate(z):
        o_ref[0, 0, i] = re
        o_ref[0, 1, i] = im
    if next_ref:
        for s, (re, im) in enumerate(_fft_dif(z + [None] * (n1 // 2))):
            next_ref[0][0, 0, s] = re
            next_ref[0][0, 1, s] = im


def _outer_inv(a, u, gate, skip, *, tn2, ct, with_next):
    p, _, n1, n2, c = a.shape
    blk = lambda rows: pl.BlockSpec((1, 2, rows, tn2, ct), lambda pi, r, j: (pi, 0, 0, r, j))
    out_specs, out_shape = [blk(n1 // 2)], [jax.ShapeDtypeStruct(u.shape, F32)]
    if with_next:
        out_specs.append(blk(n1))
        out_shape.append(jax.ShapeDtypeStruct(a.shape, F32))
    return pl.pallas_call(
        _outer_inv_kernel,
        grid=(p, n2 // tn2, c // ct),
        in_specs=[blk(n1), blk(n1 // 2), blk(n1 // 2), pl.BlockSpec((1, ct), lambda pi, r, j: (0, j))],
        out_specs=out_specs, out_shape=out_shape,
        compiler_params=_cparams(("arbitrary", "arbitrary", "arbitrary")),
        name="hy_outer_inv_fwd" if with_next else "hy_outer_inv",
    )(a, u, gate, skip)


def _dot3(m3, x):
    hi = x.astype(BF16)
    lo = (x - hi.astype(F32)).astype(BF16)
    return jnp.dot(m3, jnp.concatenate([hi, hi, lo], axis=0), preferred_element_type=F32)


def _mid_kernel(a_ref, g_ref, *rest, spectrum):
    np_, n2, ct = a_ref.shape[0], a_ref.shape[3], a_ref.shape[4]
    xs = [_dot3(g_ref[0], a_ref[p, :, 0].reshape(2 * n2, ct)) for p in range(np_)]
    if spectrum:
        sc_ref, o_ref = rest
        f, bw = xs
        sc = sc_ref[...]
        o_ref[0, 0, 0] = (f[:n2] + bw[:n2]) * sc
        o_ref[0, 1, 0] = (f[n2:] - bw[n2:]) * sc
    else:
        gt_ref, kf_ref, o_ref = rest
        kr, ki = kf_ref[0, 0, 0], kf_ref[0, 1, 0]
        ys = [jnp.concatenate([x[:n2] * kr - x[n2:] * ki, x[:n2] * ki + x[n2:] * kr], axis=0) for x in xs]
        for p in range(np_):
            o_ref[p, :, 0] = _dot3(gt_ref[0], ys[p]).reshape(2, n2, ct)


def _mid(a, g, gt=None, kf=None, scale=None, *, ct, kf_col0=0):
    p, _, n1, n2, c = a.shape
    ablk = pl.BlockSpec((p, 2, 1, n2, ct), lambda k, j: (0, 0, k, 0, j))
    gblk = pl.BlockSpec((1,) + g.shape[1:], lambda k, j: (k, 0, 0))
    oblk, oshape = ablk, a.shape
    if kf is None:
        ins = [a, g, scale]
        in_specs = [ablk, gblk, pl.BlockSpec((1, ct), lambda k, j: (0, j))]
        oblk, oshape = pl.BlockSpec((1, 2, 1, n2, ct), lambda k, j: (0, 0, k, 0, j)), (1,) + a.shape[1:]
    else:
        ins = [a, g, gt, kf]
        in_specs = [ablk, gblk, gblk,
                    pl.BlockSpec((1, 2, 1, n2, ct), lambda k, j: (0, 0, k, 0, kf_col0 + j))]
    return pl.pallas_call(
        functools.partial(_mid_kernel, spectrum=kf is None),
        grid=(n1, c // ct), in_specs=in_specs, out_specs=oblk,
        out_shape=jax.ShapeDtypeStruct(oshape, F32),
        compiler_params=_cparams(("arbitrary", "arbitrary")),
        name="hy_spectrum" if kf is None else "hy_mid",
    )(*ins)


def _dft_tables(l):
    n = 2 * l
    n2 = DFT_N2
    n1 = n // n2
    bits = n1.bit_length() - 1
    k1 = np.array([int(format(s, "0%db" % bits)[::-1], 2) for s in range(n1)], dtype=np.float64)
    kk = k1[:, None, None] + n1 * np.arange(n2, dtype=np.float64)[None, :, None]
    th = 2.0 * np.pi * kk * np.arange(n2, dtype=np.float64)[None, None, :] / n
    c, s = np.cos(th), np.sin(th)
    g = np.concatenate([np.concatenate([c, s], axis=2), np.concatenate([-s, c], axis=2)], axis=1)

    def split3(m):
        m = jnp.asarray(m, F32)
        hi = m.astype(BF16)
        lo = (m - hi.astype(F32)).astype(BF16)
        return jnp.concatenate([hi, lo, hi], axis=2)

    return split3(g), split3(np.transpose(g, (0, 2, 1)))


def _pair_view(u, n1h):
    b, l, c = u.shape
    return u.reshape(b // 2, 2, n1h, l // n1h, c)


def _long_convs(u, gates, skips, kf, tabs, *, ct):
    g, gt = tabs
    c = u.shape[-1]
    a = _outer_fwd(u, real_input=False, tn2=32, ct=256)
    z = u
    for o, (gate, skip) in enumerate(zip(gates, skips)):
        a = _mid(a, g, gt, kf, ct=ct, kf_col0=o * (c // ct))
        last = o == len(gates) - 1
        res = _outer_inv(a, z, gate, skip, tn2=32, ct=256, with_next=not last)
        z, a = (res[0], None) if last else res
    return z


def _out1_kernel(x_ref, z_ref, sg_ref, w_ref, ada_ref, fw_ref, o_ref, *, d):
    y = jnp.dot((z_ref[0] * sg_ref[0]).astype(BF16), w_ref[...], preferred_element_type=F32)
    x = x_ref[0] + ada_ref[0][:, 2 * d:3 * d] * y
    o_ref[0] = _row_rms(x) * fw_ref[...]


def _out1(x, z, sg, w, ada, fw, *, tl):
    b, l, d = x.shape
    row = lambda bi, i: (bi, i, 0)
    blk = pl.BlockSpec((1, tl, d), row)
    return pl.pallas_call(
        functools.partial(_out1_kernel, d=d),
        grid=(b, l // tl),
        in_specs=[blk, blk, blk, pl.BlockSpec(w.shape, lambda bi, i: (0, 0)),
                  pl.BlockSpec((1, 1, ada.shape[-1]), lambda bi, i: (bi, 0, 0)),
                  pl.BlockSpec(fw.shape, lambda bi, i: (0, 0))],
        out_specs=blk,
        out_shape=jax.ShapeDtypeStruct((b, l, d), F32),
        compiler_params=_cparams(("arbitrary", "arbitrary")),
        name="out1",
    )(x, z, sg, w, ada, fw)


def _swap_cols(w, q):
    return w[..., np.arange(w.shape[-1]) ^ q]


def _pack_attn_w_in(w):
    d = w.shape[0]
    o = 0
    wq = w[:, o:o + 512]; o += 512
    wk = w[:, o:o + 128]; o += 128
    wv = w[:, o:o + 128]; o += 128
    wcq = w[:, o:o + MLA_Q_RANK]; o += MLA_Q_RANK
    wckv = w[:, o:o + MLA_KV_RANK]; o += MLA_KV_RANK
    wkpe = w[:, o:o + MLA_ROPE_DIM]; o += MLA_ROPE_DIM
    wg = w[:, o:]
    qa, qm = GQA_HEAD_DIM // 4, MLA_ROPE_DIM // 4

    def pe_chunk(wp):
        return jnp.concatenate([jnp.zeros((d, 64), w.dtype), wp, jnp.zeros((d, 32), w.dtype)], axis=1)

    kpe, kpe_sw = pe_chunk(wkpe), pe_chunk(_swap_cols(wkpe, qm))
    kv_part = jnp.concatenate([wk, wv, wckv, kpe], axis=1)
    lat = jnp.concatenate([wq, _swap_cols(wq, qa), wk, _swap_cols(wk, qa), wv, wcq, wckv, kpe, kpe_sw, wg], axis=1)
    return lat.astype(BF16), kv_part.astype(BF16)


def _pack_mla_up(w_uq, w_ukv):
    dq = MLA_NOPE_DIM + MLA_ROPE_DIM
    r = w_uq.shape[0]
    z = lambda n: jnp.zeros((r, n), w_uq.dtype)
    uq, uq_sw = [], []
    for h in range(MLA_HEADS):
        nope, pe = w_uq[:, dq * h:dq * h + MLA_NOPE_DIM], w_uq[:, dq * h + MLA_NOPE_DIM:dq * (h + 1)]
        uq += [nope, pe, z(LANE - dq)]
        uq_sw += [z(MLA_NOPE_DIM), _swap_cols(pe, MLA_ROPE_DIM // 4), z(LANE - dq)]
    dkv = MLA_NOPE_DIM + MLA_V_DIM
    kn = jnp.concatenate(
        [jnp.concatenate([w_ukv[:, dkv * h:dkv * h + MLA_NOPE_DIM],
                          jnp.zeros((w_ukv.shape[0], LANE - MLA_NOPE_DIM), w_ukv.dtype)], axis=1)
         for h in range(MLA_HEADS)], axis=1)
    vm = jnp.concatenate(
        [jnp.concatenate([w_ukv[:, dkv * h + MLA_NOPE_DIM:dkv * (h + 1)],
                          jnp.zeros((w_ukv.shape[0], LANE - MLA_V_DIM), w_ukv.dtype)], axis=1)
         for h in range(MLA_HEADS)], axis=1)
    return jnp.concatenate(uq + uq_sw, axis=1).astype(BF16), jnp.concatenate([kn, vm], axis=1).astype(BF16)


def _head_ones2(width):
    i = np.arange(width) // GQA_HEAD_DIM
    blk = (i[:, None] == i[None, :]).astype(np.float32)
    return jnp.asarray(np.concatenate([blk, blk], axis=0), BF16)


def _rope_tables(l):
    rows = (jnp.arange(l, dtype=jnp.int32) // GRID_W).astype(F32)[:, None]
    cols = (jnp.arange(l, dtype=jnp.int32) % GRID_W).astype(F32)[:, None]

    def tab(rot_dim):
        q = rot_dim // 4
        inv = ROPE_BASE ** (-jnp.arange(q, dtype=F32) / q)
        ar, ac = rows * inv, cols * inv
        cos = jnp.concatenate([jnp.cos(ar)] * 2 + [jnp.cos(ac)] * 2, axis=1)
        sin = jnp.concatenate([-jnp.sin(ar), jnp.sin(ar), -jnp.sin(ac), jnp.sin(ac)], axis=1)
        return cos, sin

    ca, sa = tab(GQA_HEAD_DIM)
    ca = jnp.concatenate([ca] * (LANE // GQA_HEAD_DIM), axis=1)
    sa = jnp.concatenate([sa] * (LANE // GQA_HEAD_DIM), axis=1)
    cm, sm = tab(MLA_ROPE_DIM)
    one, zero = jnp.ones((l, 1), F32), jnp.zeros((l, 1), F32)
    cm = jnp.concatenate([jnp.tile(one, (1, 64)), cm, jnp.tile(one, (1, 32))], axis=1)
    sm = jnp.concatenate([jnp.tile(zero, (1, 64)), sm, jnp.tile(zero, (1, 32))], axis=1)
    return ca, sa, cm, sm


def _pad2(a, r, c):
    return jnp.pad(a, ((0, r - a.shape[0]), (0, c - a.shape[1])))


def kernel(x, c, ctx, c_ctx, ada_w, ada_b, norm_w, attn_w_in, attn_q_norm, attn_k_norm, mla_q_norm, mla_kv_norm, mla_w_uq, mla_w_ukv, attn_w_out, hy_w_in, hy_conv_w, hy_conv_b, hy_ffn_w1, hy_ffn_b1, hy_ffn_w2, hy_ffn_b2, hy_ffn_w3, hy_ffn_b3, hy_freq, hy_skip, hy_w_out, final_norm_w):
    b, l, d = x.shape
    lc = ctx.shape[1]
    tl = min(256, l)

    rows = -(-(b + 1) // 8) * 8
    cs = jnp.concatenate([c, c_ctx[None, :], jnp.zeros((rows - b - 1, d), F32)], axis=0)
    ada = _ada(cs, ada_w, ada_b)
    ada_lat = [ada[i, :b].reshape(b, 1, 3 * d) for i in range(ada.shape[0])]
    ada_ctx0 = jnp.broadcast_to(ada[0, b].reshape(1, 1, 3 * d), (b, 1, 3 * d))

    w_lat, w_kv = _pack_attn_w_in(attn_w_in[0])
    wuq, wukv = _pack_mla_up(mla_w_uq[0], mla_w_ukv[0])
    nw0 = norm_w[0].reshape(1, d)
    def norm_rows(wn, width):
        sw = _swap_cols(wn, GQA_HEAD_DIM // 4)
        return jnp.stack([jnp.tile(wn, width // GQA_HEAD_DIM), jnp.tile(sw, width // GQA_HEAD_DIM)])

    qn, kn = norm_rows(attn_q_norm[0], _QA_W), norm_rows(attn_k_norm[0], LANE)
    cqn = mla_q_norm[0].reshape(1, MLA_Q_RANK)
    ckvn = mla_kv_norm[0].reshape(1, MLA_KV_RANK)
    o2q, o2k = _head_ones2(_QA_W), _head_ones2(LANE)
    tabs = _rope_tables(l)
    qa, ka, va, qm, km, vm, sg = _prep(x, ada_lat[0], nw0, w_lat, qn, kn, cqn, ckvn, wuq, wukv, o2q, o2k, tabs,
                                       latent=True, tl=tl)
    kac, vac, kmc, vmc = _prep(ctx, ada_ctx0, nw0, w_kv, None, kn, None, ckvn, None, wukv, None, o2k, None,
                               latent=False, tl=min(tl, lc))
    tq, tk = min(2048, l), min(512, l // 2)
    oa = _attention(qa, kac, vac, ka, va, q_shared=True, kv_group=2, tq=tq, tk=tk)
    om = _attention(qm, kmc, vmc, km, vm, q_shared=False, kv_group=1, tq=tq, tk=tk)
    x1 = _out0(x, oa, om, sg, attn_w_out[0].astype(BF16), ada_lat[0], tl=tl)

    nw1 = norm_w[1].reshape(1, d)
    u = _hyin(x1, ada_lat[1], nw1, hy_w_in[0].astype(BF16), hy_conv_w[0], hy_conv_b[0].reshape(1, -1),
              tl=min(512, l))
    n1 = 2 * l // DFT_N2
    n1h = n1 // 2
    tabs_d = _dft_tables(l)

    t = jnp.linspace(0.0, 1.0, l, dtype=F32)[:, None]
    wpos = (2.0 * math.pi / l) * jnp.arange(l, dtype=F32)[:, None]
    bands = jnp.linspace(1e-4, HY_BANDS - 1, HY_BANDS, dtype=F32)
    emb = jnp.concatenate([t, jnp.cos(wpos * bands), -jnp.sin(wpos * bands)], axis=-1)
    deltas = jnp.abs(jnp.linspace(math.log(HY_DECAY_TARGET) / HY_SLOW_DECAY,
                                  math.log(HY_DECAY_TARGET) / HY_FAST_DECAY, d, dtype=F32)).reshape(1, d)
    wf = hy_ffn_w3.shape[-1]
    oc = HY_ORDER * d

    def by_direction(a):
        return a.reshape(-1, HY_ORDER, 2, d).transpose(0, 2, 1, 3).reshape(-1, wf)

    hw, asum = _filters(_pad2(emb, l, LANE), _pad2(hy_ffn_w1[0], LANE, LANE), _pad2(hy_ffn_b1[0][None], 1, LANE),
                        _pad2(hy_ffn_w2[0], LANE, LANE), _pad2(hy_ffn_b2[0][None], 1, LANE),
                        _pad2(by_direction(hy_ffn_w3[0]), LANE, wf), by_direction(hy_ffn_b3[0][None]),
                        _pad2(hy_freq[0][None], 1, LANE), deltas, tl=tl)
    l1 = asum[:, :oc] + asum[:, oc:]
    ct = 512
    af = _outer_fwd(hw.reshape(n1h, DFT_N2, wf), real_input=True, nseq=2, tn2=32, ct=256)
    kf = _mid(af, tabs_d[0], scale=1.0 / (l1 * (2 * l)), ct=oc)

    v2, x1g, x2g = (_pair_view(u[i], n1h) for i in range(3))
    z = _long_convs(v2, [x1g, x2g], [hy_skip[0, o:o + 1] for o in range(HY_ORDER)], kf, tabs_d, ct=ct)
    z = z.reshape(b, l, d)
    return _out1(x1, z, u[3], hy_w_out[0].astype(BF16), ada_lat[1], final_norm_w.reshape(1, d), tl=tl)
```

```python
import functools
import math

import numpy as np
import jax
import jax.numpy as jnp
from jax import lax
from jax.experimental import pallas as pl
from jax.experimental.pallas import tpu as pltpu

EPS = 1e-6
GRID_W = 64
ROPE_BASE = 10000.0
GQA_HEADS, GQA_KV_HEADS, GQA_HEAD_DIM = 8, 2, 64
MLA_HEADS, MLA_Q_RANK, MLA_KV_RANK = 8, 256, 128
MLA_NOPE_DIM, MLA_ROPE_DIM, MLA_V_DIM = 64, 32, 64
HY_ORDER, HY_SHORT, HY_BANDS, HY_FFN = 2, 3, 16, 64
HY_FAST_DECAY, HY_SLOW_DECAY, HY_DECAY_TARGET = 0.3, 1.5, 1e-2
LANE = 128
DFT_N2 = 128
VMEM_LIMIT = 56 * 1024 * 1024
LOG2E = 1.4426950408889634
HI = lax.Precision.HIGHEST
F32 = jnp.float32
BF16 = jnp.bfloat16


def _cparams(sem):
    return pltpu.CompilerParams(dimension_semantics=sem, vmem_limit_bytes=VMEM_LIMIT)


def _per_chunk(fn, *arrs):
    width = arrs[0].shape[-1]
    outs = [fn(*[a[:, c:c + LANE] for a in arrs]) for c in range(0, width, LANE)]
    return outs[0] if len(outs) == 1 else jnp.concatenate(outs, axis=-1)


def _lane_iota(shape):
    return lax.broadcasted_iota(jnp.int32, shape, len(shape) - 1)


def _head_rsqrt(x, ones2):
    ss = x * x
    hi = ss.astype(BF16)
    lo = (ss - hi.astype(F32)).astype(BF16)
    tot = jnp.dot(jnp.concatenate([hi, lo], axis=-1), ones2, preferred_element_type=F32)
    return lax.rsqrt(tot * (1.0 / GQA_HEAD_DIM) + EPS)


def _rope(x, x_sw, cos, sin_signed):
    return _per_chunk(lambda c, w: c * cos + w * sin_signed, x, x_sw)


def _row_rms(x):
    return x * lax.rsqrt(jnp.mean(x * x, axis=-1, keepdims=True) + EPS)


def _silu(x):
    return x * (1.0 / (1.0 + jnp.exp(-x)))


def _ada_kernel(c_ref, w_ref, b_ref, o_ref):
    s = _silu(c_ref[...])
    o_ref[0] = jnp.dot(s, w_ref[0], precision=HI, preferred_element_type=F32) + b_ref[0]


def _ada(cs, ada_w, ada_b):
    depth, d, d3 = ada_w.shape
    rows = cs.shape[0]
    nt = d3 // d
    return pl.pallas_call(
        _ada_kernel,
        grid=(depth, nt),
        in_specs=[pl.BlockSpec((rows, d), lambda i, j: (0, 0)),
                  pl.BlockSpec((1, d, d), lambda i, j: (i, 0, j)),
                  pl.BlockSpec((1, 1, d), lambda i, j: (i, 0, j))],
        out_specs=pl.BlockSpec((1, rows, d), lambda i, j: (i, 0, j)),
        out_shape=jax.ShapeDtypeStruct((depth, rows, d3), F32),
        compiler_params=_cparams(("arbitrary", "arbitrary")),
        name="ada",
    )(cs, ada_w, ada_b.reshape(depth, 1, d3))


_QA_W, _KA_W, _VA_W, _KPE_W = 512, 512, 256, 128


def _prep_kernel(*refs, latent, d):
    if latent:
        (x_ref, ada_ref, nw_ref, w_ref, qn_ref, kn_ref, cqn_ref, ckvn_ref, wuq_ref, wukv_ref, o2q_ref, o2k_ref,
         ca_ref, sa_ref, cm_ref, sm_ref,
         qa_o, ka_o, va_o, qm_o, km_o, vm_o, sg_o) = refs
    else:
        (x_ref, ada_ref, nw_ref, w_ref, kn_ref, ckvn_ref, wukv_ref, o2k_ref,
         ka_o, va_o, km_o, vm_o) = refs
    ada = ada_ref[0]
    shift, scale = ada[:, :d], ada[:, d:2 * d]
    h = (_row_rms(x_ref[0]) * nw_ref[...]) * (1.0 + scale) + shift
    p = jnp.dot(h.astype(BF16), w_ref[...], preferred_element_type=F32)
    off = 0

    def take(width):
        nonlocal off
        off += width
        return p[:, off - width:off]

    if latent:
        qa, qa_sw = take(_QA_W), take(_QA_W)
        k, k_sw = take(LANE), take(LANE)
    else:
        k = take(LANE)
    v = take(LANE)
    if latent:
        cq = take(MLA_Q_RANK)
    ckv = take(MLA_KV_RANK)
    kpe = take(LANE)

    kn = kn_ref[...]
    rk = _head_rsqrt(k, o2k_ref[...])
    k = k * rk * kn[0:1]
    if latent:
        k = _rope(k, k_sw * rk * kn[1:2], ca_ref[...], sa_ref[...])
        kpe = _rope(kpe, take(LANE), cm_ref[...], sm_ref[...])
    low = _lane_iota(k.shape) < (LANE // 2)
    k_x = pltpu.roll(k, LANE // 2, axis=1)
    v_x = pltpu.roll(v, LANE // 2, axis=1)
    zero = jnp.zeros_like(k)
    ka = [jnp.where(low, k, zero), jnp.where(low, zero, k_x), jnp.where(low, k_x, zero), jnp.where(low, zero, k)]
    ka_o[0] = jnp.concatenate(ka, axis=-1).astype(BF16)
    one64 = (_lane_iota(k.shape) == LANE // 2).astype(F32)
    va = jnp.concatenate([jnp.where(low, v, one64), jnp.where(low, v_x, one64)], axis=-1)
    va_o[0] = va.T.astype(BF16)
    ckv_n = (_row_rms(ckv) * ckvn_ref[...]).astype(BF16)
    kv = jnp.dot(ckv_n, wukv_ref[...], preferred_element_type=F32)
    nk = MLA_HEADS * LANE
    km_o[0] = _per_chunk(lambda c: c + kpe, kv[:, :nk]).astype(BF16)
    vm_o[0] = _per_chunk(lambda c: c + one64, kv[:, nk:]).T.astype(BF16)
    if latent:
        qn = qn_ref[...]
        rq = _head_rsqrt(qa, o2q_ref[...])
        qa = _rope(qa * rq * qn[0:1], qa_sw * rq * qn[1:2], ca_ref[...], sa_ref[...])
        qa_o[0] = (qa * (GQA_HEAD_DIM ** -0.5 * LOG2E)).astype(BF16)
        cq_n = (_row_rms(cq) * cqn_ref[...]).astype(BF16)
        qm = jnp.dot(cq_n, wuq_ref[...], preferred_element_type=F32)
        qm = _rope(qm[:, :nk], qm[:, nk:], cm_ref[...], sm_ref[...])
        qm_o[0] = (qm * ((MLA_NOPE_DIM + MLA_ROPE_DIM) ** -0.5 * LOG2E)).astype(BF16)
        sg_o[0] = _silu(take(d)).astype(BF16)


def _prep(x, ada, nw, w, qn, kn, cqn, ckvn, wuq, wukv, o2q, o2k, tabs, *, latent, tl):
    b, l, d = x.shape
    grid = (l // tl, b)
    row = lambda i, j: (j, i, 0)
    const = lambda i, j: (0, 0)
    tab = lambda i, j: (i, 0)
    xspec = pl.BlockSpec((1, tl, d), row)
    adaspec = pl.BlockSpec((1, 1, ada.shape[-1]), lambda i, j: (j, 0, 0))

    def full(a):
        return pl.BlockSpec(a.shape, const)

    def out(width):
        return (pl.BlockSpec((1, tl, width), row), jax.ShapeDtypeStruct((b, l, width), BF16))

    def out_t(width):
        return (pl.BlockSpec((1, width, tl), lambda i, j: (j, 0, i)), jax.ShapeDtypeStruct((b, width, l), BF16))

    if latent:
        ca, sa, cm, sm = tabs
        ins = [x, ada, nw, w, qn, kn, cqn, ckvn, wuq, wukv, o2q, o2k, ca, sa, cm, sm]
        in_specs = [xspec, adaspec, full(nw), full(w), full(qn), full(kn), full(cqn), full(ckvn),
                    full(wuq), full(wukv), full(o2q), full(o2k)] + [pl.BlockSpec((tl, t.shape[1]), tab) for t in tabs]
        outs = [out(_QA_W), out(_KA_W), out_t(_VA_W), out(MLA_HEADS * LANE), out(MLA_HEADS * LANE),
                out_t(MLA_HEADS * LANE), out(d)]
    else:
        ins = [x, ada, nw, w, kn, ckvn, wukv, o2k]
        in_specs = [xspec, adaspec, full(nw), full(w), full(kn), full(ckvn), full(wukv), full(o2k)]
        outs = [out(_KA_W), out_t(_VA_W), out(MLA_HEADS * LANE), out_t(MLA_HEADS * LANE)]
    return pl.pallas_call(
        functools.partial(_prep_kernel, latent=latent, d=d),
        grid=grid, in_specs=in_specs,
        out_specs=[o[0] for o in outs], out_shape=[o[1] for o in outs],
        compiler_params=_cparams(("arbitrary", "arbitrary")),
        name="prep_lat" if latent else "prep_ctx",
    )(*ins)


_ATT_SUB = 256


def _attn_kernel(q_ref, kc_ref, vc_ref, kl_ref, vl_ref, o_ref, s0, s1, p0, p1, acc_ref, *, tk, q_shared):
    tq = q_ref.shape[1]
    lk, lc = kl_ref.shape[1], kc_ref.shape[1]
    nblk = lk // tk
    t = _ATT_SUB
    dn = (((1,), (1,)), ((), ()))
    lanes = lambda e: slice(e * LANE, (e + 1) * LANE)

    def q_sub(e, qc):
        return q_ref[0, qc * t:(qc + 1) * t, lanes(0 if q_shared else e)]

    def vrows(v_ref, e, cols):
        return v_ref[0, :, cols] if q_shared else v_ref[0, lanes(e), cols]

    def mxu_phase(s_next, k_next, p_prev, v_prev, nk_prev, alpha):
        for qc in range(tq // t):
            cols = slice(qc * t, (qc + 1) * t)
            pv = [None, None]
            for kr in range(tk // t):
                rows = slice(kr * t, (kr + 1) * t)
                for e in range(2):
                    if s_next is not None:
                        s_next[e, rows, cols] = lax.dot_general(k_next(e, kr), q_sub(e, qc), dn,
                                                                preferred_element_type=F32)
                    if p_prev is not None and kr < nk_prev:
                        d = jnp.dot(v_prev(e, kr), p_prev(e, rows, cols), preferred_element_type=F32)
                        pv[e] = d if pv[e] is None else pv[e] + d
            if p_prev is not None:
                for e in range(2):
                    acc_ref[e, :, cols] = alpha[e][:, cols] * acc_ref[e, :, cols] + pv[e]

    def softmax_phase(s_cur, p_cur, m):
        m_new, alpha, ps = [], [], []
        for e in range(2):
            s = s_cur[e]
            mn = jnp.maximum(m[e], jnp.max(s, axis=0, keepdims=True))
            p = jnp.exp2(s - mn).astype(BF16)
            if p_cur is not None:
                p_cur[e] = p
            m_new.append(mn)
            alpha.append(jnp.exp2(m[e] - mn))
            ps.append(p)
        return m_new, alpha, ps

    def from_ref(ref):
        return lambda e, rows, cols: ref[e, rows, cols]

    def k_lat(r):
        return lambda e, kr: kl_ref[0, pl.ds(pl.multiple_of(r + kr * t, t), t), lanes(e)]

    def v_lat(r):
        return lambda e, kr: vrows(vl_ref, e, pl.ds(pl.multiple_of(r + kr * t, t), t))

    acc_ref[...] = jnp.zeros_like(acc_ref)
    ones = [jnp.ones((1, tq), F32)] * 2
    m = [jnp.full((1, tq), -1e30, F32)] * 2

    s_ctx = [lax.dot_general(kc_ref[0, :, lanes(e)], q_ref[0, :, lanes(0 if q_shared else e)], dn,
                             preferred_element_type=F32) for e in range(2)]
    mxu_phase(s0, k_lat(0), None, None, 0, None)
    m, _, p_ctx = softmax_phase(s_ctx, None, m)
    mxu_phase(s1, k_lat(tk), lambda e, rows, cols: p_ctx[e][rows, cols],
              lambda e, kr: vrows(vc_ref, e, slice(kr * t, (kr + 1) * t)), lc // t, ones)
    m, alpha, _ = softmax_phase(s0, p0, m)

    def body(j, carry):
        m, alpha = list(carry[0]), list(carry[1])
        r = pl.multiple_of(2 * j * tk, tk)
        mxu_phase(s0, k_lat(r + 2 * tk), from_ref(p0), v_lat(r), tk // t, alpha)
        m, alpha, _ = softmax_phase(s1, p1, m)
        mxu_phase(s1, k_lat(r + 3 * tk), from_ref(p1), v_lat(r + tk), tk // t, alpha)
        m, alpha, _ = softmax_phase(s0, p0, m)
        return tuple(m), tuple(alpha)

    m, alpha = lax.fori_loop(0, (nblk - 2) // 2, body, (tuple(m), tuple(alpha)))
    mxu_phase(None, None, from_ref(p0), v_lat((nblk - 2) * tk), tk // t, alpha)
    m, alpha, _ = softmax_phase(s1, p1, list(m))
    mxu_phase(None, None, from_ref(p1), v_lat((nblk - 1) * tk), tk // t, alpha)
    dv = LANE // 2
    outs = [acc_ref[e, :dv, :] / acc_ref[e, dv:dv + 1, :] for e in range(2)]
    o_ref[0] = jnp.concatenate(outs, axis=0).T.astype(o_ref.dtype)


def _attention(q, kc, vc, kl, vl, *, q_shared, kv_group, tq, tk):
    b, l, _ = q.shape
    wq = LANE if q_shared else 2 * LANE
    pairs = q.shape[-1] // wq
    lc = kc.shape[1]
    assert (l // tk) % 2 == 0, "key blocks are consumed two per loop trip"
    vw = LANE if q_shared else 2 * LANE
    kv = lambda bi, j, i: (bi, 0, j // kv_group)
    vt = lambda bi, j, i: (bi, j // kv_group, 0)
    return pl.pallas_call(
        functools.partial(_attn_kernel, tk=tk, q_shared=q_shared),
        grid=(b, pairs, l // tq),
        in_specs=[pl.BlockSpec((1, tq, wq), lambda bi, j, i: (bi, i, j)),
                  pl.BlockSpec((1, lc, 2 * LANE), kv), pl.BlockSpec((1, vw, lc), vt),
                  pl.BlockSpec((1, l, 2 * LANE), kv), pl.BlockSpec((1, vw, l), vt)],
        out_specs=pl.BlockSpec((1, tq, LANE), lambda bi, j, i: (bi, i, j)),
        out_shape=jax.ShapeDtypeStruct((b, l, pairs * LANE), BF16),
        scratch_shapes=[pltpu.VMEM((2, tk, tq), F32), pltpu.VMEM((2, tk, tq), F32),
                        pltpu.VMEM((2, tk, tq), BF16), pltpu.VMEM((2, tk, tq), BF16),
                        pltpu.VMEM((2, LANE, tq), F32)],
        compiler_params=_cparams(("arbitrary", "arbitrary", "arbitrary")),
        name="attn_gqa" if q_shared else "attn_mla",
    )(q, kc, vc, kl, vl)


def _out0_kernel(x_ref, oa_ref, om_ref, sg_ref, w_ref, ada_ref, o_ref, *, d):
    o = jnp.concatenate([oa_ref[0], om_ref[0]], axis=-1).astype(F32) * sg_ref[0].astype(F32)
    y = jnp.dot(o.astype(BF16), w_ref[...], preferred_element_type=F32)
    o_ref[0] = x_ref[0] + ada_ref[0][:, 2 * d:3 * d] * y


def _out0(x, oa, om, sg, w, ada, *, tl):
    b, l, d = x.shape
    row = lambda bi, i: (bi, i, 0)
    return pl.pallas_call(
        functools.partial(_out0_kernel, d=d),
        grid=(b, l // tl),
        in_specs=[pl.BlockSpec((1, tl, d), row), pl.BlockSpec((1, tl, oa.shape[-1]), row),
                  pl.BlockSpec((1, tl, om.shape[-1]), row), pl.BlockSpec((1, tl, d), row),
                  pl.BlockSpec(w.shape, lambda bi, i: (0, 0)),
                  pl.BlockSpec((1, 1, ada.shape[-1]), lambda bi, i: (bi, 0, 0))],
        out_specs=pl.BlockSpec((1, tl, d), row),
        out_shape=jax.ShapeDtypeStruct((b, l, d), F32),
        compiler_params=_cparams(("arbitrary", "arbitrary")),
        name="out0",
    )(x, oa, om, sg, w, ada)


_HALO = 8
_HY_STORE = BF16


def _hyin_kernel(x_ref, xp_ref, xn_ref, ada_ref, nw_ref, w_ref, cw_ref, cb_ref, *o_refs, d):
    i = pl.program_id(1)
    tl = x_ref.shape[1]
    n_conv = len(o_refs) - 1
    ada = ada_ref[0]
    shift, scale = ada[:, :d], ada[:, d:2 * d]

    def mod(x):
        return (_row_rms(x) * nw_ref[...]) * (1.0 + scale) + shift

    hp = mod(xp_ref[0]) * (i > 0).astype(F32)
    hn = mod(xn_ref[0]) * (i < pl.num_programs(1) - 1).astype(F32)
    h = jnp.concatenate([mod(x_ref[0]), hp, hn], axis=0).astype(BF16)
    rows = lax.broadcasted_iota(jnp.int32, (tl, d), 0)
    cw = cw_ref[...]
    for n in range(n_conv + 1):
        p = jnp.dot(h, w_ref[:, n * d:(n + 1) * d], preferred_element_type=F32)
        pm = p[0:tl]
        if n < n_conv:
            prev = jnp.where(rows == 0, p[tl + _HALO - 1:tl + _HALO], pltpu.roll(pm, 1, axis=0))
            nxt = jnp.where(rows == tl - 1, p[tl + _HALO:tl + _HALO + 1], pltpu.roll(pm, tl - 1, axis=0))
            c0, c1, c2 = (cw[j:j + 1, n * d:(n + 1) * d] for j in range(HY_SHORT))
            out = prev * c0 + pm * c1 + nxt * c2 + cb_ref[:, n * d:(n + 1) * d]
        else:
            out = _silu(pm)
        o_refs[n][0] = out.astype(o_refs[n].dtype)


def _hyin(x, ada, nw, w, cw, cb, *, tl):
    b, l, d = x.shape
    ng = w.shape[1] // d
    tb = tl // _HALO
    nb = l // _HALO
    const = lambda bi, i: (0, 0)
    blk = pl.BlockSpec((1, tl, d), lambda bi, i: (bi, i, 0))
    return pl.pallas_call(
        functools.partial(_hyin_kernel, d=d),
        grid=(b, l // tl),
        in_specs=[blk,
                  pl.BlockSpec((1, _HALO, d), lambda bi, i: (bi, jnp.maximum(i * tb - 1, 0), 0)),
                  pl.BlockSpec((1, _HALO, d), lambda bi, i: (bi, jnp.minimum((i + 1) * tb, nb - 1), 0)),
                  pl.BlockSpec((1, 1, ada.shape[-1]), lambda bi, i: (bi, 0, 0)),
                  pl.BlockSpec(nw.shape, const), pl.BlockSpec(w.shape, const),
                  pl.BlockSpec(cw.shape, const), pl.BlockSpec(cb.shape, const)],
        out_specs=[blk] * ng,
        out_shape=[jax.ShapeDtypeStruct((b, l, d), _HY_STORE)] * ng,
        compiler_params=_cparams(("arbitrary", "arbitrary")),
        name="hy_in",
    )(x, x, x, ada, nw, w, cw, cb)


def _filt_kernel(emb_ref, w1_ref, b1_ref, w2_ref, b2_ref, w3_ref, b3_ref, fr_ref, dl_ref, h_o, s_o, *, reps):
    emb = emb_ref[...]
    fr = fr_ref[...]
    hid = jnp.sin(fr * (jnp.dot(emb, w1_ref[...], precision=HI, preferred_element_type=F32) + b1_ref[...]))
    hid = jnp.sin(fr * (jnp.dot(hid, w2_ref[...], precision=HI, preferred_element_type=F32) + b2_ref[...]))
    h = jnp.dot(hid, w3_ref[...], precision=HI, preferred_element_type=F32) + b3_ref[...]
    win = jnp.exp(-emb[:, 0:1] * dl_ref[...])
    hw = h * jnp.concatenate([win] * reps, axis=-1)
    rows = lax.broadcasted_iota(jnp.int32, hw.shape, 0) + pl.program_id(0) * hw.shape[0]
    hw = jnp.where((rows == 0) & (_lane_iota(hw.shape) >= hw.shape[1] // 2), 0.0, hw)
    h_o[...] = hw

    @pl.when(pl.program_id(0) == 0)
    def _():
        s_o[...] = jnp.zeros_like(s_o)

    s_o[...] += jnp.sum(jnp.abs(hw), axis=0, keepdims=True)


def _filters(emb, w1, b1, w2, b2, w3, b3, fr, dl, *, tl):
    l = emb.shape[0]
    wo = w3.shape[1]
    const = lambda i: (0, 0)
    full = lambda a: pl.BlockSpec(a.shape, const)
    return pl.pallas_call(
        functools.partial(_filt_kernel, reps=wo // dl.shape[1]),
        grid=(l // tl,),
        in_specs=[pl.BlockSpec((tl, emb.shape[1]), lambda i: (i, 0)), full(w1), full(b1), full(w2), full(b2),
                  full(w3), full(b3), full(fr), full(dl)],
        out_specs=[pl.BlockSpec((tl, wo), lambda i: (i, 0)), pl.BlockSpec((1, wo), const)],
        out_shape=[jax.ShapeDtypeStruct((l, wo), F32), jax.ShapeDtypeStruct((1, wo), F32)],
        compiler_params=_cparams(("arbitrary",)),
        name="hy_filter",
    )(emb, w1, b1, w2, b2, w3, b3, fr, dl)


def _cmul_const(a, ang):
    ar, ai = a
    q = ang / (0.5 * math.pi)
    if abs(q - round(q)) < 1e-12:
        return [(ar, ai), (-ai, ar), (-ar, -ai), (ai, -ar)][int(round(q)) % 4]
    c, s = math.cos(ang), math.sin(ang)
    return (ar * c - ai * s, ar * s + ai * c)


def _fft_dif(x):
    x = list(x)
    n = len(x)
    half = n // 2
    while half >= 1:
        for base in range(0, n, 2 * half):
            for j in range(half):
                a, b = x[base + j], x[base + j + half]
                ang = -math.pi * j / half
                if b is None:
                    x[base + j + half] = None if a is None else _cmul_const(a, ang)
                else:
                    x[base + j] = (a[0] + b[0], a[1] + b[1])
                    x[base + j + half] = _cmul_const((a[0] - b[0], a[1] - b[1]), ang)
        half //= 2
    return x


def _ifft_dit(x, keep):
    x = list(x)
    n = len(x)
    half = 1
    while half <= n // 2:
        last = half == n // 2
        for base in range(0, n, 2 * half):
            for j in range(half):
                a = x[base + j]
                b = _cmul_const(x[base + j + half], math.pi * j / half)
                x[base + j] = (a[0] + b[0], a[1] + b[1])
                if not last or base + j + half < keep:
                    x[base + j + half] = (a[0] - b[0], a[1] - b[1])
        half *= 2
    return x[:keep]


def _outer_fwd_kernel(u_ref, o_ref, *, real_input):
    n1 = o_ref.shape[2]
    if real_input:
        x = [(u_ref[i], jnp.zeros_like(u_ref[i])) for i in range(n1 // 2)]
    else:
        x = [(u_ref[0, 0, i].astype(F32), u_ref[0, 1, i].astype(F32)) for i in range(n1 // 2)]
    for s, (re, im) in enumerate(_fft_dif(x + [None] * (n1 // 2))):
        o_ref[0, 0, s] = re.astype(o_ref.dtype)
        o_ref[0, 1, s] = im.astype(o_ref.dtype)


def _outer_fwd(u, *, real_input, tn2, ct, nseq=1):
    n2 = u.shape[-2]
    if real_input:
        p, c, n1 = nseq, u.shape[-1] // nseq, 2 * u.shape[0]
        in_spec = pl.BlockSpec((n1 // 2, tn2, ct), lambda pi, r, j: (0, r, pi * (c // ct) + j))
    else:
        p, c, n1 = u.shape[0], u.shape[-1], 2 * u.shape[2]
        in_spec = pl.BlockSpec((1, 2, n1 // 2, tn2, ct), lambda pi, r, j: (pi, 0, 0, r, j))
    return pl.pallas_call(
        functools.partial(_outer_fwd_kernel, real_input=real_input),
        grid=(p, n2 // tn2, c // ct), in_specs=[in_spec],
        out_specs=pl.BlockSpec((1, 2, n1, tn2, ct), lambda pi, r, j: (pi, 0, 0, r, j)),
        out_shape=jax.ShapeDtypeStruct((p, 2, n1, n2, c), u.dtype),
        compiler_params=_cparams(("arbitrary", "arbitrary", "arbitrary")),
        name="hy_outer_filt" if real_input else "hy_outer_fwd",
    )(u)


def _outer_inv_kernel(a_ref, u_ref, g_ref, sk_ref, o_ref, *next_ref):
    n1 = a_ref.shape[2]
    f = lambda ref, r, i: ref[0, r, i].astype(F32)
    y = _ifft_dit([(f(a_ref, 0, s), f(a_ref, 1, s)) for s in range(n1)], n1 // 2)
    sk = sk_ref[...]
    z = [(f(g_ref, 0, i) * (re + f(u_ref, 0, i) * sk), f(g_ref, 1, i) * (im + f(u_ref, 1, i) * sk))
         for i, (re, im) in enumerate(y)]
    for i, (re, im) in enumerate(z):
        o_ref[0, 0, i] = re.astype(o_ref.dtype)
        o_ref[0, 1, i] = im.astype(o_ref.dtype)
    if next_ref:
        nxt = next_ref[0]
        for s, (re, im) in enumerate(_fft_dif(z + [None] * (n1 // 2))):
            nxt[0, 0, s] = re.astype(nxt.dtype)
            nxt[0, 1, s] = im.astype(nxt.dtype)


def _outer_inv(a, u, gate, skip, *, tn2, ct, with_next):
    p, _, n1, n2, c = a.shape
    blk = lambda rows: pl.BlockSpec((1, 2, rows, tn2, ct), lambda pi, r, j: (pi, 0, 0, r, j))
    out_specs, out_shape = [blk(n1 // 2)], [jax.ShapeDtypeStruct(u.shape, u.dtype)]
    if with_next:
        out_specs.append(blk(n1))
        out_shape.append(jax.ShapeDtypeStruct(a.shape, a.dtype))
    return pl.pallas_call(
        _outer_inv_kernel,
        grid=(p, n2 // tn2, c // ct),
        in_specs=[blk(n1), blk(n1 // 2), blk(n1 // 2), pl.BlockSpec((1, ct), lambda pi, r, j: (0, j))],
        out_specs=out_specs, out_shape=out_shape,
        compiler_params=_cparams(("arbitrary", "arbitrary", "arbitrary")),
        name="hy_outer_inv_fwd" if with_next else "hy_outer_inv",
    )(a, u, gate, skip)


def _dot3(m3, x):
    hi = x.astype(BF16)
    lo = (x - hi.astype(F32)).astype(BF16)
    return jnp.dot(m3, jnp.concatenate([hi, hi, lo], axis=0), preferred_element_type=F32)


def _mid_kernel(a_ref, g_ref, *rest, spectrum):
    np_, n2, ct = a_ref.shape[0], a_ref.shape[3], a_ref.shape[4]
    if spectrum:
        mm = _dot3
    else:
        mm = lambda m, x: jnp.dot(m, x.astype(BF16), preferred_element_type=F32)
    xs = [mm(g_ref[0], a_ref[p, :, 0].reshape(2 * n2, ct)) for p in range(np_)]
    if spectrum:
        sc_ref, o_ref = rest
        f, bw = xs
        sc = sc_ref[...]
        o_ref[0, 0, 0] = (f[:n2] + bw[:n2]) * sc
        o_ref[0, 1, 0] = (f[n2:] - bw[n2:]) * sc
    else:
        gt_ref, kf_ref, o_ref = rest
        kr, ki = kf_ref[0, 0, 0], kf_ref[0, 1, 0]
        ys = [jnp.concatenate([x[:n2] * kr - x[n2:] * ki, x[:n2] * ki + x[n2:] * kr], axis=0) for x in xs]
        for p in range(np_):
            o_ref[p, :, 0] = mm(gt_ref[0], ys[p]).reshape(2, n2, ct).astype(o_ref.dtype)


def _mid(a, g, gt=None, kf=None, scale=None, *, ct, kf_col0=0):
    p, _, n1, n2, c = a.shape
    ablk = pl.BlockSpec((p, 2, 1, n2, ct), lambda k, j: (0, 0, k, 0, j))
    gblk = pl.BlockSpec((1,) + g.shape[1:], lambda k, j: (k, 0, 0))
    oblk, oshape = ablk, a.shape
    if kf is None:
        ins = [a, g, scale]
        in_specs = [ablk, gblk, pl.BlockSpec((1, ct), lambda k, j: (0, j))]
        oblk, oshape = pl.BlockSpec((1, 2, 1, n2, ct), lambda k, j: (0, 0, k, 0, j)), (1,) + a.shape[1:]
    else:
        ins = [a, g, gt, kf]
        in_specs = [ablk, gblk, gblk,
                    pl.BlockSpec((1, 2, 1, n2, ct), lambda k, j: (0, 0, k, 0, kf_col0 + j))]
    return pl.pallas_call(
        functools.partial(_mid_kernel, spectrum=kf is None),
        grid=(n1, c // ct), in_specs=in_specs, out_specs=oblk,
        out_shape=jax.ShapeDtypeStruct(oshape, a.dtype),
        compiler_params=_cparams(("arbitrary", "arbitrary")),
        name="hy_spectrum" if kf is None else "hy_mid",
    )(*ins)


def _dft_tables(l):
    n = 2 * l
    n2 = DFT_N2
    n1 = n // n2
    bits = n1.bit_length() - 1
    k1 = np.array([int(format(s, "0%db" % bits)[::-1], 2) for s in range(n1)], dtype=np.float64)
    kk = k1[:, None, None] + n1 * np.arange(n2, dtype=np.float64)[None, :, None]
    th = 2.0 * np.pi * kk * np.arange(n2, dtype=np.float64)[None, None, :] / n
    c, s = np.cos(th), np.sin(th)
    g = np.concatenate([np.concatenate([c, s], axis=2), np.concatenate([-s, c], axis=2)], axis=1)

    def split3(m):
        m = jnp.asarray(m, F32)
        hi = m.astype(BF16)
        lo = (m - hi.astype(F32)).astype(BF16)
        return jnp.concatenate([hi, lo, hi], axis=2)

    return split3(g), jnp.asarray(g, BF16), jnp.asarray(np.transpose(g, (0, 2, 1)), BF16)


def _pair_view(u, n1h):
    b, l, c = u.shape
    return u.reshape(b // 2, 2, n1h, l // n1h, c)


def _long_convs(u, gates, skips, kf, tabs, *, ct):
    _, g, gt = tabs
    c = u.shape[-1]
    a = _outer_fwd(u, real_input=False, tn2=32, ct=256)
    z = u
    for o, (gate, skip) in enumerate(zip(gates, skips)):
        a = _mid(a, g, gt, kf, ct=ct, kf_col0=o * (c // ct))
        last = o == len(gates) - 1
        res = _outer_inv(a, z, gate, skip, tn2=32, ct=256, with_next=not last)
        z, a = (res[0], None) if last else res
    return z


def _out1_kernel(x_ref, z_ref, sg_ref, w_ref, ada_ref, fw_ref, o_ref, *, d):
    y = jnp.dot((z_ref[0].astype(F32) * sg_ref[0].astype(F32)).astype(BF16), w_ref[...],
                preferred_element_type=F32)
    x = x_ref[0] + ada_ref[0][:, 2 * d:3 * d] * y
    o_ref[0] = _row_rms(x) * fw_ref[...]


def _out1(x, z, sg, w, ada, fw, *, tl):
    b, l, d = x.shape
    row = lambda bi, i: (bi, i, 0)
    blk = pl.BlockSpec((1, tl, d), row)
    return pl.pallas_call(
        functools.partial(_out1_kernel, d=d),
        grid=(b, l // tl),
        in_specs=[blk, blk, blk, pl.BlockSpec(w.shape, lambda bi, i: (0, 0)),
                  pl.BlockSpec((1, 1, ada.shape[-1]), lambda bi, i: (bi, 0, 0)),
                  pl.BlockSpec(fw.shape, lambda bi, i: (0, 0))],
        out_specs=blk,
        out_shape=jax.ShapeDtypeStruct((b, l, d), F32),
        compiler_params=_cparams(("arbitrary", "arbitrary")),
        name="out1",
    )(x, z, sg, w, ada, fw)


def _swap_cols(w, q):
    return w[..., np.arange(w.shape[-1]) ^ q]


def _pack_attn_w_in(w):
    d = w.shape[0]
    o = 0
    wq = w[:, o:o + 512]; o += 512
    wk = w[:, o:o + 128]; o += 128
    wv = w[:, o:o + 128]; o += 128
    wcq = w[:, o:o + MLA_Q_RANK]; o += MLA_Q_RANK
    wckv = w[:, o:o + MLA_KV_RANK]; o += MLA_KV_RANK
    wkpe = w[:, o:o + MLA_ROPE_DIM]; o += MLA_ROPE_DIM
    wg = w[:, o:]
    qa, qm = GQA_HEAD_DIM // 4, MLA_ROPE_DIM // 4

    def pe_chunk(wp):
        return jnp.concatenate([jnp.zeros((d, 64), w.dtype), wp, jnp.zeros((d, 32), w.dtype)], axis=1)

    kpe, kpe_sw = pe_chunk(wkpe), pe_chunk(_swap_cols(wkpe, qm))
    kv_part = jnp.concatenate([wk, wv, wckv, kpe], axis=1)
    lat = jnp.concatenate([wq, _swap_cols(wq, qa), wk, _swap_cols(wk, qa), wv, wcq, wckv, kpe, kpe_sw, wg], axis=1)
    return lat.astype(BF16), kv_part.astype(BF16)


def _pack_mla_up(w_uq, w_ukv):
    dq = MLA_NOPE_DIM + MLA_ROPE_DIM
    r = w_uq.shape[0]
    z = lambda n: jnp.zeros((r, n), w_uq.dtype)
    uq, uq_sw = [], []
    for h in range(MLA_HEADS):
        nope, pe = w_uq[:, dq * h:dq * h + MLA_NOPE_DIM], w_uq[:, dq * h + MLA_NOPE_DIM:dq * (h + 1)]
        uq += [nope, pe, z(LANE - dq)]
        uq_sw += [z(MLA_NOPE_DIM), _swap_cols(pe, MLA_ROPE_DIM // 4), z(LANE - dq)]
    dkv = MLA_NOPE_DIM + MLA_V_DIM
    kn = jnp.concatenate(
        [jnp.concatenate([w_ukv[:, dkv * h:dkv * h + MLA_NOPE_DIM],
                          jnp.zeros((w_ukv.shape[0], LANE - MLA_NOPE_DIM), w_ukv.dtype)], axis=1)
         for h in range(MLA_HEADS)], axis=1)
    vm = jnp.concatenate(
        [jnp.concatenate([w_ukv[:, dkv * h + MLA_NOPE_DIM:dkv * (h + 1)],
                          jnp.zeros((w_ukv.shape[0], LANE - MLA_V_DIM), w_ukv.dtype)], axis=1)
         for h in range(MLA_HEADS)], axis=1)
    return jnp.concatenate(uq + uq_sw, axis=1).astype(BF16), jnp.concatenate([kn, vm], axis=1).astype(BF16)


def _head_ones2(width):
    i = np.arange(width) // GQA_HEAD_DIM
    blk = (i[:, None] == i[None, :]).astype(np.float32)
    return jnp.asarray(np.concatenate([blk, blk], axis=0), BF16)


def _rope_tables(l):
    rows = (jnp.arange(l, dtype=jnp.int32) // GRID_W).astype(F32)[:, None]
    cols = (jnp.arange(l, dtype=jnp.int32) % GRID_W).astype(F32)[:, None]

    def tab(rot_dim):
        q = rot_dim // 4
        inv = ROPE_BASE ** (-jnp.arange(q, dtype=F32) / q)
        ar, ac = rows * inv, cols * inv
        cos = jnp.concatenate([jnp.cos(ar)] * 2 + [jnp.cos(ac)] * 2, axis=1)
        sin = jnp.concatenate([-jnp.sin(ar), jnp.sin(ar), -jnp.sin(ac), jnp.sin(ac)], axis=1)
        return cos, sin

    ca, sa = tab(GQA_HEAD_DIM)
    ca = jnp.concatenate([ca] * (LANE // GQA_HEAD_DIM), axis=1)
    sa = jnp.concatenate([sa] * (LANE // GQA_HEAD_DIM), axis=1)
    cm, sm = tab(MLA_ROPE_DIM)
    one, zero = jnp.ones((l, 1), F32), jnp.zeros((l, 1), F32)
    cm = jnp.concatenate([jnp.tile(one, (1, 64)), cm, jnp.tile(one, (1, 32))], axis=1)
    sm = jnp.concatenate([jnp.tile(zero, (1, 64)), sm, jnp.tile(zero, (1, 32))], axis=1)
    return ca, sa, cm, sm


def _pad2(a, r, c):
    return jnp.pad(a, ((0, r - a.shape[0]), (0, c - a.shape[1])))


def kernel(x, c, ctx, c_ctx, ada_w, ada_b, norm_w, attn_w_in, attn_q_norm, attn_k_norm, mla_q_norm, mla_kv_norm, mla_w_uq, mla_w_ukv, attn_w_out, hy_w_in, hy_conv_w, hy_conv_b, hy_ffn_w1, hy_ffn_b1, hy_ffn_w2, hy_ffn_b2, hy_ffn_w3, hy_ffn_b3, hy_freq, hy_skip, hy_w_out, final_norm_w):
    b, l, d = x.shape
    lc = ctx.shape[1]
    tl = min(256, l)

    rows = -(-(b + 1) // 8) * 8
    cs = jnp.concatenate([c, c_ctx[None, :], jnp.zeros((rows - b - 1, d), F32)], axis=0)
    ada = _ada(cs, ada_w, ada_b)
    ada_lat = [ada[i, :b].reshape(b, 1, 3 * d) for i in range(ada.shape[0])]
    ada_ctx0 = jnp.broadcast_to(ada[0, b].reshape(1, 1, 3 * d), (b, 1, 3 * d))

    w_lat, w_kv = _pack_attn_w_in(attn_w_in[0])
    wuq, wukv = _pack_mla_up(mla_w_uq[0], mla_w_ukv[0])
    nw0 = norm_w[0].reshape(1, d)
    def norm_rows(wn, width):
        sw = _swap_cols(wn, GQA_HEAD_DIM // 4)
        return jnp.stack([jnp.tile(wn, width // GQA_HEAD_DIM), jnp.tile(sw, width // GQA_HEAD_DIM)])

    qn, kn = norm_rows(attn_q_norm[0], _QA_W), norm_rows(attn_k_norm[0], LANE)
    cqn = mla_q_norm[0].reshape(1, MLA_Q_RANK)
    ckvn = mla_kv_norm[0].reshape(1, MLA_KV_RANK)
    o2q, o2k = _head_ones2(_QA_W), _head_ones2(LANE)
    tabs = _rope_tables(l)
    qa, ka, va, qm, km, vm, sg = _prep(x, ada_lat[0], nw0, w_lat, qn, kn, cqn, ckvn, wuq, wukv, o2q, o2k, tabs,
                                       latent=True, tl=tl)
    kac, vac, kmc, vmc = _prep(ctx, ada_ctx0, nw0, w_kv, None, kn, None, ckvn, None, wukv, None, o2k, None,
                               latent=False, tl=min(tl, lc))
    tq, tk = min(2048, l), min(512, l // 2)
    oa = _attention(qa, kac, vac, ka, va, q_shared=True, kv_group=2, tq=tq, tk=tk)
    om = _attention(qm, kmc, vmc, km, vm, q_shared=False, kv_group=1, tq=tq, tk=tk)
    x1 = _out0(x, oa, om, sg, attn_w_out[0].astype(BF16), ada_lat[0], tl=tl)

    nw1 = norm_w[1].reshape(1, d)
    u = _hyin(x1, ada_lat[1], nw1, hy_w_in[0].astype(BF16), hy_conv_w[0], hy_conv_b[0].reshape(1, -1),
              tl=min(512, l))
    n1 = 2 * l // DFT_N2
    n1h = n1 // 2
    tabs_d = _dft_tables(l)

    t = jnp.linspace(0.0, 1.0, l, dtype=F32)[:, None]
    wpos = (2.0 * math.pi / l) * jnp.arange(l, dtype=F32)[:, None]
    bands = jnp.linspace(1e-4, HY_BANDS - 1, HY_BANDS, dtype=F32)
    emb = jnp.concatenate([t, jnp.cos(wpos * bands), -jnp.sin(wpos * bands)], axis=-1)
    deltas = jnp.abs(jnp.linspace(math.log(HY_DECAY_TARGET) / HY_SLOW_DECAY,
                                  math.log(HY_DECAY_TARGET) / HY_FAST_DECAY, d, dtype=F32)).reshape(1, d)
    wf = hy_ffn_w3.shape[-1]
    oc = HY_ORDER * d

    def by_direction(a):
        return a.reshape(-1, HY_ORDER, 2, d).transpose(0, 2, 1, 3).reshape(-1, wf)

    hw, asum = _filters(_pad2(emb, l, LANE), _pad2(hy_ffn_w1[0], LANE, LANE), _pad2(hy_ffn_b1[0][None], 1, LANE),
                        _pad2(hy_ffn_w2[0], LANE, LANE), _pad2(hy_ffn_b2[0][None], 1, LANE),
                        _pad2(by_direction(hy_ffn_w3[0]), LANE, wf), by_direction(hy_ffn_b3[0][None]),
                        _pad2(hy_freq[0][None], 1, LANE), deltas, tl=tl)
    l1 = asum[:, :oc] + asum[:, oc:]
    ct = min(1024, d)
    af = _outer_fwd(hw.reshape(n1h, DFT_N2, wf), real_input=True, nseq=2, tn2=32, ct=256)
    kf = _mid(af, tabs_d[0], scale=1.0 / (l1 * (2 * l)), ct=oc)

    v2, x1g, x2g = (_pair_view(u[i], n1h) for i in range(3))
    z = _long_convs(v2, [x1g, x2g], [hy_skip[0, o:o + 1] for o in range(HY_ORDER)], kf, tabs_d, ct=ct)
    z = z.reshape(b, l, d)
    return _out1(x1, z, u[3], hy_w_out[0].astype(BF16), ada_lat[1], final_norm_w.reshape(1, d), tl=tl)
```

```python
import functools
import math

import numpy as np
import jax
import jax.numpy as jnp
from jax import lax
from jax.experimental import pallas as pl
from jax.experimental.pallas import tpu as pltpu

EPS = 1e-6
GRID_W = 64
ROPE_BASE = 10000.0
GQA_HEADS, GQA_KV_HEADS, GQA_HEAD_DIM = 8, 2, 64
MLA_HEADS, MLA_Q_RANK, MLA_KV_RANK = 8, 256, 128
MLA_NOPE_DIM, MLA_ROPE_DIM, MLA_V_DIM = 64, 32, 64
HY_ORDER, HY_SHORT, HY_BANDS, HY_FFN = 2, 3, 16, 64
HY_FAST_DECAY, HY_SLOW_DECAY, HY_DECAY_TARGET = 0.3, 1.5, 1e-2
LANE = 128
DFT_N2 = 128
VMEM_LIMIT = 56 * 1024 * 1024
LOG2E = 1.4426950408889634
HI = lax.Precision.HIGHEST
F32 = jnp.float32
BF16 = jnp.bfloat16


def _cparams(sem):
    return pltpu.CompilerParams(dimension_semantics=sem, vmem_limit_bytes=VMEM_LIMIT)


def _per_chunk(fn, *arrs):
    width = arrs[0].shape[-1]
    outs = [fn(*[a[:, c:c + LANE] for a in arrs]) for c in range(0, width, LANE)]
    return outs[0] if len(outs) == 1 else jnp.concatenate(outs, axis=-1)


def _lane_iota(shape):
    return lax.broadcasted_iota(jnp.int32, shape, len(shape) - 1)


def _head_rsqrt(x, ones2):
    ss = x * x
    hi = ss.astype(BF16)
    lo = (ss - hi.astype(F32)).astype(BF16)
    tot = jnp.dot(jnp.concatenate([hi, lo], axis=-1), ones2, preferred_element_type=F32)
    return lax.rsqrt(tot * (1.0 / GQA_HEAD_DIM) + EPS)


def _rope(x, x_sw, cos, sin_signed):
    return _per_chunk(lambda c, w: c * cos + w * sin_signed, x, x_sw)


def _row_rms(x):
    return x * lax.rsqrt(jnp.mean(x * x, axis=-1, keepdims=True) + EPS)


def _silu(x):
    return x * (1.0 / (1.0 + jnp.exp(-x)))


def _ada_kernel(c_ref, w_ref, b_ref, o_ref):
    s = _silu(c_ref[...])
    o_ref[0] = jnp.dot(s, w_ref[0], precision=HI, preferred_element_type=F32) + b_ref[0]


def _ada(cs, ada_w, ada_b):
    depth, d, d3 = ada_w.shape
    rows = cs.shape[0]
    nt = d3 // d
    return pl.pallas_call(
        _ada_kernel,
        grid=(depth, nt),
        in_specs=[pl.BlockSpec((rows, d), lambda i, j: (0, 0)),
                  pl.BlockSpec((1, d, d), lambda i, j: (i, 0, j)),
                  pl.BlockSpec((1, 1, d), lambda i, j: (i, 0, j))],
        out_specs=pl.BlockSpec((1, rows, d), lambda i, j: (i, 0, j)),
        out_shape=jax.ShapeDtypeStruct((depth, rows, d3), F32),
        compiler_params=_cparams(("arbitrary", "arbitrary")),
        name="ada",
    )(cs, ada_w, ada_b.reshape(depth, 1, d3))


_QA_W, _KA_W, _VA_W, _KPE_W = 512, 512, 256, 128


def _prep_kernel(*refs, latent, d):
    if latent:
        (x_ref, ada_ref, nw_ref, w_ref, qn_ref, kn_ref, cqn_ref, ckvn_ref, wuq_ref, wukv_ref, o2q_ref, o2k_ref,
         ca_ref, sa_ref, cm_ref, sm_ref,
         qa_o, ka_o, va_o, qm_o, km_o, vm_o, sg_o) = refs
    else:
        (x_ref, ada_ref, nw_ref, w_ref, kn_ref, ckvn_ref, wukv_ref, o2k_ref,
         ka_o, va_o, km_o, vm_o) = refs
    ada = ada_ref[0]
    shift, scale = ada[:, :d], ada[:, d:2 * d]
    h = (_row_rms(x_ref[0]) * nw_ref[...]) * (1.0 + scale) + shift
    p = jnp.dot(h.astype(BF16), w_ref[...], preferred_element_type=F32)
    off = 0

    def take(width):
        nonlocal off
        off += width
        return p[:, off - width:off]

    if latent:
        qa, qa_sw = take(_QA_W), take(_QA_W)
        k, k_sw = take(LANE), take(LANE)
    else:
        k = take(LANE)
    v = take(LANE)
    if latent:
        cq = take(MLA_Q_RANK)
    ckv = take(MLA_KV_RANK)
    kpe = take(LANE)

    kn = kn_ref[...]
    rk = _head_rsqrt(k, o2k_ref[...])
    k = k * rk * kn[0:1]
    if latent:
        k = _rope(k, k_sw * rk * kn[1:2], ca_ref[...], sa_ref[...])
        kpe = _rope(kpe, take(LANE), cm_ref[...], sm_ref[...])
    low = _lane_iota(k.shape) < (LANE // 2)
    k_x = pltpu.roll(k, LANE // 2, axis=1)
    v_x = pltpu.roll(v, LANE // 2, axis=1)
    zero = jnp.zeros_like(k)
    ka = [jnp.where(low, k, zero), jnp.where(low, zero, k_x), jnp.where(low, k_x, zero), jnp.where(low, zero, k)]
    ka_o[0] = jnp.concatenate(ka, axis=-1).astype(BF16)
    one64 = (_lane_iota(k.shape) == LANE // 2).astype(F32)
    va = jnp.concatenate([jnp.where(low, v, one64), jnp.where(low, v_x, one64)], axis=-1)
    va_o[0] = va.T.astype(BF16)
    ckv_n = (_row_rms(ckv) * ckvn_ref[...]).astype(BF16)
    kv = jnp.dot(ckv_n, wukv_ref[...], preferred_element_type=F32)
    nk = MLA_HEADS * LANE
    km_o[0] = _per_chunk(lambda c: c + kpe, kv[:, :nk]).astype(BF16)
    vm_o[0] = _per_chunk(lambda c: c + one64, kv[:, nk:]).T.astype(BF16)
    if latent:
        qn = qn_ref[...]
        rq = _head_rsqrt(qa, o2q_ref[...])
        qa = _rope(qa * rq * qn[0:1], qa_sw * rq * qn[1:2], ca_ref[...], sa_ref[...])
        qa_o[0] = (qa * (GQA_HEAD_DIM ** -0.5 * LOG2E)).astype(BF16)
        cq_n = (_row_rms(cq) * cqn_ref[...]).astype(BF16)
        qm = jnp.dot(cq_n, wuq_ref[...], preferred_element_type=F32)
        qm = _rope(qm[:, :nk], qm[:, nk:], cm_ref[...], sm_ref[...])
        qm_o[0] = (qm * ((MLA_NOPE_DIM + MLA_ROPE_DIM) ** -0.5 * LOG2E)).astype(BF16)
        sg_o[0] = _silu(take(d)).astype(BF16)


def _prep(x, ada, nw, w, qn, kn, cqn, ckvn, wuq, wukv, o2q, o2k, tabs, *, latent, tl):
    b, l, d = x.shape
    grid = (l // tl, b)
    row = lambda i, j: (j, i, 0)
    const = lambda i, j: (0, 0)
    tab = lambda i, j: (i, 0)
    xspec = pl.BlockSpec((1, tl, d), row)
    adaspec = pl.BlockSpec((1, 1, ada.shape[-1]), lambda i, j: (j, 0, 0))

    def full(a):
        return pl.BlockSpec(a.shape, const)

    def out(width):
        return (pl.BlockSpec((1, tl, width), row), jax.ShapeDtypeStruct((b, l, width), BF16))

    def out_t(width):
        return (pl.BlockSpec((1, width, tl), lambda i, j: (j, 0, i)), jax.ShapeDtypeStruct((b, width, l), BF16))

    if latent:
        ca, sa, cm, sm = tabs
        ins = [x, ada, nw, w, qn, kn, cqn, ckvn, wuq, wukv, o2q, o2k, ca, sa, cm, sm]
        in_specs = [xspec, adaspec, full(nw), full(w), full(qn), full(kn), full(cqn), full(ckvn),
                    full(wuq), full(wukv), full(o2q), full(o2k)] + [pl.BlockSpec((tl, t.shape[1]), tab) for t in tabs]
        outs = [out(_QA_W), out(_KA_W), out_t(_VA_W), out(MLA_HEADS * LANE), out(MLA_HEADS * LANE),
                out_t(MLA_HEADS * LANE), out(d)]
    else:
        ins = [x, ada, nw, w, kn, ckvn, wukv, o2k]
        in_specs = [xspec, adaspec, full(nw), full(w), full(kn), full(ckvn), full(wukv), full(o2k)]
        outs = [out(_KA_W), out_t(_VA_W), out(MLA_HEADS * LANE), out_t(MLA_HEADS * LANE)]
    return pl.pallas_call(
        functools.partial(_prep_kernel, latent=latent, d=d),
        grid=grid, in_specs=in_specs,
        out_specs=[o[0] for o in outs], out_shape=[o[1] for o in outs],
        compiler_params=_cparams(("arbitrary", "arbitrary")),
        name="prep_lat" if latent else "prep_ctx",
    )(*ins)


_ATT_SUB = 256


def _attn_kernel(q_ref, kc_ref, vc_ref, kl_ref, vl_ref, o_ref, s0, s1, p0, p1, acc_ref, *, tk, q_shared):
    tq = q_ref.shape[1]
    lk, lc = kl_ref.shape[1], kc_ref.shape[1]
    nblk = lk // tk
    t = _ATT_SUB
    dn = (((1,), (1,)), ((), ()))
    lanes = lambda e: slice(e * LANE, (e + 1) * LANE)

    def q_sub(e, qc):
        return q_ref[0, qc * t:(qc + 1) * t, lanes(0 if q_shared else e)]

    def vrows(v_ref, e, cols):
        return v_ref[0, :, cols] if q_shared else v_ref[0, lanes(e), cols]

    def mxu_phase(s_next, k_next, p_prev, v_prev, nk_prev, alpha):
        smax = [[], []]
        for qc in range(tq // t):
            cols = slice(qc * t, (qc + 1) * t)
            pv, cmax = [None, None], [None, None]
            for kr in range(tk // t):
                rows = slice(kr * t, (kr + 1) * t)
                for e in range(2):
                    if s_next is not None:
                        s_tile = lax.dot_general(k_next(e, kr), q_sub(e, qc), dn, preferred_element_type=F32)
                        s_next[e, rows, cols] = s_tile
                        tmax = jnp.max(s_tile, axis=0, keepdims=True)
                        cmax[e] = tmax if cmax[e] is None else jnp.maximum(cmax[e], tmax)
                    if p_prev is not None and kr < nk_prev:
                        d = jnp.dot(v_prev(e, kr), p_prev(e, rows, cols), preferred_element_type=F32)
                        pv[e] = d if pv[e] is None else pv[e] + d
            for e in range(2):
                if p_prev is not None:
                    acc_ref[e, :, cols] = alpha[e][:, cols] * acc_ref[e, :, cols] + pv[e]
                smax[e].append(cmax[e])
        return None if s_next is None else [jnp.concatenate(c, axis=-1) for c in smax]

    def softmax_phase(s_cur, smax, p_cur, m):
        m_new, alpha, ps = [], [], []
        for e in range(2):
            s = s_cur[e]
            mn = jnp.maximum(m[e], smax[e])
            p = jnp.exp2(s - mn).astype(BF16)
            if p_cur is not None:
                p_cur[e] = p
            m_new.append(mn)
            alpha.append(jnp.exp2(m[e] - mn))
            ps.append(p)
        return m_new, alpha, ps

    def from_ref(ref):
        return lambda e, rows, cols: ref[e, rows, cols]

    def k_lat(r):
        return lambda e, kr: kl_ref[0, pl.ds(pl.multiple_of(r + kr * t, t), t), lanes(e)]

    def v_lat(r):
        return lambda e, kr: vrows(vl_ref, e, pl.ds(pl.multiple_of(r + kr * t, t), t))

    acc_ref[...] = jnp.zeros_like(acc_ref)
    ones = [jnp.ones((1, tq), F32)] * 2
    m = [jnp.full((1, tq), -1e30, F32)] * 2

    s_ctx = [lax.dot_general(kc_ref[0, :, lanes(e)], q_ref[0, :, lanes(0 if q_shared else e)], dn,
                             preferred_element_type=F32) for e in range(2)]
    x0 = mxu_phase(s0, k_lat(0), None, None, 0, None)
    m, _, p_ctx = softmax_phase(s_ctx, [jnp.max(s, axis=0, keepdims=True) for s in s_ctx], None, m)
    x1 = mxu_phase(s1, k_lat(tk), lambda e, rows, cols: p_ctx[e][rows, cols],
                   lambda e, kr: vrows(vc_ref, e, slice(kr * t, (kr + 1) * t)), lc // t, ones)
    m, alpha, _ = softmax_phase(s0, x0, p0, m)

    def body(j, carry):
        m, alpha, x1 = (list(c) for c in carry)
        r = pl.multiple_of(2 * j * tk, tk)
        x0 = mxu_phase(s0, k_lat(r + 2 * tk), from_ref(p0), v_lat(r), tk // t, alpha)
        m, alpha, _ = softmax_phase(s1, x1, p1, m)
        x1 = mxu_phase(s1, k_lat(r + 3 * tk), from_ref(p1), v_lat(r + tk), tk // t, alpha)
        m, alpha, _ = softmax_phase(s0, x0, p0, m)
        return tuple(m), tuple(alpha), tuple(x1)

    m, alpha, x1 = lax.fori_loop(0, (nblk - 2) // 2, body, (tuple(m), tuple(alpha), tuple(x1)))
    mxu_phase(None, None, from_ref(p0), v_lat((nblk - 2) * tk), tk // t, alpha)
    m, alpha, _ = softmax_phase(s1, list(x1), p1, list(m))
    mxu_phase(None, None, from_ref(p1), v_lat((nblk - 1) * tk), tk // t, alpha)
    dv = LANE // 2
    outs = [acc_ref[e, :dv, :] / acc_ref[e, dv:dv + 1, :] for e in range(2)]
    o_ref[0] = jnp.concatenate(outs, axis=0).T.astype(o_ref.dtype)


def _attention(q, kc, vc, kl, vl, *, q_shared, kv_group, tq, tk):
    b, l, _ = q.shape
    wq = LANE if q_shared else 2 * LANE
    pairs = q.shape[-1] // wq
    lc = kc.shape[1]
    assert (l // tk) % 2 == 0, "key blocks are consumed two per loop trip"
    vw = LANE if q_shared else 2 * LANE
    kv = lambda bi, j, i: (bi, 0, j // kv_group)
    vt = lambda bi, j, i: (bi, j // kv_group, 0)
    return pl.pallas_call(
        functools.partial(_attn_kernel, tk=tk, q_shared=q_shared),
        grid=(b, pairs, l // tq),
        in_specs=[pl.BlockSpec((1, tq, wq), lambda bi, j, i: (bi, i, j)),
                  pl.BlockSpec((1, lc, 2 * LANE), kv), pl.BlockSpec((1, vw, lc), vt),
                  pl.BlockSpec((1, l, 2 * LANE), kv), pl.BlockSpec((1, vw, l), vt)],
        out_specs=pl.BlockSpec((1, tq, LANE), lambda bi, j, i: (bi, i, j)),
        out_shape=jax.ShapeDtypeStruct((b, l, pairs * LANE), BF16),
        scratch_shapes=[pltpu.VMEM((2, tk, tq), F32), pltpu.VMEM((2, tk, tq), F32),
                        pltpu.VMEM((2, tk, tq), BF16), pltpu.VMEM((2, tk, tq), BF16),
                        pltpu.VMEM((2, LANE, tq), F32)],
        compiler_params=_cparams(("arbitrary", "arbitrary", "arbitrary")),
        name="attn_gqa" if q_shared else "attn_mla",
    )(q, kc, vc, kl, vl)


def _out0_kernel(x_ref, oa_ref, om_ref, sg_ref, w_ref, ada_ref, o_ref, *, d):
    o = jnp.concatenate([oa_ref[0], om_ref[0]], axis=-1).astype(F32) * sg_ref[0].astype(F32)
    y = jnp.dot(o.astype(BF16), w_ref[...], preferred_element_type=F32)
    o_ref[0] = x_ref[0] + ada_ref[0][:, 2 * d:3 * d] * y


def _out0(x, oa, om, sg, w, ada, *, tl):
    b, l, d = x.shape
    row = lambda bi, i: (bi, i, 0)
    return pl.pallas_call(
        functools.partial(_out0_kernel, d=d),
        grid=(b, l // tl),
        in_specs=[pl.BlockSpec((1, tl, d), row), pl.BlockSpec((1, tl, oa.shape[-1]), row),
                  pl.BlockSpec((1, tl, om.shape[-1]), row), pl.BlockSpec((1, tl, d), row),
                  pl.BlockSpec(w.shape, lambda bi, i: (0, 0)),
                  pl.BlockSpec((1, 1, ada.shape[-1]), lambda bi, i: (bi, 0, 0))],
        out_specs=pl.BlockSpec((1, tl, d), row),
        out_shape=jax.ShapeDtypeStruct((b, l, d), F32),
        compiler_params=_cparams(("arbitrary", "arbitrary")),
        name="out0",
    )(x, oa, om, sg, w, ada)


_HALO = 8
_HY_STORE = BF16


def _hyin_kernel(x_ref, xp_ref, xn_ref, ada_ref, nw_ref, w_ref, cw_ref, cb_ref, *o_refs, d):
    i = pl.program_id(1)
    tl = x_ref.shape[1]
    n_conv = len(o_refs) - 1
    ada = ada_ref[0]
    shift, scale = ada[:, :d], ada[:, d:2 * d]

    def mod(x):
        return (_row_rms(x) * nw_ref[...]) * (1.0 + scale) + shift

    hp = mod(xp_ref[0]) * (i > 0).astype(F32)
    hn = mod(xn_ref[0]) * (i < pl.num_programs(1) - 1).astype(F32)
    h = jnp.concatenate([mod(x_ref[0]), hp, hn], axis=0).astype(BF16)
    rows = lax.broadcasted_iota(jnp.int32, (tl, d), 0)
    cw = cw_ref[...]
    for n in range(n_conv + 1):
        p = jnp.dot(h, w_ref[:, n * d:(n + 1) * d], preferred_element_type=F32)
        pm = p[0:tl]
        if n < n_conv:
            prev = jnp.where(rows == 0, p[tl + _HALO - 1:tl + _HALO], pltpu.roll(pm, 1, axis=0))
            nxt = jnp.where(rows == tl - 1, p[tl + _HALO:tl + _HALO + 1], pltpu.roll(pm, tl - 1, axis=0))
            c0, c1, c2 = (cw[j:j + 1, n * d:(n + 1) * d] for j in range(HY_SHORT))
            out = prev * c0 + pm * c1 + nxt * c2 + cb_ref[:, n * d:(n + 1) * d]
        else:
            out = _silu(pm)
        o_refs[n][0] = out.astype(o_refs[n].dtype)


def _hyin(x, ada, nw, w, cw, cb, *, tl):
    b, l, d = x.shape
    ng = w.shape[1] // d
    tb = tl // _HALO
    nb = l // _HALO
    const = lambda bi, i: (0, 0)
    blk = pl.BlockSpec((1, tl, d), lambda bi, i: (bi, i, 0))
    return pl.pallas_call(
        functools.partial(_hyin_kernel, d=d),
        grid=(b, l // tl),
        in_specs=[blk,
                  pl.BlockSpec((1, _HALO, d), lambda bi, i: (bi, jnp.maximum(i * tb - 1, 0), 0)),
                  pl.BlockSpec((1, _HALO, d), lambda bi, i: (bi, jnp.minimum((i + 1) * tb, nb - 1), 0)),
                  pl.BlockSpec((1, 1, ada.shape[-1]), lambda bi, i: (bi, 0, 0)),
                  pl.BlockSpec(nw.shape, const), pl.BlockSpec(w.shape, const),
                  pl.BlockSpec(cw.shape, const), pl.BlockSpec(cb.shape, const)],
        out_specs=[blk] * ng,
        out_shape=[jax.ShapeDtypeStruct((b, l, d), _HY_STORE)] * ng,
        compiler_params=_cparams(("arbitrary", "arbitrary")),
        name="hy_in",
    )(x, x, x, ada, nw, w, cw, cb)


def _filt_kernel(emb_ref, w1_ref, b1_ref, w2_ref, b2_ref, w3_ref, b3_ref, fr_ref, dl_ref, h_o, s_o, *, reps):
    emb = emb_ref[...]
    fr = fr_ref[...]
    hid = jnp.sin(fr * (jnp.dot(emb, w1_ref[...], precision=HI, preferred_element_type=F32) + b1_ref[...]))
    hid = jnp.sin(fr * (jnp.dot(hid, w2_ref[...], precision=HI, preferred_element_type=F32) + b2_ref[...]))
    h = jnp.dot(hid, w3_ref[...], precision=HI, preferred_element_type=F32) + b3_ref[...]
    win = jnp.exp(-emb[:, 0:1] * dl_ref[...])
    hw = h * jnp.concatenate([win] * reps, axis=-1)
    rows = lax.broadcasted_iota(jnp.int32, hw.shape, 0) + pl.program_id(0) * hw.shape[0]
    hw = jnp.where((rows == 0) & (_lane_iota(hw.shape) >= hw.shape[1] // 2), 0.0, hw)
    h_o[...] = hw

    @pl.when(pl.program_id(0) == 0)
    def _():
        s_o[...] = jnp.zeros_like(s_o)

    s_o[...] += jnp.sum(jnp.abs(hw), axis=0, keepdims=True)


def _filters(emb, w1, b1, w2, b2, w3, b3, fr, dl, *, tl):
    l = emb.shape[0]
    wo = w3.shape[1]
    const = lambda i: (0, 0)
    full = lambda a: pl.BlockSpec(a.shape, const)
    return pl.pallas_call(
        functools.partial(_filt_kernel, reps=wo // dl.shape[1]),
        grid=(l // tl,),
        in_specs=[pl.BlockSpec((tl, emb.shape[1]), lambda i: (i, 0)), full(w1), full(b1), full(w2), full(b2),
                  full(w3), full(b3), full(fr), full(dl)],
        out_specs=[pl.BlockSpec((tl, wo), lambda i: (i, 0)), pl.BlockSpec((1, wo), const)],
        out_shape=[jax.ShapeDtypeStruct((l, wo), F32), jax.ShapeDtypeStruct((1, wo), F32)],
        compiler_params=_cparams(("arbitrary",)),
        name="hy_filter",
    )(emb, w1, b1, w2, b2, w3, b3, fr, dl)


def _cmul_const(a, ang):
    ar, ai = a
    q = ang / (0.5 * math.pi)
    if abs(q - round(q)) < 1e-12:
        return [(ar, ai), (-ai, ar), (-ar, -ai), (ai, -ar)][int(round(q)) % 4]
    c, s = math.cos(ang), math.sin(ang)
    return (ar * c - ai * s, ar * s + ai * c)


def _fft_dif(x):
    x = list(x)
    n = len(x)
    half = n // 2
    while half >= 1:
        for base in range(0, n, 2 * half):
            for j in range(half):
                a, b = x[base + j], x[base + j + half]
                ang = -math.pi * j / half
                if b is None:
                    x[base + j + half] = None if a is None else _cmul_const(a, ang)
                else:
                    x[base + j] = (a[0] + b[0], a[1] + b[1])
                    x[base + j + half] = _cmul_const((a[0] - b[0], a[1] - b[1]), ang)
        half //= 2
    return x


def _ifft_dit(x, keep):
    x = list(x)
    n = len(x)
    half = 1
    while half <= n // 2:
        last = half == n // 2
        for base in range(0, n, 2 * half):
            for j in range(half):
                a = x[base + j]
                b = _cmul_const(x[base + j + half], math.pi * j / half)
                x[base + j] = (a[0] + b[0], a[1] + b[1])
                if not last or base + j + half < keep:
                    x[base + j + half] = (a[0] - b[0], a[1] - b[1])
        half *= 2
    return x[:keep]


def _outer_fwd_kernel(u_ref, o_ref, *, real_input):
    n1 = o_ref.shape[2]
    if real_input:
        x = [(u_ref[i], jnp.zeros_like(u_ref[i])) for i in range(n1 // 2)]
    else:
        x = [(u_ref[0, 0, i].astype(F32), u_ref[0, 1, i].astype(F32)) for i in range(n1 // 2)]
    for s, (re, im) in enumerate(_fft_dif(x + [None] * (n1 // 2))):
        o_ref[0, 0, s] = re.astype(o_ref.dtype)
        o_ref[0, 1, s] = im.astype(o_ref.dtype)


def _outer_fwd(u, *, real_input, tn2, ct, nseq=1):
    n2 = u.shape[-2]
    if real_input:
        p, c, n1 = nseq, u.shape[-1] // nseq, 2 * u.shape[0]
        in_spec = pl.BlockSpec((n1 // 2, tn2, ct), lambda pi, r, j: (0, r, pi * (c // ct) + j))
    else:
        p, c, n1 = u.shape[0], u.shape[-1], 2 * u.shape[2]
        in_spec = pl.BlockSpec((1, 2, n1 // 2, tn2, ct), lambda pi, r, j: (pi, 0, 0, r, j))
    return pl.pallas_call(
        functools.partial(_outer_fwd_kernel, real_input=real_input),
        grid=(p, n2 // tn2, c // ct), in_specs=[in_spec],
        out_specs=pl.BlockSpec((1, 2, n1, tn2, ct), lambda pi, r, j: (pi, 0, 0, r, j)),
        out_shape=jax.ShapeDtypeStruct((p, 2, n1, n2, c), u.dtype),
        compiler_params=_cparams(("arbitrary", "arbitrary", "arbitrary")),
        name="hy_outer_filt" if real_input else "hy_outer_fwd",
    )(u)


def _outer_inv_kernel(a_ref, u_ref, g_ref, sk_ref, o_ref, *next_ref):
    n1 = a_ref.shape[2]
    f = lambda ref, r, i: ref[0, r, i].astype(F32)
    y = _ifft_dit([(f(a_ref, 0, s), f(a_ref, 1, s)) for s in range(n1)], n1 // 2)
    sk = sk_ref[...]
    z = [(f(g_ref, 0, i) * (re + f(u_ref, 0, i) * sk), f(g_ref, 1, i) * (im + f(u_ref, 1, i) * sk))
         for i, (re, im) in enumerate(y)]
    for i, (re, im) in enumerate(z):
        o_ref[0, 0, i] = re.astype(o_ref.dtype)
        o_ref[0, 1, i] = im.astype(o_ref.dtype)
    if next_ref:
        nxt = next_ref[0]
        for s, (re, im) in enumerate(_fft_dif(z + [None] * (n1 // 2))):
            nxt[0, 0, s] = re.astype(nxt.dtype)
            nxt[0, 1, s] = im.astype(nxt.dtype)


def _outer_inv(a, u, gate, skip, *, tn2, ct, with_next):
    p, _, n1, n2, c = a.shape
    blk = lambda rows: pl.BlockSpec((1, 2, rows, tn2, ct), lambda pi, r, j: (pi, 0, 0, r, j))
    out_specs, out_shape = [blk(n1 // 2)], [jax.ShapeDtypeStruct(u.shape, u.dtype)]
    if with_next:
        out_specs.append(blk(n1))
        out_shape.append(jax.ShapeDtypeStruct(a.shape, a.dtype))
    return pl.pallas_call(
        _outer_inv_kernel,
        grid=(p, n2 // tn2, c // ct),
        in_specs=[blk(n1), blk(n1 // 2), blk(n1 // 2), pl.BlockSpec((1, ct), lambda pi, r, j: (0, j))],
        out_specs=out_specs, out_shape=out_shape,
        compiler_params=_cparams(("arbitrary", "arbitrary", "arbitrary")),
        name="hy_outer_inv_fwd" if with_next else "hy_outer_inv",
    )(a, u, gate, skip)


def _dot3(m3, x):
    hi = x.astype(BF16)
    lo = (x - hi.astype(F32)).astype(BF16)
    return jnp.dot(m3, jnp.concatenate([hi, hi, lo], axis=0), preferred_element_type=F32)


def _mid_kernel(a_ref, g_ref, *rest, spectrum):
    np_, n2, ct = a_ref.shape[0], a_ref.shape[3], a_ref.shape[4]
    if spectrum:
        mm = _dot3
    else:
        mm = lambda m, x: jnp.dot(m, x.astype(BF16), preferred_element_type=F32)
    xs = [mm(g_ref[0], a_ref[p, :, 0].reshape(2 * n2, ct)) for p in range(np_)]
    if spectrum:
        sc_ref, o_ref = rest
        f, bw = xs
        sc = sc_ref[...]
        o_ref[0, 0, 0] = (f[:n2] + bw[:n2]) * sc
        o_ref[0, 1, 0] = (f[n2:] - bw[n2:]) * sc
    else:
        gt_ref, kf_ref, o_ref = rest
        kr, ki = kf_ref[0, 0, 0], kf_ref[0, 1, 0]
        ys = [jnp.concatenate([x[:n2] * kr - x[n2:] * ki, x[:n2] * ki + x[n2:] * kr], axis=0) for x in xs]
        for p in range(np_):
            o_ref[p, :, 0] = mm(gt_ref[0], ys[p]).reshape(2, n2, ct).astype(o_ref.dtype)


def _mid(a, g, gt=None, kf=None, scale=None, *, ct, kf_col0=0):
    p, _, n1, n2, c = a.shape
    ablk = pl.BlockSpec((p, 2, 1, n2, ct), lambda k, j: (0, 0, k, 0, j))
    gblk = pl.BlockSpec((1,) + g.shape[1:], lambda k, j: (k, 0, 0))
    oblk, oshape = ablk, a.shape
    if kf is None:
        ins = [a, g, scale]
        in_specs = [ablk, gblk, pl.BlockSpec((1, ct), lambda k, j: (0, j))]
        oblk, oshape = pl.BlockSpec((1, 2, 1, n2, ct), lambda k, j: (0, 0, k, 0, j)), (1,) + a.shape[1:]
    else:
        ins = [a, g, gt, kf]
        in_specs = [ablk, gblk, gblk,
                    pl.BlockSpec((1, 2, 1, n2, ct), lambda k, j: (0, 0, k, 0, kf_col0 + j))]
    return pl.pallas_call(
        functools.partial(_mid_kernel, spectrum=kf is None),
        grid=(n1, c // ct), in_specs=in_specs, out_specs=oblk,
        out_shape=jax.ShapeDtypeStruct(oshape, a.dtype),
        compiler_params=_cparams(("arbitrary", "arbitrary")),
        name="hy_spectrum" if kf is None else "hy_mid",
    )(*ins)


def _dft_tables(l):
    n = 2 * l
    n2 = DFT_N2
    n1 = n // n2
    bits = n1.bit_length() - 1
    k1 = np.array([int(format(s, "0%db" % bits)[::-1], 2) for s in range(n1)], dtype=np.float64)
    kk = k1[:, None, None] + n1 * np.arange(n2, dtype=np.float64)[None, :, None]
    th = 2.0 * np.pi * kk * np.arange(n2, dtype=np.float64)[None, None, :] / n
    c, s = np.cos(th), np.sin(th)
    g = np.concatenate([np.concatenate([c, s], axis=2), np.concatenate([-s, c], axis=2)], axis=1)

    def split3(m):
        m = jnp.asarray(m, F32)
        hi = m.astype(BF16)
        lo = (m - hi.astype(F32)).astype(BF16)
        return jnp.concatenate([hi, lo, hi], axis=2)

    return split3(g), jnp.asarray(g, BF16), jnp.asarray(np.transpose(g, (0, 2, 1)), BF16)


def _pair_view(u, n1h):
    b, l, c = u.shape
    return u.reshape(b // 2, 2, n1h, l // n1h, c)


def _long_convs(u, gates, skips, kf, tabs, *, ct):
    _, g, gt = tabs
    c = u.shape[-1]
    a = _outer_fwd(u, real_input=False, tn2=32, ct=256)
    z = u
    for o, (gate, skip) in enumerate(zip(gates, skips)):
        a = _mid(a, g, gt, kf, ct=ct, kf_col0=o * (c // ct))
        last = o == len(gates) - 1
        res = _outer_inv(a, z, gate, skip, tn2=32, ct=256, with_next=not last)
        z, a = (res[0], None) if last else res
    return z


def _out1_kernel(x_ref, z_ref, sg_ref, w_ref, ada_ref, fw_ref, o_ref, *, d):
    y = jnp.dot((z_ref[0].astype(F32) * sg_ref[0].astype(F32)).astype(BF16), w_ref[...],
                preferred_element_type=F32)
    x = x_ref[0] + ada_ref[0][:, 2 * d:3 * d] * y
    o_ref[0] = _row_rms(x) * fw_ref[...]


def _out1(x, z, sg, w, ada, fw, *, tl):
    b, l, d = x.shape
    row = lambda bi, i: (bi, i, 0)
    blk = pl.BlockSpec((1, tl, d), row)
    return pl.pallas_call(
        functools.partial(_out1_kernel, d=d),
        grid=(b, l // tl),
        in_specs=[blk, blk, blk, pl.BlockSpec(w.shape, lambda bi, i: (0, 0)),
                  pl.BlockSpec((1, 1, ada.shape[-1]), lambda bi, i: (bi, 0, 0)),
                  pl.BlockSpec(fw.shape, lambda bi, i: (0, 0))],
        out_specs=blk,
        out_shape=jax.ShapeDtypeStruct((b, l, d), F32),
        compiler_params=_cparams(("arbitrary", "arbitrary")),
        name="out1",
    )(x, z, sg, w, ada, fw)


def _swap_cols(w, q):
    return w[..., np.arange(w.shape[-1]) ^ q]


def _pack_attn_w_in(w):
    d = w.shape[0]
    o = 0
    wq = w[:, o:o + 512]; o += 512
    wk = w[:, o:o + 128]; o += 128
    wv = w[:, o:o + 128]; o += 128
    wcq = w[:, o:o + MLA_Q_RANK]; o += MLA_Q_RANK
    wckv = w[:, o:o + MLA_KV_RANK]; o += MLA_KV_RANK
    wkpe = w[:, o:o + MLA_ROPE_DIM]; o += MLA_ROPE_DIM
    wg = w[:, o:]
    qa, qm = GQA_HEAD_DIM // 4, MLA_ROPE_DIM // 4

    def pe_chunk(wp):
        return jnp.concatenate([jnp.zeros((d, 64), w.dtype), wp, jnp.zeros((d, 32), w.dtype)], axis=1)

    kpe, kpe_sw = pe_chunk(wkpe), pe_chunk(_swap_cols(wkpe, qm))
    kv_part = jnp.concatenate([wk, wv, wckv, kpe], axis=1)
    lat = jnp.concatenate([wq, _swap_cols(wq, qa), wk, _swap_cols(wk, qa), wv, wcq, wckv, kpe, kpe_sw, wg], axis=1)
    return lat.astype(BF16), kv_part.astype(BF16)


def _pack_mla_up(w_uq, w_ukv):
    dq = MLA_NOPE_DIM + MLA_ROPE_DIM
    r = w_uq.shape[0]
    z = lambda n: jnp.zeros((r, n), w_uq.dtype)
    uq, uq_sw = [], []
    for h in range(MLA_HEADS):
        nope, pe = w_uq[:, dq * h:dq * h + MLA_NOPE_DIM], w_uq[:, dq * h + MLA_NOPE_DIM:dq * (h + 1)]
        uq += [nope, pe, z(LANE - dq)]
        uq_sw += [z(MLA_NOPE_DIM), _swap_cols(pe, MLA_ROPE_DIM // 4), z(LANE - dq)]
    dkv = MLA_NOPE_DIM + MLA_V_DIM
    kn = jnp.concatenate(
        [jnp.concatenate([w_ukv[:, dkv * h:dkv * h + MLA_NOPE_DIM],
                          jnp.zeros((w_ukv.shape[0], LANE - MLA_NOPE_DIM), w_ukv.dtype)], axis=1)
         for h in range(MLA_HEADS)], axis=1)
    vm = jnp.concatenate(
        [jnp.concatenate([w_ukv[:, dkv * h + MLA_NOPE_DIM:dkv * (h + 1)],
                          jnp.zeros((w_ukv.shape[0], LANE - MLA_V_DIM), w_ukv.dtype)], axis=1)
         for h in range(MLA_HEADS)], axis=1)
    return jnp.concatenate(uq + uq_sw, axis=1).astype(BF16), jnp.concatenate([kn, vm], axis=1).astype(BF16)


def _head_ones2(width):
    i = np.arange(width) // GQA_HEAD_DIM
    blk = (i[:, None] == i[None, :]).astype(np.float32)
    return jnp.asarray(np.concatenate([blk, blk], axis=0), BF16)


def _rope_tables(l):
    rows = (jnp.arange(l, dtype=jnp.int32) // GRID_W).astype(F32)[:, None]
    cols = (jnp.arange(l, dtype=jnp.int32) % GRID_W).astype(F32)[:, None]

    def tab(rot_dim):
        q = rot_dim // 4
        inv = ROPE_BASE ** (-jnp.arange(q, dtype=F32) / q)
        ar, ac = rows * inv, cols * inv
        cos = jnp.concatenate([jnp.cos(ar)] * 2 + [jnp.cos(ac)] * 2, axis=1)
        sin = jnp.concatenate([-jnp.sin(ar), jnp.sin(ar), -jnp.sin(ac), jnp.sin(ac)], axis=1)
        return cos, sin

    ca, sa = tab(GQA_HEAD_DIM)
    ca = jnp.concatenate([ca] * (LANE // GQA_HEAD_DIM), axis=1)
    sa = jnp.concatenate([sa] * (LANE // GQA_HEAD_DIM), axis=1)
    cm, sm = tab(MLA_ROPE_DIM)
    one, zero = jnp.ones((l, 1), F32), jnp.zeros((l, 1), F32)
    cm = jnp.concatenate([jnp.tile(one, (1, 64)), cm, jnp.tile(one, (1, 32))], axis=1)
    sm = jnp.concatenate([jnp.tile(zero, (1, 64)), sm, jnp.tile(zero, (1, 32))], axis=1)
    return ca, sa, cm, sm


def _pad2(a, r, c):
    return jnp.pad(a, ((0, r - a.shape[0]), (0, c - a.shape[1])))


def kernel(x, c, ctx, c_ctx, ada_w, ada_b, norm_w, attn_w_in, attn_q_norm, attn_k_norm, mla_q_norm, mla_kv_norm, mla_w_uq, mla_w_ukv, attn_w_out, hy_w_in, hy_conv_w, hy_conv_b, hy_ffn_w1, hy_ffn_b1, hy_ffn_w2, hy_ffn_b2, hy_ffn_w3, hy_ffn_b3, hy_freq, hy_skip, hy_w_out, final_norm_w):
    b, l, d = x.shape
    lc = ctx.shape[1]
    tl = min(256, l)

    rows = -(-(b + 1) // 8) * 8
    cs = jnp.concatenate([c, c_ctx[None, :], jnp.zeros((rows - b - 1, d), F32)], axis=0)
    ada = _ada(cs, ada_w, ada_b)
    ada_lat = [ada[i, :b].reshape(b, 1, 3 * d) for i in range(ada.shape[0])]
    ada_ctx0 = jnp.broadcast_to(ada[0, b].reshape(1, 1, 3 * d), (b, 1, 3 * d))

    w_lat, w_kv = _pack_attn_w_in(attn_w_in[0])
    wuq, wukv = _pack_mla_up(mla_w_uq[0], mla_w_ukv[0])
    nw0 = norm_w[0].reshape(1, d)
    def norm_rows(wn, width):
        sw = _swap_cols(wn, GQA_HEAD_DIM // 4)
        return jnp.stack([jnp.tile(wn, width // GQA_HEAD_DIM), jnp.tile(sw, width // GQA_HEAD_DIM)])

    qn, kn = norm_rows(attn_q_norm[0], _QA_W), norm_rows(attn_k_norm[0], LANE)
    cqn = mla_q_norm[0].reshape(1, MLA_Q_RANK)
    ckvn = mla_kv_norm[0].reshape(1, MLA_KV_RANK)
    o2q, o2k = _head_ones2(_QA_W), _head_ones2(LANE)
    tabs = _rope_tables(l)
    qa, ka, va, qm, km, vm, sg = _prep(x, ada_lat[0], nw0, w_lat, qn, kn, cqn, ckvn, wuq, wukv, o2q, o2k, tabs,
                                       latent=True, tl=tl)
    kac, vac, kmc, vmc = _prep(ctx, ada_ctx0, nw0, w_kv, None, kn, None, ckvn, None, wukv, None, o2k, None,
                               latent=False, tl=min(tl, lc))
    tq, tk = min(2048, l), min(512, l // 2)
    oa = _attention(qa, kac, vac, ka, va, q_shared=True, kv_group=2, tq=tq, tk=tk)
    om = _attention(qm, kmc, vmc, km, vm, q_shared=False, kv_group=1, tq=tq, tk=tk)
    x1 = _out0(x, oa, om, sg, attn_w_out[0].astype(BF16), ada_lat[0], tl=tl)

    nw1 = norm_w[1].reshape(1, d)
    u = _hyin(x1, ada_lat[1], nw1, hy_w_in[0].astype(BF16), hy_conv_w[0], hy_conv_b[0].reshape(1, -1),
              tl=min(512, l))
    n1 = 2 * l // DFT_N2
    n1h = n1 // 2
    tabs_d = _dft_tables(l)

    t = jnp.linspace(0.0, 1.0, l, dtype=F32)[:, None]
    wpos = (2.0 * math.pi / l) * jnp.arange(l, dtype=F32)[:, None]
    bands = jnp.linspace(1e-4, HY_BANDS - 1, HY_BANDS, dtype=F32)
    emb = jnp.concatenate([t, jnp.cos(wpos * bands), -jnp.sin(wpos * bands)], axis=-1)
    deltas = jnp.abs(jnp.linspace(math.log(HY_DECAY_TARGET) / HY_SLOW_DECAY,
                                  math.log(HY_DECAY_TARGET) / HY_FAST_DECAY, d, dtype=F32)).reshape(1, d)
    wf = hy_ffn_w3.shape[-1]
    oc = HY_ORDER * d

    def by_direction(a):
        return a.reshape(-1, HY_ORDER, 2, d).transpose(0, 2, 1, 3).reshape(-1, wf)

    hw, asum = _filters(_pad2(emb, l, LANE), _pad2(hy_ffn_w1[0], LANE, LANE), _pad2(hy_ffn_b1[0][None], 1, LANE),
                        _pad2(hy_ffn_w2[0], LANE, LANE), _pad2(hy_ffn_b2[0][None], 1, LANE),
                        _pad2(by_direction(hy_ffn_w3[0]), LANE, wf), by_direction(hy_ffn_b3[0][None]),
                        _pad2(hy_freq[0][None], 1, LANE), deltas, tl=tl)
    l1 = asum[:, :oc] + asum[:, oc:]
    ct = min(1024, d)
    af = _outer_fwd(hw.reshape(n1h, DFT_N2, wf), real_input=True, nseq=2, tn2=32, ct=256)
    kf = _mid(af, tabs_d[0], scale=1.0 / (l1 * (2 * l)), ct=oc)

    v2, x1g, x2g = (_pair_view(u[i], n1h) for i in range(3))
    z = _long_convs(v2, [x1g, x2g], [hy_skip[0, o:o + 1] for o in range(HY_ORDER)], kf, tabs_d, ct=ct)
    z = z.reshape(b, l, d)
    return _out1(x1, z, u[3], hy_w_out[0].astype(BF16), ada_lat[1], final_norm_w.reshape(1, d), tl=tl)
```

```python
import functools
import math

import numpy as np
import jax
import jax.numpy as jnp
from jax import lax
from jax.experimental import pallas as pl
from jax.experimental.pallas import tpu as pltpu

EPS = 1e-6
GRID_W = 64
ROPE_BASE = 10000.0
GQA_HEADS, GQA_KV_HEADS, GQA_HEAD_DIM = 8, 2, 64
MLA_HEADS, MLA_Q_RANK, MLA_KV_RANK = 8, 256, 128
MLA_NOPE_DIM, MLA_ROPE_DIM, MLA_V_DIM = 64, 32, 64
HY_ORDER, HY_SHORT, HY_BANDS, HY_FFN = 2, 3, 16, 64
HY_FAST_DECAY, HY_SLOW_DECAY, HY_DECAY_TARGET = 0.3, 1.5, 1e-2
LANE = 128
DFT_N2 = 128
VMEM_LIMIT = 56 * 1024 * 1024
LOG2E = 1.4426950408889634
HI = lax.Precision.HIGHEST
F32 = jnp.float32
BF16 = jnp.bfloat16


def _cparams(sem):
    return pltpu.CompilerParams(dimension_semantics=sem, vmem_limit_bytes=VMEM_LIMIT)


def _per_chunk(fn, *arrs):
    width = arrs[0].shape[-1]
    outs = [fn(*[a[:, c:c + LANE] for a in arrs]) for c in range(0, width, LANE)]
    return outs[0] if len(outs) == 1 else jnp.concatenate(outs, axis=-1)


def _lane_iota(shape):
    return lax.broadcasted_iota(jnp.int32, shape, len(shape) - 1)


def _head_rsqrt(x, ones2):
    ss = x * x
    hi = ss.astype(BF16)
    lo = (ss - hi.astype(F32)).astype(BF16)
    tot = jnp.dot(jnp.concatenate([hi, lo], axis=-1), ones2, preferred_element_type=F32)
    return lax.rsqrt(tot * (1.0 / GQA_HEAD_DIM) + EPS)


def _rope(x, x_sw, cos, sin_signed):
    return _per_chunk(lambda c, w: c * cos + w * sin_signed, x, x_sw)


def _row_rms(x):
    return x * lax.rsqrt(jnp.mean(x * x, axis=-1, keepdims=True) + EPS)


def _silu(x):
    return x * (1.0 / (1.0 + jnp.exp(-x)))


def _ada_kernel(c_ref, w_ref, b_ref, o_ref):
    s = _silu(c_ref[...])
    o_ref[0] = jnp.dot(s, w_ref[0], precision=HI, preferred_element_type=F32) + b_ref[0]


def _ada(cs, ada_w, ada_b):
    depth, d, d3 = ada_w.shape
    rows = cs.shape[0]
    nt = d3 // d
    return pl.pallas_call(
        _ada_kernel,
        grid=(depth, nt),
        in_specs=[pl.BlockSpec((rows, d), lambda i, j: (0, 0)),
                  pl.BlockSpec((1, d, d), lambda i, j: (i, 0, j)),
                  pl.BlockSpec((1, 1, d), lambda i, j: (i, 0, j))],
        out_specs=pl.BlockSpec((1, rows, d), lambda i, j: (i, 0, j)),
        out_shape=jax.ShapeDtypeStruct((depth, rows, d3), F32),
        compiler_params=_cparams(("arbitrary", "arbitrary")),
        name="ada",
    )(cs, ada_w, ada_b.reshape(depth, 1, d3))


_QA_W, _KA_W, _VA_W, _KPE_W = 512, 512, 256, 128


def _prep_kernel(*refs, latent, d):
    if latent:
        (x_ref, ada_ref, nw_ref, w_ref, qn_ref, kn_ref, cqn_ref, ckvn_ref, wuq_ref, wukv_ref, o2q_ref, o2k_ref,
         ca_ref, sa_ref, cm_ref, sm_ref,
         qa_o, ka_o, va_o, qm_o, km_o, vm_o, sg_o) = refs
    else:
        (x_ref, ada_ref, nw_ref, w_ref, kn_ref, ckvn_ref, wukv_ref, o2k_ref,
         ka_o, va_o, km_o, vm_o) = refs
    ada = ada_ref[0]
    shift, scale = ada[:, :d], ada[:, d:2 * d]
    h = (_row_rms(x_ref[0]) * nw_ref[...]) * (1.0 + scale) + shift
    p = jnp.dot(h.astype(BF16), w_ref[...], preferred_element_type=F32)
    off = 0

    def take(width):
        nonlocal off
        off += width
        return p[:, off - width:off]

    if latent:
        qa, qa_sw = take(_QA_W), take(_QA_W)
        k, k_sw = take(LANE), take(LANE)
    else:
        k = take(LANE)
    v = take(LANE)
    if latent:
        cq = take(MLA_Q_RANK)
    ckv = take(MLA_KV_RANK)
    kpe = take(LANE)

    kn = kn_ref[...]
    rk = _head_rsqrt(k, o2k_ref[...])
    k = k * rk * kn[0:1]
    if latent:
        k = _rope(k, k_sw * rk * kn[1:2], ca_ref[...], sa_ref[...])
        kpe = _rope(kpe, take(LANE), cm_ref[...], sm_ref[...])
    low = _lane_iota(k.shape) < (LANE // 2)
    k_x = pltpu.roll(k, LANE // 2, axis=1)
    v_x = pltpu.roll(v, LANE // 2, axis=1)
    zero = jnp.zeros_like(k)
    ka = [jnp.where(low, k, zero), jnp.where(low, zero, k_x), jnp.where(low, k_x, zero), jnp.where(low, zero, k)]
    ka_o[0] = jnp.concatenate(ka, axis=-1).astype(BF16)
    one64 = (_lane_iota(k.shape) == LANE // 2).astype(F32)
    va = jnp.concatenate([jnp.where(low, v, one64), jnp.where(low, v_x, one64)], axis=-1)
    va_o[0] = va.T.astype(BF16)
    ckv_n = (_row_rms(ckv) * ckvn_ref[...]).astype(BF16)
    kv = jnp.dot(ckv_n, wukv_ref[...], preferred_element_type=F32)
    nk = MLA_HEADS * LANE
    km_o[0] = _per_chunk(lambda c: c + kpe, kv[:, :nk]).astype(BF16)
    vm_o[0] = _per_chunk(lambda c: c + one64, kv[:, nk:]).T.astype(BF16)
    if latent:
        qn = qn_ref[...]
        rq = _head_rsqrt(qa, o2q_ref[...])
        qa = _rope(qa * rq * qn[0:1], qa_sw * rq * qn[1:2], ca_ref[...], sa_ref[...])
        qa_o[0] = (qa * (GQA_HEAD_DIM ** -0.5 * LOG2E)).astype(BF16)
        cq_n = (_row_rms(cq) * cqn_ref[...]).astype(BF16)
        qm = jnp.dot(cq_n, wuq_ref[...], preferred_element_type=F32)
        qm = _rope(qm[:, :nk], qm[:, nk:], cm_ref[...], sm_ref[...])
        qm_o[0] = (qm * ((MLA_NOPE_DIM + MLA_ROPE_DIM) ** -0.5 * LOG2E)).astype(BF16)
        sg_o[0] = _silu(take(d)).astype(BF16)


def _prep(x, ada, nw, w, qn, kn, cqn, ckvn, wuq, wukv, o2q, o2k, tabs, *, latent, tl):
    b, l, d = x.shape
    grid = (l // tl, b)
    row = lambda i, j: (j, i, 0)
    const = lambda i, j: (0, 0)
    tab = lambda i, j: (i, 0)
    xspec = pl.BlockSpec((1, tl, d), row)
    adaspec = pl.BlockSpec((1, 1, ada.shape[-1]), lambda i, j: (j, 0, 0))

    def full(a):
        return pl.BlockSpec(a.shape, const)

    def out(width):
        return (pl.BlockSpec((1, tl, width), row), jax.ShapeDtypeStruct((b, l, width), BF16))

    def out_t(width):
        return (pl.BlockSpec((1, width, tl), lambda i, j: (j, 0, i)), jax.ShapeDtypeStruct((b, width, l), BF16))

    if latent:
        ca, sa, cm, sm = tabs
        ins = [x, ada, nw, w, qn, kn, cqn, ckvn, wuq, wukv, o2q, o2k, ca, sa, cm, sm]
        in_specs = [xspec, adaspec, full(nw), full(w), full(qn), full(kn), full(cqn), full(ckvn),
                    full(wuq), full(wukv), full(o2q), full(o2k)] + [pl.BlockSpec((tl, t.shape[1]), tab) for t in tabs]
        outs = [out(_QA_W), out(_KA_W), out_t(_VA_W), out(MLA_HEADS * LANE), out(MLA_HEADS * LANE),
                out_t(MLA_HEADS * LANE), out(d)]
    else:
        ins = [x, ada, nw, w, kn, ckvn, wukv, o2k]
        in_specs = [xspec, adaspec, full(nw), full(w), full(kn), full(ckvn), full(wukv), full(o2k)]
        outs = [out(_KA_W), out_t(_VA_W), out(MLA_HEADS * LANE), out_t(MLA_HEADS * LANE)]
    return pl.pallas_call(
        functools.partial(_prep_kernel, latent=latent, d=d),
        grid=grid, in_specs=in_specs,
        out_specs=[o[0] for o in outs], out_shape=[o[1] for o in outs],
        compiler_params=_cparams(("arbitrary", "arbitrary")),
        name="prep_lat" if latent else "prep_ctx",
    )(*ins)


_ATT_SUB = 256


def _attn_kernel(q_ref, kc_ref, vc_ref, kl_ref, vl_ref, o_ref, s0, s1, p0, p1, acc_ref, *, tk, q_shared):
    tq = q_ref.shape[1]
    lk, lc = kl_ref.shape[1], kc_ref.shape[1]
    nblk = lk // tk
    t = _ATT_SUB
    dn = (((1,), (1,)), ((), ()))
    lanes = lambda e: slice(e * LANE, (e + 1) * LANE)

    def q_sub(e, qc):
        return q_ref[0, qc * t:(qc + 1) * t, lanes(0 if q_shared else e)]

    def vrows(v_ref, e, cols):
        return v_ref[0, :, cols] if q_shared else v_ref[0, lanes(e), cols]

    def mxu_phase(s_next, k_next, p_prev, v_prev, nk_prev, alpha):
        smax = [[], []]
        for qc in range(tq // t):
            cols = slice(qc * t, (qc + 1) * t)
            for e in range(2):
                cmax = None
                for kr in range(tk // t if s_next is not None else 0):
                    rows = slice(kr * t, (kr + 1) * t)
                    s_tile = lax.dot_general(k_next(e, kr), q_sub(e, qc), dn, preferred_element_type=F32)
                    s_next[e, rows, cols] = s_tile
                    tmax = jnp.max(s_tile, axis=0, keepdims=True)
                    cmax = tmax if cmax is None else jnp.maximum(cmax, tmax)
                smax[e].append(cmax)
                if p_prev is not None:
                    pv = jnp.dot(v_prev(e), p_prev(e, slice(0, nk_prev * t), cols), preferred_element_type=F32)
                    acc_ref[e, :, cols] = alpha[e][:, cols] * acc_ref[e, :, cols] + pv
        return None if s_next is None else [jnp.concatenate(c, axis=-1) for c in smax]

    def softmax_phase(s_cur, smax, p_cur, m):
        m_new, alpha, ps = [], [], []
        for e in range(2):
            s = s_cur[e]
            mn = jnp.maximum(m[e], smax[e])
            p = jnp.exp2(s - mn).astype(BF16)
            if p_cur is not None:
                p_cur[e] = p
            m_new.append(mn)
            alpha.append(jnp.exp2(m[e] - mn))
            ps.append(p)
        return m_new, alpha, ps

    def from_ref(ref):
        return lambda e, rows, cols: ref[e, rows, cols]

    def k_lat(r):
        return lambda e, kr: kl_ref[0, pl.ds(pl.multiple_of(r + kr * t, t), t), lanes(e)]

    def v_lat(r):
        return lambda e: vrows(vl_ref, e, pl.ds(pl.multiple_of(r, t), tk))

    acc_ref[...] = jnp.zeros_like(acc_ref)
    ones = [jnp.ones((1, tq), F32)] * 2
    m = [jnp.full((1, tq), -1e30, F32)] * 2

    s_ctx = [lax.dot_general(kc_ref[0, :, lanes(e)], q_ref[0, :, lanes(0 if q_shared else e)], dn,
                             preferred_element_type=F32) for e in range(2)]
    x0 = mxu_phase(s0, k_lat(0), None, None, 0, None)
    m, _, p_ctx = softmax_phase(s_ctx, [jnp.max(s, axis=0, keepdims=True) for s in s_ctx], None, m)
    x1 = mxu_phase(s1, k_lat(tk), lambda e, rows, cols: p_ctx[e][rows, cols],
                   lambda e: vrows(vc_ref, e, slice(0, lc)), lc // t, ones)
    m, alpha, _ = softmax_phase(s0, x0, p0, m)

    def body(j, carry):
        m, alpha, x1 = (list(c) for c in carry)
        r = pl.multiple_of(2 * j * tk, tk)
        x0 = mxu_phase(s0, k_lat(r + 2 * tk), from_ref(p0), v_lat(r), tk // t, alpha)
        m, alpha, _ = softmax_phase(s1, x1, p1, m)
        x1 = mxu_phase(s1, k_lat(r + 3 * tk), from_ref(p1), v_lat(r + tk), tk // t, alpha)
        m, alpha, _ = softmax_phase(s0, x0, p0, m)
        return tuple(m), tuple(alpha), tuple(x1)

    m, alpha, x1 = lax.fori_loop(0, (nblk - 2) // 2, body, (tuple(m), tuple(alpha), tuple(x1)))
    mxu_phase(None, None, from_ref(p0), v_lat((nblk - 2) * tk), tk // t, alpha)
    m, alpha, _ = softmax_phase(s1, list(x1), p1, list(m))
    mxu_phase(None, None, from_ref(p1), v_lat((nblk - 1) * tk), tk // t, alpha)
    dv = LANE // 2
    outs = [acc_ref[e, :dv, :] / acc_ref[e, dv:dv + 1, :] for e in range(2)]
    o_ref[0] = jnp.concatenate(outs, axis=0).T.astype(o_ref.dtype)


def _attention(q, kc, vc, kl, vl, *, q_shared, kv_group, tq, tk):
    b, l, _ = q.shape
    wq = LANE if q_shared else 2 * LANE
    pairs = q.shape[-1] // wq
    lc = kc.shape[1]
    assert (l // tk) % 2 == 0, "key blocks are consumed two per loop trip"
    vw = LANE if q_shared else 2 * LANE
    kv = lambda bi, j, i: (bi, 0, j // kv_group)
    vt = lambda bi, j, i: (bi, j // kv_group, 0)
    return pl.pallas_call(
        functools.partial(_attn_kernel, tk=tk, q_shared=q_shared),
        grid=(b, pairs, l // tq),
        in_specs=[pl.BlockSpec((1, tq, wq), lambda bi, j, i: (bi, i, j)),
                  pl.BlockSpec((1, lc, 2 * LANE), kv), pl.BlockSpec((1, vw, lc), vt),
                  pl.BlockSpec((1, l, 2 * LANE), kv), pl.BlockSpec((1, vw, l), vt)],
        out_specs=pl.BlockSpec((1, tq, LANE), lambda bi, j, i: (bi, i, j)),
        out_shape=jax.ShapeDtypeStruct((b, l, pairs * LANE), BF16),
        scratch_shapes=[pltpu.VMEM((2, tk, tq), F32), pltpu.VMEM((2, tk, tq), F32),
                        pltpu.VMEM((2, tk, tq), BF16), pltpu.VMEM((2, tk, tq), BF16),
                        pltpu.VMEM((2, LANE, tq), F32)],
        compiler_params=_cparams(("arbitrary", "arbitrary", "arbitrary")),
        name="attn_gqa" if q_shared else "attn_mla",
    )(q, kc, vc, kl, vl)


def _out0_kernel(x_ref, oa_ref, om_ref, sg_ref, w_ref, ada_ref, o_ref, *, d):
    o = jnp.concatenate([oa_ref[0], om_ref[0]], axis=-1).astype(F32) * sg_ref[0].astype(F32)
    y = jnp.dot(o.astype(BF16), w_ref[...], preferred_element_type=F32)
    o_ref[0] = x_ref[0] + ada_ref[0][:, 2 * d:3 * d] * y


def _out0(x, oa, om, sg, w, ada, *, tl):
    b, l, d = x.shape
    row = lambda bi, i: (bi, i, 0)
    return pl.pallas_call(
        functools.partial(_out0_kernel, d=d),
        grid=(b, l // tl),
        in_specs=[pl.BlockSpec((1, tl, d), row), pl.BlockSpec((1, tl, oa.shape[-1]), row),
                  pl.BlockSpec((1, tl, om.shape[-1]), row), pl.BlockSpec((1, tl, d), row),
                  pl.BlockSpec(w.shape, lambda bi, i: (0, 0)),
                  pl.BlockSpec((1, 1, ada.shape[-1]), lambda bi, i: (bi, 0, 0))],
        out_specs=pl.BlockSpec((1, tl, d), row),
        out_shape=jax.ShapeDtypeStruct((b, l, d), F32),
        compiler_params=_cparams(("arbitrary", "arbitrary")),
        name="out0",
    )(x, oa, om, sg, w, ada)


_HALO = 8
_HY_STORE = BF16


def _hyin_kernel(x_ref, xp_ref, xn_ref, ada_ref, nw_ref, w_ref, cw_ref, cb_ref, *o_refs, d):
    i = pl.program_id(1)
    tl = x_ref.shape[1]
    n_conv = len(o_refs) - 1
    ada = ada_ref[0]
    shift, scale = ada[:, :d], ada[:, d:2 * d]

    def mod(x):
        return (_row_rms(x) * nw_ref[...]) * (1.0 + scale) + shift

    hp = mod(xp_ref[0]) * (i > 0).astype(F32)
    hn = mod(xn_ref[0]) * (i < pl.num_programs(1) - 1).astype(F32)
    h = jnp.concatenate([mod(x_ref[0]), hp, hn], axis=0).astype(BF16)
    rows = lax.broadcasted_iota(jnp.int32, (tl, d), 0)
    cw = cw_ref[...]
    for n in range(n_conv + 1):
        p = jnp.dot(h, w_ref[:, n * d:(n + 1) * d], preferred_element_type=F32)
        pm = p[0:tl]
        if n < n_conv:
            prev = jnp.where(rows == 0, p[tl + _HALO - 1:tl + _HALO], pltpu.roll(pm, 1, axis=0))
            nxt = jnp.where(rows == tl - 1, p[tl + _HALO:tl + _HALO + 1], pltpu.roll(pm, tl - 1, axis=0))
            c0, c1, c2 = (cw[j:j + 1, n * d:(n + 1) * d] for j in range(HY_SHORT))
            out = prev * c0 + pm * c1 + nxt * c2 + cb_ref[:, n * d:(n + 1) * d]
        else:
            out = _silu(pm)
        o_refs[n][0] = out.astype(o_refs[n].dtype)


def _hyin(x, ada, nw, w, cw, cb, *, tl):
    b, l, d = x.shape
    ng = w.shape[1] // d
    tb = tl // _HALO
    nb = l // _HALO
    const = lambda bi, i: (0, 0)
    blk = pl.BlockSpec((1, tl, d), lambda bi, i: (bi, i, 0))
    return pl.pallas_call(
        functools.partial(_hyin_kernel, d=d),
        grid=(b, l // tl),
        in_specs=[blk,
                  pl.BlockSpec((1, _HALO, d), lambda bi, i: (bi, jnp.maximum(i * tb - 1, 0), 0)),
                  pl.BlockSpec((1, _HALO, d), lambda bi, i: (bi, jnp.minimum((i + 1) * tb, nb - 1), 0)),
                  pl.BlockSpec((1, 1, ada.shape[-1]), lambda bi, i: (bi, 0, 0)),
                  pl.BlockSpec(nw.shape, const), pl.BlockSpec(w.shape, const),
                  pl.BlockSpec(cw.shape, const), pl.BlockSpec(cb.shape, const)],
        out_specs=[blk] * ng,
        out_shape=[jax.ShapeDtypeStruct((b, l, d), _HY_STORE)] * ng,
        compiler_params=_cparams(("arbitrary", "arbitrary")),
        name="hy_in",
    )(x, x, x, ada, nw, w, cw, cb)


def _filt_kernel(emb_ref, w1_ref, b1_ref, w2_ref, b2_ref, w3_ref, b3_ref, fr_ref, dl_ref, h_o, s_o, *, reps):
    emb = emb_ref[...]
    fr = fr_ref[...]
    hid = jnp.sin(fr * (jnp.dot(emb, w1_ref[...], precision=HI, preferred_element_type=F32) + b1_ref[...]))
    hid = jnp.sin(fr * (jnp.dot(hid, w2_ref[...], precision=HI, preferred_element_type=F32) + b2_ref[...]))
    h = jnp.dot(hid, w3_ref[...], precision=HI, preferred_element_type=F32) + b3_ref[...]
    win = jnp.exp(-emb[:, 0:1] * dl_ref[...])
    hw = h * jnp.concatenate([win] * reps, axis=-1)
    rows = lax.broadcasted_iota(jnp.int32, hw.shape, 0) + pl.program_id(0) * hw.shape[0]
    hw = jnp.where((rows == 0) & (_lane_iota(hw.shape) >= hw.shape[1] // 2), 0.0, hw)
    h_o[...] = hw

    @pl.when(pl.program_id(0) == 0)
    def _():
        s_o[...] = jnp.zeros_like(s_o)

    s_o[...] += jnp.sum(jnp.abs(hw), axis=0, keepdims=True)


def _filters(emb, w1, b1, w2, b2, w3, b3, fr, dl, *, tl):
    l = emb.shape[0]
    wo = w3.shape[1]
    const = lambda i: (0, 0)
    full = lambda a: pl.BlockSpec(a.shape, const)
    return pl.pallas_call(
        functools.partial(_filt_kernel, reps=wo // dl.shape[1]),
        grid=(l // tl,),
        in_specs=[pl.BlockSpec((tl, emb.shape[1]), lambda i: (i, 0)), full(w1), full(b1), full(w2), full(b2),
                  full(w3), full(b3), full(fr), full(dl)],
        out_specs=[pl.BlockSpec((tl, wo), lambda i: (i, 0)), pl.BlockSpec((1, wo), const)],
        out_shape=[jax.ShapeDtypeStruct((l, wo), F32), jax.ShapeDtypeStruct((1, wo), F32)],
        compiler_params=_cparams(("arbitrary",)),
        name="hy_filter",
    )(emb, w1, b1, w2, b2, w3, b3, fr, dl)


def _cmul_const(a, ang):
    ar, ai = a
    q = ang / (0.5 * math.pi)
    if abs(q - round(q)) < 1e-12:
        return [(ar, ai), (-ai, ar), (-ar, -ai), (ai, -ar)][int(round(q)) % 4]
    c, s = math.cos(ang), math.sin(ang)
    return (ar * c - ai * s, ar * s + ai * c)


def _fft_dif(x):
    x = list(x)
    n = len(x)
    half = n // 2
    while half >= 1:
        for base in range(0, n, 2 * half):
            for j in range(half):
                a, b = x[base + j], x[base + j + half]
                ang = -math.pi * j / half
                if b is None:
                    x[base + j + half] = None if a is None else _cmul_const(a, ang)
                else:
                    x[base + j] = (a[0] + b[0], a[1] + b[1])
                    x[base + j + half] = _cmul_const((a[0] - b[0], a[1] - b[1]), ang)
        half //= 2
    return x


def _ifft_dit(x, keep):
    x = list(x)
    n = len(x)
    half = 1
    while half <= n // 2:
        last = half == n // 2
        for base in range(0, n, 2 * half):
            for j in range(half):
                a = x[base + j]
                b = _cmul_const(x[base + j + half], math.pi * j / half)
                x[base + j] = (a[0] + b[0], a[1] + b[1])
                if not last or base + j + half < keep:
                    x[base + j + half] = (a[0] - b[0], a[1] - b[1])
        half *= 2
    return x[:keep]


def _outer_fwd_kernel(u_ref, o_ref, *, real_input):
    n1 = o_ref.shape[2]
    if real_input:
        x = [(u_ref[i], jnp.zeros_like(u_ref[i])) for i in range(n1 // 2)]
    else:
        x = [(u_ref[0, 0, i].astype(F32), u_ref[0, 1, i].astype(F32)) for i in range(n1 // 2)]
    for s, (re, im) in enumerate(_fft_dif(x + [None] * (n1 // 2))):
        o_ref[0, 0, s] = re.astype(o_ref.dtype)
        o_ref[0, 1, s] = im.astype(o_ref.dtype)


def _outer_fwd(u, *, real_input, tn2, ct, nseq=1):
    n2 = u.shape[-2]
    if real_input:
        p, c, n1 = nseq, u.shape[-1] // nseq, 2 * u.shape[0]
        in_spec = pl.BlockSpec((n1 // 2, tn2, ct), lambda pi, r, j: (0, r, pi * (c // ct) + j))
    else:
        p, c, n1 = u.shape[0], u.shape[-1], 2 * u.shape[2]
        in_spec = pl.BlockSpec((1, 2, n1 // 2, tn2, ct), lambda pi, r, j: (pi, 0, 0, r, j))
    return pl.pallas_call(
        functools.partial(_outer_fwd_kernel, real_input=real_input),
        grid=(p, n2 // tn2, c // ct), in_specs=[in_spec],
        out_specs=pl.BlockSpec((1, 2, n1, tn2, ct), lambda pi, r, j: (pi, 0, 0, r, j)),
        out_shape=jax.ShapeDtypeStruct((p, 2, n1, n2, c), u.dtype),
        compiler_params=_cparams(("arbitrary", "arbitrary", "arbitrary")),
        name="hy_outer_filt" if real_input else "hy_outer_fwd",
    )(u)


def _outer_inv_kernel(a_ref, u_ref, g_ref, sk_ref, o_ref, *next_ref):
    n1 = a_ref.shape[2]
    f = lambda ref, r, i: ref[0, r, i].astype(F32)
    y = _ifft_dit([(f(a_ref, 0, s), f(a_ref, 1, s)) for s in range(n1)], n1 // 2)
    sk = sk_ref[...]
    z = [(f(g_ref, 0, i) * (re + f(u_ref, 0, i) * sk), f(g_ref, 1, i) * (im + f(u_ref, 1, i) * sk))
         for i, (re, im) in enumerate(y)]
    for i, (re, im) in enumerate(z):
        o_ref[0, 0, i] = re.astype(o_ref.dtype)
        o_ref[0, 1, i] = im.astype(o_ref.dtype)
    if next_ref:
        nxt = next_ref[0]
        for s, (re, im) in enumerate(_fft_dif(z + [None] * (n1 // 2))):
            nxt[0, 0, s] = re.astype(nxt.dtype)
            nxt[0, 1, s] = im.astype(nxt.dtype)


def _outer_inv(a, u, gate, skip, *, tn2, ct, with_next):
    p, _, n1, n2, c = a.shape
    blk = lambda rows: pl.BlockSpec((1, 2, rows, tn2, ct), lambda pi, r, j: (pi, 0, 0, r, j))
    out_specs, out_shape = [blk(n1 // 2)], [jax.ShapeDtypeStruct(u.shape, u.dtype)]
    if with_next:
        out_specs.append(blk(n1))
        out_shape.append(jax.ShapeDtypeStruct(a.shape, a.dtype))
    return pl.pallas_call(
        _outer_inv_kernel,
        grid=(p, n2 // tn2, c // ct),
        in_specs=[blk(n1), blk(n1 // 2), blk(n1 // 2), pl.BlockSpec((1, ct), lambda pi, r, j: (0, j))],
        out_specs=out_specs, out_shape=out_shape,
        compiler_params=_cparams(("arbitrary", "arbitrary", "arbitrary")),
        name="hy_outer_inv_fwd" if with_next else "hy_outer_inv",
    )(a, u, gate, skip)


def _dot3(m3, x):
    hi = x.astype(BF16)
    lo = (x - hi.astype(F32)).astype(BF16)
    return jnp.dot(m3, jnp.concatenate([hi, hi, lo], axis=0), preferred_element_type=F32)


def _mid_kernel(a_ref, g_ref, *rest, spectrum):
    np_, n2, ct = a_ref.shape[0], a_ref.shape[3], a_ref.shape[4]
    if spectrum:
        mm = _dot3
    else:
        mm = lambda m, x: jnp.dot(m, x.astype(BF16), preferred_element_type=F32)
    xs = [mm(g_ref[0], a_ref[p, :, 0].reshape(2 * n2, ct)) for p in range(np_)]
    if spectrum:
        sc_ref, o_ref = rest
        f, bw = xs
        sc = sc_ref[...]
        o_ref[0, 0, 0] = (f[:n2] + bw[:n2]) * sc
        o_ref[0, 1, 0] = (f[n2:] - bw[n2:]) * sc
    else:
        gt_ref, kf_ref, o_ref = rest
        kr, ki = kf_ref[0, 0, 0], kf_ref[0, 1, 0]
        ys = [jnp.concatenate([x[:n2] * kr - x[n2:] * ki, x[:n2] * ki + x[n2:] * kr], axis=0) for x in xs]
        for p in range(np_):
            o_ref[p, :, 0] = mm(gt_ref[0], ys[p]).reshape(2, n2, ct).astype(o_ref.dtype)


def _mid(a, g, gt=None, kf=None, scale=None, *, ct, kf_col0=0):
    p, _, n1, n2, c = a.shape
    ablk = pl.BlockSpec((p, 2, 1, n2, ct), lambda k, j: (0, 0, k, 0, j))
    gblk = pl.BlockSpec((1,) + g.shape[1:], lambda k, j: (k, 0, 0))
    oblk, oshape = ablk, a.shape
    if kf is None:
        ins = [a, g, scale]
        in_specs = [ablk, gblk, pl.BlockSpec((1, ct), lambda k, j: (0, j))]
        oblk, oshape = pl.BlockSpec((1, 2, 1, n2, ct), lambda k, j: (0, 0, k, 0, j)), (1,) + a.shape[1:]
    else:
        ins = [a, g, gt, kf]
        in_specs = [ablk, gblk, gblk,
                    pl.BlockSpec((1, 2, 1, n2, ct), lambda k, j: (0, 0, k, 0, kf_col0 + j))]
    return pl.pallas_call(
        functools.partial(_mid_kernel, spectrum=kf is None),
        grid=(n1, c // ct), in_specs=in_specs, out_specs=oblk,
        out_shape=jax.ShapeDtypeStruct(oshape, a.dtype),
        compiler_params=_cparams(("arbitrary", "arbitrary")),
        name="hy_spectrum" if kf is None else "hy_mid",
    )(*ins)


def _dft_tables(l):
    n = 2 * l
    n2 = DFT_N2
    n1 = n // n2
    bits = n1.bit_length() - 1
    k1 = np.array([int(format(s, "0%db" % bits)[::-1], 2) for s in range(n1)], dtype=np.float64)
    kk = k1[:, None, None] + n1 * np.arange(n2, dtype=np.float64)[None, :, None]
    th = 2.0 * np.pi * kk * np.arange(n2, dtype=np.float64)[None, None, :] / n
    c, s = np.cos(th), np.sin(th)
    g = np.concatenate([np.concatenate([c, s], axis=2), np.concatenate([-s, c], axis=2)], axis=1)

    def split3(m):
        m = jnp.asarray(m, F32)
        hi = m.astype(BF16)
        lo = (m - hi.astype(F32)).astype(BF16)
        return jnp.concatenate([hi, lo, hi], axis=2)

    return split3(g), jnp.asarray(g, BF16), jnp.asarray(np.transpose(g, (0, 2, 1)), BF16)


def _pair_view(u, n1h):
    b, l, c = u.shape
    return u.reshape(b // 2, 2, n1h, l // n1h, c)


def _long_convs(u, gates, skips, kf, tabs, *, ct):
    _, g, gt = tabs
    c = u.shape[-1]
    a = _outer_fwd(u, real_input=False, tn2=32, ct=256)
    z = u
    for o, (gate, skip) in enumerate(zip(gates, skips)):
        a = _mid(a, g, gt, kf, ct=ct, kf_col0=o * (c // ct))
        last = o == len(gates) - 1
        res = _outer_inv(a, z, gate, skip, tn2=32, ct=256, with_next=not last)
        z, a = (res[0], None) if last else res
    return z


def _out1_kernel(x_ref, z_ref, sg_ref, w_ref, ada_ref, fw_ref, o_ref, *, d):
    y = jnp.dot((z_ref[0].astype(F32) * sg_ref[0].astype(F32)).astype(BF16), w_ref[...],
                preferred_element_type=F32)
    x = x_ref[0] + ada_ref[0][:, 2 * d:3 * d] * y
    o_ref[0] = _row_rms(x) * fw_ref[...]


def _out1(x, z, sg, w, ada, fw, *, tl):
    b, l, d = x.shape
    row = lambda bi, i: (bi, i, 0)
    blk = pl.BlockSpec((1, tl, d), row)
    return pl.pallas_call(
        functools.partial(_out1_kernel, d=d),
        grid=(b, l // tl),
        in_specs=[blk, blk, blk, pl.BlockSpec(w.shape, lambda bi, i: (0, 0)),
                  pl.BlockSpec((1, 1, ada.shape[-1]), lambda bi, i: (bi, 0, 0)),
                  pl.BlockSpec(fw.shape, lambda bi, i: (0, 0))],
        out_specs=blk,
        out_shape=jax.ShapeDtypeStruct((b, l, d), F32),
        compiler_params=_cparams(("arbitrary", "arbitrary")),
        name="out1",
    )(x, z, sg, w, ada, fw)


def _swap_cols(w, q):
    return w[..., np.arange(w.shape[-1]) ^ q]


def _pack_attn_w_in(w):
    d = w.shape[0]
    o = 0
    wq = w[:, o:o + 512]; o += 512
    wk = w[:, o:o + 128]; o += 128
    wv = w[:, o:o + 128]; o += 128
    wcq = w[:, o:o + MLA_Q_RANK]; o += MLA_Q_RANK
    wckv = w[:, o:o + MLA_KV_RANK]; o += MLA_KV_RANK
    wkpe = w[:, o:o + MLA_ROPE_DIM]; o += MLA_ROPE_DIM
    wg = w[:, o:]
    qa, qm = GQA_HEAD_DIM // 4, MLA_ROPE_DIM // 4

    def pe_chunk(wp):
        return jnp.concatenate([jnp.zeros((d, 64), w.dtype), wp, jnp.zeros((d, 32), w.dtype)], axis=1)

    kpe, kpe_sw = pe_chunk(wkpe), pe_chunk(_swap_cols(wkpe, qm))
    kv_part = jnp.concatenate([wk, wv, wckv, kpe], axis=1)
    lat = jnp.concatenate([wq, _swap_cols(wq, qa), wk, _swap_cols(wk, qa), wv, wcq, wckv, kpe, kpe_sw, wg], axis=1)
    return lat.astype(BF16), kv_part.astype(BF16)


def _pack_mla_up(w_uq, w_ukv):
    dq = MLA_NOPE_DIM + MLA_ROPE_DIM
    r = w_uq.shape[0]
    z = lambda n: jnp.zeros((r, n), w_uq.dtype)
    uq, uq_sw = [], []
    for h in range(MLA_HEADS):
        nope, pe = w_uq[:, dq * h:dq * h + MLA_NOPE_DIM], w_uq[:, dq * h + MLA_NOPE_DIM:dq * (h + 1)]
        uq += [nope, pe, z(LANE - dq)]
        uq_sw += [z(MLA_NOPE_DIM), _swap_cols(pe, MLA_ROPE_DIM // 4), z(LANE - dq)]
    dkv = MLA_NOPE_DIM + MLA_V_DIM
    kn = jnp.concatenate(
        [jnp.concatenate([w_ukv[:, dkv * h:dkv * h + MLA_NOPE_DIM],
                          jnp.zeros((w_ukv.shape[0], LANE - MLA_NOPE_DIM), w_ukv.dtype)], axis=1)
         for h in range(MLA_HEADS)], axis=1)
    vm = jnp.concatenate(
        [jnp.concatenate([w_ukv[:, dkv * h + MLA_NOPE_DIM:dkv * (h + 1)],
                          jnp.zeros((w_ukv.shape[0], LANE - MLA_V_DIM), w_ukv.dtype)], axis=1)
         for h in range(MLA_HEADS)], axis=1)
    return jnp.concatenate(uq + uq_sw, axis=1).astype(BF16), jnp.concatenate([kn, vm], axis=1).astype(BF16)


def _head_ones2(width):
    i = np.arange(width) // GQA_HEAD_DIM
    blk = (i[:, None] == i[None, :]).astype(np.float32)
    return jnp.asarray(np.concatenate([blk, blk], axis=0), BF16)


def _rope_tables(l):
    rows = (jnp.arange(l, dtype=jnp.int32) // GRID_W).astype(F32)[:, None]
    cols = (jnp.arange(l, dtype=jnp.int32) % GRID_W).astype(F32)[:, None]

    def tab(rot_dim):
        q = rot_dim // 4
        inv = ROPE_BASE ** (-jnp.arange(q, dtype=F32) / q)
        ar, ac = rows * inv, cols * inv
        cos = jnp.concatenate([jnp.cos(ar)] * 2 + [jnp.cos(ac)] * 2, axis=1)
        sin = jnp.concatenate([-jnp.sin(ar), jnp.sin(ar), -jnp.sin(ac), jnp.sin(ac)], axis=1)
        return cos, sin

    ca, sa = tab(GQA_HEAD_DIM)
    ca = jnp.concatenate([ca] * (LANE // GQA_HEAD_DIM), axis=1)
    sa = jnp.concatenate([sa] * (LANE // GQA_HEAD_DIM), axis=1)
    cm, sm = tab(MLA_ROPE_DIM)
    one, zero = jnp.ones((l, 1), F32), jnp.zeros((l, 1), F32)
    cm = jnp.concatenate([jnp.tile(one, (1, 64)), cm, jnp.tile(one, (1, 32))], axis=1)
    sm = jnp.concatenate([jnp.tile(zero, (1, 64)), sm, jnp.tile(zero, (1, 32))], axis=1)
    return ca, sa, cm, sm


def _pad2(a, r, c):
    return jnp.pad(a, ((0, r - a.shape[0]), (0, c - a.shape[1])))


def kernel(x, c, ctx, c_ctx, ada_w, ada_b, norm_w, attn_w_in, attn_q_norm, attn_k_norm, mla_q_norm, mla_kv_norm, mla_w_uq, mla_w_ukv, attn_w_out, hy_w_in, hy_conv_w, hy_conv_b, hy_ffn_w1, hy_ffn_b1, hy_ffn_w2, hy_ffn_b2, hy_ffn_w3, hy_ffn_b3, hy_freq, hy_skip, hy_w_out, final_norm_w):
    b, l, d = x.shape
    lc = ctx.shape[1]
    tl = min(256, l)

    rows = -(-(b + 1) // 8) * 8
    cs = jnp.concatenate([c, c_ctx[None, :], jnp.zeros((rows - b - 1, d), F32)], axis=0)
    ada = _ada(cs, ada_w, ada_b)
    ada_lat = [ada[i, :b].reshape(b, 1, 3 * d) for i in range(ada.shape[0])]
    ada_ctx0 = jnp.broadcast_to(ada[0, b].reshape(1, 1, 3 * d), (b, 1, 3 * d))

    w_lat, w_kv = _pack_attn_w_in(attn_w_in[0])
    wuq, wukv = _pack_mla_up(mla_w_uq[0], mla_w_ukv[0])
    nw0 = norm_w[0].reshape(1, d)
    def norm_rows(wn, width):
        sw = _swap_cols(wn, GQA_HEAD_DIM // 4)
        return jnp.stack([jnp.tile(wn, width // GQA_HEAD_DIM), jnp.tile(sw, width // GQA_HEAD_DIM)])

    qn, kn = norm_rows(attn_q_norm[0], _QA_W), norm_rows(attn_k_norm[0], LANE)
    cqn = mla_q_norm[0].reshape(1, MLA_Q_RANK)
    ckvn = mla_kv_norm[0].reshape(1, MLA_KV_RANK)
    o2q, o2k = _head_ones2(_QA_W), _head_ones2(LANE)
    tabs = _rope_tables(l)
    qa, ka, va, qm, km, vm, sg = _prep(x, ada_lat[0], nw0, w_lat, qn, kn, cqn, ckvn, wuq, wukv, o2q, o2k, tabs,
                                       latent=True, tl=tl)
    kac, vac, kmc, vmc = _prep(ctx, ada_ctx0, nw0, w_kv, None, kn, None, ckvn, None, wukv, None, o2k, None,
                               latent=False, tl=min(tl, lc))
    tq, tk = min(2048, l), min(512, l // 2)
    oa = _attention(qa, kac, vac, ka, va, q_shared=True, kv_group=2, tq=tq, tk=tk)
    om = _attention(qm, kmc, vmc, km, vm, q_shared=False, kv_group=1, tq=tq, tk=tk)
    x1 = _out0(x, oa, om, sg, attn_w_out[0].astype(BF16), ada_lat[0], tl=tl)

    nw1 = norm_w[1].reshape(1, d)
    u = _hyin(x1, ada_lat[1], nw1, hy_w_in[0].astype(BF16), hy_conv_w[0], hy_conv_b[0].reshape(1, -1),
              tl=min(512, l))
    n1 = 2 * l // DFT_N2
    n1h = n1 // 2
    tabs_d = _dft_tables(l)

    t = jnp.linspace(0.0, 1.0, l, dtype=F32)[:, None]
    wpos = (2.0 * math.pi / l) * jnp.arange(l, dtype=F32)[:, None]
    bands = jnp.linspace(1e-4, HY_BANDS - 1, HY_BANDS, dtype=F32)
    emb = jnp.concatenate([t, jnp.cos(wpos * bands), -jnp.sin(wpos * bands)], axis=-1)
    deltas = jnp.abs(jnp.linspace(math.log(HY_DECAY_TARGET) / HY_SLOW_DECAY,
                                  math.log(HY_DECAY_TARGET) / HY_FAST_DECAY, d, dtype=F32)).reshape(1, d)
    wf = hy_ffn_w3.shape[-1]
    oc = HY_ORDER * d

    def by_direction(a):
        return a.reshape(-1, HY_ORDER, 2, d).transpose(0, 2, 1, 3).reshape(-1, wf)

    hw, asum = _filters(_pad2(emb, l, LANE), _pad2(hy_ffn_w1[0], LANE, LANE), _pad2(hy_ffn_b1[0][None], 1, LANE),
                        _pad2(hy_ffn_w2[0], LANE, LANE), _pad2(hy_ffn_b2[0][None], 1, LANE),
                        _pad2(by_direction(hy_ffn_w3[0]), LANE, wf), by_direction(hy_ffn_b3[0][None]),
                        _pad2(hy_freq[0][None], 1, LANE), deltas, tl=tl)
    l1 = asum[:, :oc] + asum[:, oc:]
    ct = min(1024, d)
    af = _outer_fwd(hw.reshape(n1h, DFT_N2, wf), real_input=True, nseq=2, tn2=32, ct=256)
    kf = _mid(af, tabs_d[0], scale=1.0 / (l1 * (2 * l)), ct=oc)

    v2, x1g, x2g = (_pair_view(u[i], n1h) for i in range(3))
    z = _long_convs(v2, [x1g, x2g], [hy_skip[0, o:o + 1] for o in range(HY_ORDER)], kf, tabs_d, ct=ct)
    z = z.reshape(b, l, d)
    return _out1(x1, z, u[3], hy_w_out[0].astype(BF16), ada_lat[1], final_norm_w.reshape(1, d), tl=tl)
```

```python
import functools
import math

import numpy as np
import jax
import jax.numpy as jnp
from jax import lax
from jax.experimental import pallas as pl
from jax.experimental.pallas import tpu as pltpu

EPS = 1e-6
GRID_W = 64
ROPE_BASE = 10000.0
GQA_HEADS, GQA_KV_HEADS, GQA_HEAD_DIM = 8, 2, 64
MLA_HEADS, MLA_Q_RANK, MLA_KV_RANK = 8, 256, 128
MLA_NOPE_DIM, MLA_ROPE_DIM, MLA_V_DIM = 64, 32, 64
HY_ORDER, HY_SHORT, HY_BANDS, HY_FFN = 2, 3, 16, 64
HY_FAST_DECAY, HY_SLOW_DECAY, HY_DECAY_TARGET = 0.3, 1.5, 1e-2
LANE = 128
DFT_N2 = 128
VMEM_LIMIT = 56 * 1024 * 1024
LOG2E = 1.4426950408889634
HI = lax.Precision.HIGHEST
F32 = jnp.float32
BF16 = jnp.bfloat16


def _cparams(sem):
    return pltpu.CompilerParams(dimension_semantics=sem, vmem_limit_bytes=VMEM_LIMIT)


def _per_chunk(fn, *arrs):
    width = arrs[0].shape[-1]
    outs = [fn(*[a[:, c:c + LANE] for a in arrs]) for c in range(0, width, LANE)]
    return outs[0] if len(outs) == 1 else jnp.concatenate(outs, axis=-1)


def _lane_iota(shape):
    return lax.broadcasted_iota(jnp.int32, shape, len(shape) - 1)


def _head_rsqrt(x, ones2):
    ss = x * x
    hi = ss.astype(BF16)
    lo = (ss - hi.astype(F32)).astype(BF16)
    tot = jnp.dot(jnp.concatenate([hi, lo], axis=-1), ones2, preferred_element_type=F32)
    return lax.rsqrt(tot * (1.0 / GQA_HEAD_DIM) + EPS)


def _rope(x, x_sw, cos, sin_signed):
    return _per_chunk(lambda c, w: c * cos + w * sin_signed, x, x_sw)


def _row_rms(x):
    return x * lax.rsqrt(jnp.mean(x * x, axis=-1, keepdims=True) + EPS)


def _silu(x):
    return x * (1.0 / (1.0 + jnp.exp(-x)))


def _ada_kernel(c_ref, w_ref, b_ref, o_ref):
    s = _silu(c_ref[...])
    o_ref[0] = jnp.dot(s, w_ref[0], precision=HI, preferred_element_type=F32) + b_ref[0]


def _ada(cs, ada_w, ada_b):
    depth, d, d3 = ada_w.shape
    rows = cs.shape[0]
    nt = d3 // d
    return pl.pallas_call(
        _ada_kernel,
        grid=(depth, nt),
        in_specs=[pl.BlockSpec((rows, d), lambda i, j: (0, 0)),
                  pl.BlockSpec((1, d, d), lambda i, j: (i, 0, j)),
                  pl.BlockSpec((1, 1, d), lambda i, j: (i, 0, j))],
        out_specs=pl.BlockSpec((1, rows, d), lambda i, j: (i, 0, j)),
        out_shape=jax.ShapeDtypeStruct((depth, rows, d3), F32),
        compiler_params=_cparams(("arbitrary", "arbitrary")),
        name="ada",
    )(cs, ada_w, ada_b.reshape(depth, 1, d3))


_QA_W, _KA_W, _VA_W, _KPE_W = 512, 512, 256, 128


def _prep_kernel(*refs, latent, d):
    if latent:
        (x_ref, ada_ref, nw_ref, w_ref, qn_ref, kn_ref, cqn_ref, ckvn_ref, wuq_ref, wukv_ref, o2q_ref, o2k_ref,
         ca_ref, sa_ref, cm_ref, sm_ref,
         qa_o, ka_o, va_o, qm_o, km_o, vm_o, sg_o) = refs
    else:
        (x_ref, ada_ref, nw_ref, w_ref, kn_ref, ckvn_ref, wukv_ref, o2k_ref,
         ka_o, va_o, km_o, vm_o) = refs
    ada = ada_ref[0]
    shift, scale = ada[:, :d], ada[:, d:2 * d]
    h = (_row_rms(x_ref[0]) * nw_ref[...]) * (1.0 + scale) + shift
    p = jnp.dot(h.astype(BF16), w_ref[...], preferred_element_type=F32)
    off = 0

    def take(width):
        nonlocal off
        off += width
        return p[:, off - width:off]

    if latent:
        qa, qa_sw = take(_QA_W), take(_QA_W)
        k, k_sw = take(LANE), take(LANE)
    else:
        k = take(LANE)
    v = take(LANE)
    if latent:
        cq = take(MLA_Q_RANK)
    ckv = take(MLA_KV_RANK)
    kpe = take(LANE)

    kn = kn_ref[...]
    rk = _head_rsqrt(k, o2k_ref[...])
    k = k * rk * kn[0:1]
    if latent:
        k = _rope(k, k_sw * rk * kn[1:2], ca_ref[...], sa_ref[...])
        kpe = _rope(kpe, take(LANE), cm_ref[...], sm_ref[...])
    low = _lane_iota(k.shape) < (LANE // 2)
    k_x = pltpu.roll(k, LANE // 2, axis=1)
    v_x = pltpu.roll(v, LANE // 2, axis=1)
    zero = jnp.zeros_like(k)
    ka = [jnp.where(low, k, zero), jnp.where(low, zero, k_x), jnp.where(low, k_x, zero), jnp.where(low, zero, k)]
    ka_o[0] = jnp.concatenate(ka, axis=-1).astype(BF16)
    one64 = (_lane_iota(k.shape) == LANE // 2).astype(F32)
    va = jnp.concatenate([jnp.where(low, v, one64), jnp.where(low, v_x, one64)], axis=-1)
    va_o[0] = va.T.astype(BF16)
    ckv_n = (_row_rms(ckv) * ckvn_ref[...]).astype(BF16)
    kv = jnp.dot(ckv_n, wukv_ref[...], preferred_element_type=F32)
    nk = MLA_HEADS * LANE
    km_o[0] = _per_chunk(lambda c: c + kpe, kv[:, :nk]).astype(BF16)
    vm_o[0] = _per_chunk(lambda c: c + one64, kv[:, nk:]).T.astype(BF16)
    if latent:
        qn = qn_ref[...]
        rq = _head_rsqrt(qa, o2q_ref[...])
        qa = _rope(qa * rq * qn[0:1], qa_sw * rq * qn[1:2], ca_ref[...], sa_ref[...])
        qa_o[0] = (qa * (GQA_HEAD_DIM ** -0.5 * LOG2E)).astype(BF16)
        cq_n = (_row_rms(cq) * cqn_ref[...]).astype(BF16)
        qm = jnp.dot(cq_n, wuq_ref[...], preferred_element_type=F32)
        qm = _rope(qm[:, :nk], qm[:, nk:], cm_ref[...], sm_ref[...])
        qm_o[0] = (qm * ((MLA_NOPE_DIM + MLA_ROPE_DIM) ** -0.5 * LOG2E)).astype(BF16)
        sg_o[0] = _silu(take(d)).astype(BF16)


def _prep(x, ada, nw, w, qn, kn, cqn, ckvn, wuq, wukv, o2q, o2k, tabs, *, latent, tl):
    b, l, d = x.shape
    grid = (l // tl, b)
    row = lambda i, j: (j, i, 0)
    const = lambda i, j: (0, 0)
    tab = lambda i, j: (i, 0)
    xspec = pl.BlockSpec((1, tl, d), row)
    adaspec = pl.BlockSpec((1, 1, ada.shape[-1]), lambda i, j: (j, 0, 0))

    def full(a):
        return pl.BlockSpec(a.shape, const)

    def out(width):
        return (pl.BlockSpec((1, tl, width), row), jax.ShapeDtypeStruct((b, l, width), BF16))

    def out_t(width):
        return (pl.BlockSpec((1, width, tl), lambda i, j: (j, 0, i)), jax.ShapeDtypeStruct((b, width, l), BF16))

    if latent:
        ca, sa, cm, sm = tabs
        ins = [x, ada, nw, w, qn, kn, cqn, ckvn, wuq, wukv, o2q, o2k, ca, sa, cm, sm]
        in_specs = [xspec, adaspec, full(nw), full(w), full(qn), full(kn), full(cqn), full(ckvn),
                    full(wuq), full(wukv), full(o2q), full(o2k)] + [pl.BlockSpec((tl, t.shape[1]), tab) for t in tabs]
        outs = [out(_QA_W), out(_KA_W), out_t(_VA_W), out(MLA_HEADS * LANE), out(MLA_HEADS * LANE),
                out_t(MLA_HEADS * LANE), out(d)]
    else:
        ins = [x, ada, nw, w, kn, ckvn, wukv, o2k]
        in_specs = [xspec, adaspec, full(nw), full(w), full(kn), full(ckvn), full(wukv), full(o2k)]
        outs = [out(_KA_W), out_t(_VA_W), out(MLA_HEADS * LANE), out_t(MLA_HEADS * LANE)]
    return pl.pallas_call(
        functools.partial(_prep_kernel, latent=latent, d=d),
        grid=grid, in_specs=in_specs,
        out_specs=[o[0] for o in outs], out_shape=[o[1] for o in outs],
        compiler_params=_cparams(("arbitrary", "arbitrary")),
        name="prep_lat" if latent else "prep_ctx",
    )(*ins)


_ATT_SUB = 256
_ATT_VROWS = 80


def _attn_kernel(q_ref, kc_ref, vc_ref, kl_ref, vl_ref, o_ref, s0, s1, p0, p1, acc_ref, *, tk, q_shared):
    tq = q_ref.shape[1]
    lk, lc = kl_ref.shape[1], kc_ref.shape[1]
    nblk = lk // tk
    t = _ATT_SUB
    dn = (((1,), (1,)), ((), ()))
    lanes = lambda e: slice(e * LANE, (e + 1) * LANE)

    def q_sub(e, qc):
        return q_ref[0, qc * t:(qc + 1) * t, lanes(0 if q_shared else e)]

    def vrows(v_ref, e, cols):
        base = 0 if q_shared else e * LANE
        return v_ref[0, base:base + _ATT_VROWS, cols]

    def mxu_phase(s_next, k_next, p_prev, v_prev, nk_prev, alpha):
        smax = [[], []]
        for qc in range(tq // t):
            cols = slice(qc * t, (qc + 1) * t)
            for e in range(2):
                cmax = None
                for kr in range(tk // t if s_next is not None else 0):
                    rows = slice(kr * t, (kr + 1) * t)
                    s_tile = lax.dot_general(k_next(e, kr), q_sub(e, qc), dn, preferred_element_type=F32)
                    s_next[e, rows, cols] = s_tile
                    tmax = jnp.max(s_tile, axis=0, keepdims=True)
                    cmax = tmax if cmax is None else jnp.maximum(cmax, tmax)
                smax[e].append(cmax)
                if p_prev is not None:
                    pv = jnp.dot(v_prev(e), p_prev(e, slice(0, nk_prev * t), cols), preferred_element_type=F32)
                    acc_ref[e, :, cols] = alpha[e][:, cols] * acc_ref[e, :, cols] + pv
        return None if s_next is None else [jnp.concatenate(c, axis=-1) for c in smax]

    def softmax_phase(s_cur, smax, p_cur, m):
        m_new, alpha, ps = [], [], []
        for e in range(2):
            s = s_cur[e]
            mn = jnp.maximum(m[e], smax[e])
            p = jnp.exp2(s - mn).astype(BF16)
            if p_cur is not None:
                p_cur[e] = p
            m_new.append(mn)
            alpha.append(jnp.exp2(m[e] - mn))
            ps.append(p)
        return m_new, alpha, ps

    def from_ref(ref):
        return lambda e, rows, cols: ref[e, rows, cols]

    def k_lat(r):
        return lambda e, kr: kl_ref[0, pl.ds(pl.multiple_of(r + kr * t, t), t), lanes(e)]

    def v_lat(r):
        return lambda e: vrows(vl_ref, e, pl.ds(pl.multiple_of(r, t), tk))

    acc_ref[...] = jnp.zeros_like(acc_ref)
    ones = [jnp.ones((1, tq), F32)] * 2
    m = [jnp.full((1, tq), -1e30, F32)] * 2

    s_ctx = [lax.dot_general(kc_ref[0, :, lanes(e)], q_ref[0, :, lanes(0 if q_shared else e)], dn,
                             preferred_element_type=F32) for e in range(2)]
    x0 = mxu_phase(s0, k_lat(0), None, None, 0, None)
    m, _, p_ctx = softmax_phase(s_ctx, [jnp.max(s, axis=0, keepdims=True) for s in s_ctx], None, m)
    x1 = mxu_phase(s1, k_lat(tk), lambda e, rows, cols: p_ctx[e][rows, cols],
                   lambda e: vrows(vc_ref, e, slice(0, lc)), lc // t, ones)
    m, alpha, _ = softmax_phase(s0, x0, p0, m)

    def body(j, carry):
        m, alpha, x1 = (list(c) for c in carry)
        r = pl.multiple_of(2 * j * tk, tk)
        x0 = mxu_phase(s0, k_lat(r + 2 * tk), from_ref(p0), v_lat(r), tk // t, alpha)
        m, alpha, _ = softmax_phase(s1, x1, p1, m)
        x1 = mxu_phase(s1, k_lat(r + 3 * tk), from_ref(p1), v_lat(r + tk), tk // t, alpha)
        m, alpha, _ = softmax_phase(s0, x0, p0, m)
        return tuple(m), tuple(alpha), tuple(x1)

    m, alpha, x1 = lax.fori_loop(0, (nblk - 2) // 2, body, (tuple(m), tuple(alpha), tuple(x1)))
    mxu_phase(None, None, from_ref(p0), v_lat((nblk - 2) * tk), tk // t, alpha)
    m, alpha, _ = softmax_phase(s1, list(x1), p1, list(m))
    mxu_phase(None, None, from_ref(p1), v_lat((nblk - 1) * tk), tk // t, alpha)
    dv = LANE // 2
    outs = [acc_ref[e, :dv, :] / acc_ref[e, dv:dv + 1, :] for e in range(2)]
    o_ref[0] = jnp.concatenate(outs, axis=0).T.astype(o_ref.dtype)


def _attention(q, kc, vc, kl, vl, *, q_shared, kv_group, tq, tk):
    b, l, _ = q.shape
    wq = LANE if q_shared else 2 * LANE
    pairs = q.shape[-1] // wq
    lc = kc.shape[1]
    assert (l // tk) % 2 == 0, "key blocks are consumed two per loop trip"
    vw = LANE if q_shared else 2 * LANE
    kv = lambda bi, j, i: (bi, 0, j // kv_group)
    vt = lambda bi, j, i: (bi, j // kv_group, 0)
    return pl.pallas_call(
        functools.partial(_attn_kernel, tk=tk, q_shared=q_shared),
        grid=(b, pairs, l // tq),
        in_specs=[pl.BlockSpec((1, tq, wq), lambda bi, j, i: (bi, i, j)),
                  pl.BlockSpec((1, lc, 2 * LANE), kv), pl.BlockSpec((1, vw, lc), vt),
                  pl.BlockSpec((1, l, 2 * LANE), kv), pl.BlockSpec((1, vw, l), vt)],
        out_specs=pl.BlockSpec((1, tq, LANE), lambda bi, j, i: (bi, i, j)),
        out_shape=jax.ShapeDtypeStruct((b, l, pairs * LANE), BF16),
        scratch_shapes=[pltpu.VMEM((2, tk, tq), F32), pltpu.VMEM((2, tk, tq), F32),
                        pltpu.VMEM((2, tk, tq), BF16), pltpu.VMEM((2, tk, tq), BF16),
                        pltpu.VMEM((2, _ATT_VROWS, tq), F32)],
        compiler_params=_cparams(("arbitrary", "arbitrary", "arbitrary")),
        name="attn_gqa" if q_shared else "attn_mla",
    )(q, kc, vc, kl, vl)


def _out0_kernel(x_ref, oa_ref, om_ref, sg_ref, w_ref, ada_ref, o_ref, *, d):
    o = jnp.concatenate([oa_ref[0], om_ref[0]], axis=-1).astype(F32) * sg_ref[0].astype(F32)
    y = jnp.dot(o.astype(BF16), w_ref[...], preferred_element_type=F32)
    o_ref[0] = x_ref[0] + ada_ref[0][:, 2 * d:3 * d] * y


def _out0(x, oa, om, sg, w, ada, *, tl):
    b, l, d = x.shape
    row = lambda bi, i: (bi, i, 0)
    return pl.pallas_call(
        functools.partial(_out0_kernel, d=d),
        grid=(b, l // tl),
        in_specs=[pl.BlockSpec((1, tl, d), row), pl.BlockSpec((1, tl, oa.shape[-1]), row),
                  pl.BlockSpec((1, tl, om.shape[-1]), row), pl.BlockSpec((1, tl, d), row),
                  pl.BlockSpec(w.shape, lambda bi, i: (0, 0)),
                  pl.BlockSpec((1, 1, ada.shape[-1]), lambda bi, i: (bi, 0, 0))],
        out_specs=pl.BlockSpec((1, tl, d), row),
        out_shape=jax.ShapeDtypeStruct((b, l, d), F32),
        compiler_params=_cparams(("arbitrary", "arbitrary")),
        name="out0",
    )(x, oa, om, sg, w, ada)


_HALO = 8
_HY_STORE = BF16


def _hyin_kernel(x_ref, xp_ref, xn_ref, ada_ref, nw_ref, w_ref, cw_ref, cb_ref, *o_refs, d):
    i = pl.program_id(1)
    tl = x_ref.shape[1]
    n_conv = len(o_refs) - 1
    ada = ada_ref[0]
    shift, scale = ada[:, :d], ada[:, d:2 * d]

    def mod(x):
        return (_row_rms(x) * nw_ref[...]) * (1.0 + scale) + shift

    hp = mod(xp_ref[0]) * (i > 0).astype(F32)
    hn = mod(xn_ref[0]) * (i < pl.num_programs(1) - 1).astype(F32)
    h = jnp.concatenate([mod(x_ref[0]), hp, hn], axis=0).astype(BF16)
    rows = lax.broadcasted_iota(jnp.int32, (tl, d), 0)
    cw = cw_ref[...]
    for n in range(n_conv + 1):
        p = jnp.dot(h, w_ref[:, n * d:(n + 1) * d], preferred_element_type=F32)
        pm = p[0:tl]
        if n < n_conv:
            prev = jnp.where(rows == 0, p[tl + _HALO - 1:tl + _HALO], pltpu.roll(pm, 1, axis=0))
            nxt = jnp.where(rows == tl - 1, p[tl + _HALO:tl + _HALO + 1], pltpu.roll(pm, tl - 1, axis=0))
            c0, c1, c2 = (cw[j:j + 1, n * d:(n + 1) * d] for j in range(HY_SHORT))
            out = prev * c0 + pm * c1 + nxt * c2 + cb_ref[:, n * d:(n + 1) * d]
        else:
            out = _silu(pm)
        o_refs[n][0] = out.astype(o_refs[n].dtype)


def _hyin(x, ada, nw, w, cw, cb, *, tl):
    b, l, d = x.shape
    ng = w.shape[1] // d
    tb = tl // _HALO
    nb = l // _HALO
    const = lambda bi, i: (0, 0)
    blk = pl.BlockSpec((1, tl, d), lambda bi, i: (bi, i, 0))
    return pl.pallas_call(
        functools.partial(_hyin_kernel, d=d),
        grid=(b, l // tl),
        in_specs=[blk,
                  pl.BlockSpec((1, _HALO, d), lambda bi, i: (bi, jnp.maximum(i * tb - 1, 0), 0)),
                  pl.BlockSpec((1, _HALO, d), lambda bi, i: (bi, jnp.minimum((i + 1) * tb, nb - 1), 0)),
                  pl.BlockSpec((1, 1, ada.shape[-1]), lambda bi, i: (bi, 0, 0)),
                  pl.BlockSpec(nw.shape, const), pl.BlockSpec(w.shape, const),
                  pl.BlockSpec(cw.shape, const), pl.BlockSpec(cb.shape, const)],
        out_specs=[blk] * ng,
        out_shape=[jax.ShapeDtypeStruct((b, l, d), _HY_STORE)] * ng,
        compiler_params=_cparams(("arbitrary", "arbitrary")),
        name="hy_in",
    )(x, x, x, ada, nw, w, cw, cb)


def _filt_kernel(emb_ref, w1_ref, b1_ref, w2_ref, b2_ref, w3_ref, b3_ref, fr_ref, dl_ref, h_o, s_o, *, reps):
    emb = emb_ref[...]
    fr = fr_ref[...]
    hid = jnp.sin(fr * (jnp.dot(emb, w1_ref[...], precision=HI, preferred_element_type=F32) + b1_ref[...]))
    hid = jnp.sin(fr * (jnp.dot(hid, w2_ref[...], precision=HI, preferred_element_type=F32) + b2_ref[...]))
    h = jnp.dot(hid, w3_ref[...], precision=HI, preferred_element_type=F32) + b3_ref[...]
    win = jnp.exp(-emb[:, 0:1] * dl_ref[...])
    hw = h * jnp.concatenate([win] * reps, axis=-1)
    rows = lax.broadcasted_iota(jnp.int32, hw.shape, 0) + pl.program_id(0) * hw.shape[0]
    hw = jnp.where((rows == 0) & (_lane_iota(hw.shape) >= hw.shape[1] // 2), 0.0, hw)
    h_o[...] = hw

    @pl.when(pl.program_id(0) == 0)
    def _():
        s_o[...] = jnp.zeros_like(s_o)

    s_o[...] += jnp.sum(jnp.abs(hw), axis=0, keepdims=True)


def _filters(emb, w1, b1, w2, b2, w3, b3, fr, dl, *, tl):
    l = emb.shape[0]
    wo = w3.shape[1]
    const = lambda i: (0, 0)
    full = lambda a: pl.BlockSpec(a.shape, const)
    return pl.pallas_call(
        functools.partial(_filt_kernel, reps=wo // dl.shape[1]),
        grid=(l // tl,),
        in_specs=[pl.BlockSpec((tl, emb.shape[1]), lambda i: (i, 0)), full(w1), full(b1), full(w2), full(b2),
                  full(w3), full(b3), full(fr), full(dl)],
        out_specs=[pl.BlockSpec((tl, wo), lambda i: (i, 0)), pl.BlockSpec((1, wo), const)],
        out_shape=[jax.ShapeDtypeStruct((l, wo), F32), jax.ShapeDtypeStruct((1, wo), F32)],
        compiler_params=_cparams(("arbitrary",)),
        name="hy_filter",
    )(emb, w1, b1, w2, b2, w3, b3, fr, dl)


def _cmul_const(a, ang):
    ar, ai = a
    q = ang / (0.5 * math.pi)
    if abs(q - round(q)) < 1e-12:
        return [(ar, ai), (-ai, ar), (-ar, -ai), (ai, -ar)][int(round(q)) % 4]
    c, s = math.cos(ang), math.sin(ang)
    return (ar * c - ai * s, ar * s + ai * c)


def _fft_dif(x):
    x = list(x)
    n = len(x)
    half = n // 2
    while half >= 1:
        for base in range(0, n, 2 * half):
            for j in range(half):
                a, b = x[base + j], x[base + j + half]
                ang = -math.pi * j / half
                if b is None:
                    x[base + j + half] = None if a is None else _cmul_const(a, ang)
                else:
                    x[base + j] = (a[0] + b[0], a[1] + b[1])
                    x[base + j + half] = _cmul_const((a[0] - b[0], a[1] - b[1]), ang)
        half //= 2
    return x


def _ifft_dit(x, keep):
    x = list(x)
    n = len(x)
    half = 1
    while half <= n // 2:
        last = half == n // 2
        for base in range(0, n, 2 * half):
            for j in range(half):
                a = x[base + j]
                b = _cmul_const(x[base + j + half], math.pi * j / half)
                x[base + j] = (a[0] + b[0], a[1] + b[1])
                if not last or base + j + half < keep:
                    x[base + j + half] = (a[0] - b[0], a[1] - b[1])
        half *= 2
    return x[:keep]


def _outer_fwd_kernel(u_ref, o_ref, *, real_input):
    n1 = o_ref.shape[2]
    if real_input:
        x = [(u_ref[i], jnp.zeros_like(u_ref[i])) for i in range(n1 // 2)]
    else:
        x = [(u_ref[0, 0, i].astype(F32), u_ref[0, 1, i].astype(F32)) for i in range(n1 // 2)]
    for s, (re, im) in enumerate(_fft_dif(x + [None] * (n1 // 2))):
        o_ref[0, 0, s] = re.astype(o_ref.dtype)
        o_ref[0, 1, s] = im.astype(o_ref.dtype)


def _outer_fwd(u, *, real_input, tn2, ct, nseq=1):
    n2 = u.shape[-2]
    if real_input:
        p, c, n1 = nseq, u.shape[-1] // nseq, 2 * u.shape[0]
        in_spec = pl.BlockSpec((n1 // 2, tn2, ct), lambda pi, r, j: (0, r, pi * (c // ct) + j))
    else:
        p, c, n1 = u.shape[0], u.shape[-1], 2 * u.shape[2]
        in_spec = pl.BlockSpec((1, 2, n1 // 2, tn2, ct), lambda pi, r, j: (pi, 0, 0, r, j))
    return pl.pallas_call(
        functools.partial(_outer_fwd_kernel, real_input=real_input),
        grid=(p, n2 // tn2, c // ct), in_specs=[in_spec],
        out_specs=pl.BlockSpec((1, 2, n1, tn2, ct), lambda pi, r, j: (pi, 0, 0, r, j)),
        out_shape=jax.ShapeDtypeStruct((p, 2, n1, n2, c), u.dtype),
        compiler_params=_cparams(("arbitrary", "arbitrary", "arbitrary")),
        name="hy_outer_filt" if real_input else "hy_outer_fwd",
    )(u)


def _outer_inv_kernel(a_ref, u_ref, g_ref, sk_ref, o_ref, *next_ref):
    n1 = a_ref.shape[2]
    f = lambda ref, r, i: ref[0, r, i].astype(F32)
    y = _ifft_dit([(f(a_ref, 0, s), f(a_ref, 1, s)) for s in range(n1)], n1 // 2)
    sk = sk_ref[...]
    z = [(f(g_ref, 0, i) * (re + f(u_ref, 0, i) * sk), f(g_ref, 1, i) * (im + f(u_ref, 1, i) * sk))
         for i, (re, im) in enumerate(y)]
    for i, (re, im) in enumerate(z):
        o_ref[0, 0, i] = re.astype(o_ref.dtype)
        o_ref[0, 1, i] = im.astype(o_ref.dtype)
    if next_ref:
        nxt = next_ref[0]
        for s, (re, im) in enumerate(_fft_dif(z + [None] * (n1 // 2))):
            nxt[0, 0, s] = re.astype(nxt.dtype)
            nxt[0, 1, s] = im.astype(nxt.dtype)


def _outer_inv(a, u, gate, skip, *, tn2, ct, with_next):
    p, _, n1, n2, c = a.shape
    blk = lambda rows: pl.BlockSpec((1, 2, rows, tn2, ct), lambda pi, r, j: (pi, 0, 0, r, j))
    out_specs, out_shape = [blk(n1 // 2)], [jax.ShapeDtypeStruct(u.shape, u.dtype)]
    if with_next:
        out_specs.append(blk(n1))
        out_shape.append(jax.ShapeDtypeStruct(a.shape, a.dtype))
    return pl.pallas_call(
        _outer_inv_kernel,
        grid=(p, n2 // tn2, c // ct),
        in_specs=[blk(n1), blk(n1 // 2), blk(n1 // 2), pl.BlockSpec((1, ct), lambda pi, r, j: (0, j))],
        out_specs=out_specs, out_shape=out_shape,
        compiler_params=_cparams(("arbitrary", "arbitrary", "arbitrary")),
        name="hy_outer_inv_fwd" if with_next else "hy_outer_inv",
    )(a, u, gate, skip)


def _dot3(m3, x):
    hi = x.astype(BF16)
    lo = (x - hi.astype(F32)).astype(BF16)
    return jnp.dot(m3, jnp.concatenate([hi, hi, lo], axis=0), preferred_element_type=F32)


def _mid_kernel(a_ref, g_ref, *rest, spectrum):
    np_, n2, ct = a_ref.shape[0], a_ref.shape[3], a_ref.shape[4]
    if spectrum:
        mm = _dot3
    else:
        mm = lambda m, x: jnp.dot(m, x.astype(BF16), preferred_element_type=F32)
    xs = [mm(g_ref[0], a_ref[p, :, 0].reshape(2 * n2, ct)) for p in range(np_)]
    if spectrum:
        sc_ref, o_ref = rest
        f, bw = xs
        sc = sc_ref[...]
        o_ref[0, 0, 0] = (f[:n2] + bw[:n2]) * sc
        o_ref[0, 1, 0] = (f[n2:] - bw[n2:]) * sc
    else:
        gt_ref, kf_ref, o_ref = rest
        kr, ki = kf_ref[0, 0, 0], kf_ref[0, 1, 0]
        ys = [jnp.concatenate([x[:n2] * kr - x[n2:] * ki, x[:n2] * ki + x[n2:] * kr], axis=0) for x in xs]
        for p in range(np_):
            o_ref[p, :, 0] = mm(gt_ref[0], ys[p]).reshape(2, n2, ct).astype(o_ref.dtype)


def _mid(a, g, gt=None, kf=None, scale=None, *, ct, kf_col0=0):
    p, _, n1, n2, c = a.shape
    ablk = pl.BlockSpec((p, 2, 1, n2, ct), lambda k, j: (0, 0, k, 0, j))
    gblk = pl.BlockSpec((1,) + g.shape[1:], lambda k, j: (k, 0, 0))
    oblk, oshape = ablk, a.shape
    if kf is None:
        ins = [a, g, scale]
        in_specs = [ablk, gblk, pl.BlockSpec((1, ct), lambda k, j: (0, j))]
        oblk, oshape = pl.BlockSpec((1, 2, 1, n2, ct), lambda k, j: (0, 0, k, 0, j)), (1,) + a.shape[1:]
    else:
        ins = [a, g, gt, kf]
        in_specs = [ablk, gblk, gblk,
                    pl.BlockSpec((1, 2, 1, n2, ct), lambda k, j: (0, 0, k, 0, kf_col0 + j))]
    return pl.pallas_call(
        functools.partial(_mid_kernel, spectrum=kf is None),
        grid=(n1, c // ct), in_specs=in_specs, out_specs=oblk,
        out_shape=jax.ShapeDtypeStruct(oshape, a.dtype),
        compiler_params=_cparams(("arbitrary", "arbitrary")),
        name="hy_spectrum" if kf is None else "hy_mid",
    )(*ins)


def _dft_tables(l):
    n = 2 * l
    n2 = DFT_N2
    n1 = n // n2
    bits = n1.bit_length() - 1
    k1 = np.array([int(format(s, "0%db" % bits)[::-1], 2) for s in range(n1)], dtype=np.float64)
    kk = k1[:, None, None] + n1 * np.arange(n2, dtype=np.float64)[None, :, None]
    th = 2.0 * np.pi * kk * np.arange(n2, dtype=np.float64)[None, None, :] / n
    c, s = np.cos(th), np.sin(th)
    g = np.concatenate([np.concatenate([c, s], axis=2), np.concatenate([-s, c], axis=2)], axis=1)

    def split3(m):
        m = jnp.asarray(m, F32)
        hi = m.astype(BF16)
        lo = (m - hi.astype(F32)).astype(BF16)
        return jnp.concatenate([hi, lo, hi], axis=2)

    return split3(g), jnp.asarray(g, BF16), jnp.asarray(np.transpose(g, (0, 2, 1)), BF16)


def _pair_view(u, n1h):
    b, l, c = u.shape
    return u.reshape(b // 2, 2, n1h, l // n1h, c)


def _long_convs(u, gates, skips, kf, tabs, *, ct):
    _, g, gt = tabs
    c = u.shape[-1]
    a = _outer_fwd(u, real_input=False, tn2=32, ct=256)
    z = u
    for o, (gate, skip) in enumerate(zip(gates, skips)):
        a = _mid(a, g, gt, kf, ct=ct, kf_col0=o * (c // ct))
        last = o == len(gates) - 1
        res = _outer_inv(a, z, gate, skip, tn2=32, ct=256, with_next=not last)
        z, a = (res[0], None) if last else res
    return z


def _out1_kernel(x_ref, z_ref, sg_ref, w_ref, ada_ref, fw_ref, o_ref, *, d):
    y = jnp.dot((z_ref[0].astype(F32) * sg_ref[0].astype(F32)).astype(BF16), w_ref[...],
                preferred_element_type=F32)
    x = x_ref[0] + ada_ref[0][:, 2 * d:3 * d] * y
    o_ref[0] = _row_rms(x) * fw_ref[...]


def _out1(x, z, sg, w, ada, fw, *, tl):
    b, l, d = x.shape
    row = lambda bi, i: (bi, i, 0)
    blk = pl.BlockSpec((1, tl, d), row)
    return pl.pallas_call(
        functools.partial(_out1_kernel, d=d),
        grid=(b, l // tl),
        in_specs=[blk, blk, blk, pl.BlockSpec(w.shape, lambda bi, i: (0, 0)),
                  pl.BlockSpec((1, 1, ada.shape[-1]), lambda bi, i: (bi, 0, 0)),
                  pl.BlockSpec(fw.shape, lambda bi, i: (0, 0))],
        out_specs=blk,
        out_shape=jax.ShapeDtypeStruct((b, l, d), F32),
        compiler_params=_cparams(("arbitrary", "arbitrary")),
        name="out1",
    )(x, z, sg, w, ada, fw)


def _swap_cols(w, q):
    return w[..., np.arange(w.shape[-1]) ^ q]


def _pack_attn_w_in(w):
    d = w.shape[0]
    o = 0
    wq = w[:, o:o + 512]; o += 512
    wk = w[:, o:o + 128]; o += 128
    wv = w[:, o:o + 128]; o += 128
    wcq = w[:, o:o + MLA_Q_RANK]; o += MLA_Q_RANK
    wckv = w[:, o:o + MLA_KV_RANK]; o += MLA_KV_RANK
    wkpe = w[:, o:o + MLA_ROPE_DIM]; o += MLA_ROPE_DIM
    wg = w[:, o:]
    qa, qm = GQA_HEAD_DIM // 4, MLA_ROPE_DIM // 4

    def pe_chunk(wp):
        return jnp.concatenate([jnp.zeros((d, 64), w.dtype), wp, jnp.zeros((d, 32), w.dtype)], axis=1)

    kpe, kpe_sw = pe_chunk(wkpe), pe_chunk(_swap_cols(wkpe, qm))
    kv_part = jnp.concatenate([wk, wv, wckv, kpe], axis=1)
    lat = jnp.concatenate([wq, _swap_cols(wq, qa), wk, _swap_cols(wk, qa), wv, wcq, wckv, kpe, kpe_sw, wg], axis=1)
    return lat.astype(BF16), kv_part.astype(BF16)


def _pack_mla_up(w_uq, w_ukv):
    dq = MLA_NOPE_DIM + MLA_ROPE_DIM
    r = w_uq.shape[0]
    z = lambda n: jnp.zeros((r, n), w_uq.dtype)
    uq, uq_sw = [], []
    for h in range(MLA_HEADS):
        nope, pe = w_uq[:, dq * h:dq * h + MLA_NOPE_DIM], w_uq[:, dq * h + MLA_NOPE_DIM:dq * (h + 1)]
        uq += [nope, pe, z(LANE - dq)]
        uq_sw += [z(MLA_NOPE_DIM), _swap_cols(pe, MLA_ROPE_DIM // 4), z(LANE - dq)]
    dkv = MLA_NOPE_DIM + MLA_V_DIM
    kn = jnp.concatenate(
        [jnp.concatenate([w_ukv[:, dkv * h:dkv * h + MLA_NOPE_DIM],
                          jnp.zeros((w_ukv.shape[0], LANE - MLA_NOPE_DIM), w_ukv.dtype)], axis=1)
         for h in range(MLA_HEADS)], axis=1)
    vm = jnp.concatenate(
        [jnp.concatenate([w_ukv[:, dkv * h + MLA_NOPE_DIM:dkv * (h + 1)],
                          jnp.zeros((w_ukv.shape[0], LANE - MLA_V_DIM), w_ukv.dtype)], axis=1)
         for h in range(MLA_HEADS)], axis=1)
    return jnp.concatenate(uq + uq_sw, axis=1).astype(BF16), jnp.concatenate([kn, vm], axis=1).astype(BF16)


def _head_ones2(width):
    i = np.arange(width) // GQA_HEAD_DIM
    blk = (i[:, None] == i[None, :]).astype(np.float32)
    return jnp.asarray(np.concatenate([blk, blk], axis=0), BF16)


def _rope_tables(l):
    rows = (jnp.arange(l, dtype=jnp.int32) // GRID_W).astype(F32)[:, None]
    cols = (jnp.arange(l, dtype=jnp.int32) % GRID_W).astype(F32)[:, None]

    def tab(rot_dim):
        q = rot_dim // 4
        inv = ROPE_BASE ** (-jnp.arange(q, dtype=F32) / q)
        ar, ac = rows * inv, cols * inv
        cos = jnp.concatenate([jnp.cos(ar)] * 2 + [jnp.cos(ac)] * 2, axis=1)
        sin = jnp.concatenate([-jnp.sin(ar), jnp.sin(ar), -jnp.sin(ac), jnp.sin(ac)], axis=1)
        return cos, sin

    ca, sa = tab(GQA_HEAD_DIM)
    ca = jnp.concatenate([ca] * (LANE // GQA_HEAD_DIM), axis=1)
    sa = jnp.concatenate([sa] * (LANE // GQA_HEAD_DIM), axis=1)
    cm, sm = tab(MLA_ROPE_DIM)
    one, zero = jnp.ones((l, 1), F32), jnp.zeros((l, 1), F32)
    cm = jnp.concatenate([jnp.tile(one, (1, 64)), cm, jnp.tile(one, (1, 32))], axis=1)
    sm = jnp.concatenate([jnp.tile(zero, (1, 64)), sm, jnp.tile(zero, (1, 32))], axis=1)
    return ca, sa, cm, sm


def _pad2(a, r, c):
    return jnp.pad(a, ((0, r - a.shape[0]), (0, c - a.shape[1])))


def kernel(x, c, ctx, c_ctx, ada_w, ada_b, norm_w, attn_w_in, attn_q_norm, attn_k_norm, mla_q_norm, mla_kv_norm, mla_w_uq, mla_w_ukv, attn_w_out, hy_w_in, hy_conv_w, hy_conv_b, hy_ffn_w1, hy_ffn_b1, hy_ffn_w2, hy_ffn_b2, hy_ffn_w3, hy_ffn_b3, hy_freq, hy_skip, hy_w_out, final_norm_w):
    b, l, d = x.shape
    lc = ctx.shape[1]
    tl = min(256, l)

    rows = -(-(b + 1) // 8) * 8
    cs = jnp.concatenate([c, c_ctx[None, :], jnp.zeros((rows - b - 1, d), F32)], axis=0)
    ada = _ada(cs, ada_w, ada_b)
    ada_lat = [ada[i, :b].reshape(b, 1, 3 * d) for i in range(ada.shape[0])]
    ada_ctx0 = jnp.broadcast_to(ada[0, b].reshape(1, 1, 3 * d), (b, 1, 3 * d))

    w_lat, w_kv = _pack_attn_w_in(attn_w_in[0])
    wuq, wukv = _pack_mla_up(mla_w_uq[0], mla_w_ukv[0])
    nw0 = norm_w[0].reshape(1, d)
    def norm_rows(wn, width):
        sw = _swap_cols(wn, GQA_HEAD_DIM // 4)
        return jnp.stack([jnp.tile(wn, width // GQA_HEAD_DIM), jnp.tile(sw, width // GQA_HEAD_DIM)])

    qn, kn = norm_rows(attn_q_norm[0], _QA_W), norm_rows(attn_k_norm[0], LANE)
    cqn = mla_q_norm[0].reshape(1, MLA_Q_RANK)
    ckvn = mla_kv_norm[0].reshape(1, MLA_KV_RANK)
    o2q, o2k = _head_ones2(_QA_W), _head_ones2(LANE)
    tabs = _rope_tables(l)
    qa, ka, va, qm, km, vm, sg = _prep(x, ada_lat[0], nw0, w_lat, qn, kn, cqn, ckvn, wuq, wukv, o2q, o2k, tabs,
                                       latent=True, tl=tl)
    kac, vac, kmc, vmc = _prep(ctx, ada_ctx0, nw0, w_kv, None, kn, None, ckvn, None, wukv, None, o2k, None,
                               latent=False, tl=min(tl, lc))
    tq, tk = min(2048, l), min(512, l // 2)
    oa = _attention(qa, kac, vac, ka, va, q_shared=True, kv_group=2, tq=tq, tk=tk)
    om = _attention(qm, kmc, vmc, km, vm, q_shared=False, kv_group=1, tq=tq, tk=tk)
    x1 = _out0(x, oa, om, sg, attn_w_out[0].astype(BF16), ada_lat[0], tl=tl)

    nw1 = norm_w[1].reshape(1, d)
    u = _hyin(x1, ada_lat[1], nw1, hy_w_in[0].astype(BF16), hy_conv_w[0], hy_conv_b[0].reshape(1, -1),
              tl=min(512, l))
    n1 = 2 * l // DFT_N2
    n1h = n1 // 2
    tabs_d = _dft_tables(l)

    t = jnp.linspace(0.0, 1.0, l, dtype=F32)[:, None]
    wpos = (2.0 * math.pi / l) * jnp.arange(l, dtype=F32)[:, None]
    bands = jnp.linspace(1e-4, HY_BANDS - 1, HY_BANDS, dtype=F32)
    emb = jnp.concatenate([t, jnp.cos(wpos * bands), -jnp.sin(wpos * bands)], axis=-1)
    deltas = jnp.abs(jnp.linspace(math.log(HY_DECAY_TARGET) / HY_SLOW_DECAY,
                                  math.log(HY_DECAY_TARGET) / HY_FAST_DECAY, d, dtype=F32)).reshape(1, d)
    wf = hy_ffn_w3.shape[-1]
    oc = HY_ORDER * d

    def by_direction(a):
        return a.reshape(-1, HY_ORDER, 2, d).transpose(0, 2, 1, 3).reshape(-1, wf)

    hw, asum = _filters(_pad2(emb, l, LANE), _pad2(hy_ffn_w1[0], LANE, LANE), _pad2(hy_ffn_b1[0][None], 1, LANE),
                        _pad2(hy_ffn_w2[0], LANE, LANE), _pad2(hy_ffn_b2[0][None], 1, LANE),
                        _pad2(by_direction(hy_ffn_w3[0]), LANE, wf), by_direction(hy_ffn_b3[0][None]),
                        _pad2(hy_freq[0][None], 1, LANE), deltas, tl=tl)
    l1 = asum[:, :oc] + asum[:, oc:]
    ct = min(1024, d)
    af = _outer_fwd(hw.reshape(n1h, DFT_N2, wf), real_input=True, nseq=2, tn2=32, ct=256)
    kf = _mid(af, tabs_d[0], scale=1.0 / (l1 * (2 * l)), ct=oc)

    v2, x1g, x2g = (_pair_view(u[i], n1h) for i in range(3))
    z = _long_convs(v2, [x1g, x2g], [hy_skip[0, o:o + 1] for o in range(HY_ORDER)], kf, tabs_d, ct=ct)
    z = z.reshape(b, l, d)
    return _out1(x1, z, u[3], hy_w_out[0].astype(BF16), ada_lat[1], final_norm_w.reshape(1, d), tl=tl)
```

```python
import functools
import math

import numpy as np
import jax
import jax.numpy as jnp
from jax import lax
from jax.experimental import pallas as pl
from jax.experimental.pallas import tpu as pltpu

EPS = 1e-6
GRID_W = 64
ROPE_BASE = 10000.0
GQA_HEADS, GQA_KV_HEADS, GQA_HEAD_DIM = 8, 2, 64
MLA_HEADS, MLA_Q_RANK, MLA_KV_RANK = 8, 256, 128
MLA_NOPE_DIM, MLA_ROPE_DIM, MLA_V_DIM = 64, 32, 64
HY_ORDER, HY_SHORT, HY_BANDS, HY_FFN = 2, 3, 16, 64
HY_FAST_DECAY, HY_SLOW_DECAY, HY_DECAY_TARGET = 0.3, 1.5, 1e-2
LANE = 128
DFT_N2 = 128
VMEM_LIMIT = 56 * 1024 * 1024
LOG2E = 1.4426950408889634
HI = lax.Precision.HIGHEST
F32 = jnp.float32
BF16 = jnp.bfloat16


def _cparams(sem):
    return pltpu.CompilerParams(dimension_semantics=sem, vmem_limit_bytes=VMEM_LIMIT)


def _per_chunk(fn, *arrs):
    width = arrs[0].shape[-1]
    outs = [fn(*[a[:, c:c + LANE] for a in arrs]) for c in range(0, width, LANE)]
    return outs[0] if len(outs) == 1 else jnp.concatenate(outs, axis=-1)


def _lane_iota(shape):
    return lax.broadcasted_iota(jnp.int32, shape, len(shape) - 1)


def _head_rsqrt(x, ones2):
    ss = x * x
    hi = ss.astype(BF16)
    lo = (ss - hi.astype(F32)).astype(BF16)
    tot = jnp.dot(jnp.concatenate([hi, lo], axis=-1), ones2, preferred_element_type=F32)
    return lax.rsqrt(tot * (1.0 / GQA_HEAD_DIM) + EPS)


def _rope(x, x_sw, cos, sin_signed):
    return _per_chunk(lambda c, w: c * cos + w * sin_signed, x, x_sw)


def _row_rms(x):
    return x * lax.rsqrt(jnp.mean(x * x, axis=-1, keepdims=True) + EPS)


def _silu(x):
    return x * (1.0 / (1.0 + jnp.exp(-x)))


def _ada_kernel(c_ref, w_ref, b_ref, o_ref):
    s = _silu(c_ref[...])
    o_ref[0] = jnp.dot(s, w_ref[0], precision=HI, preferred_element_type=F32) + b_ref[0]


def _ada(cs, ada_w, ada_b):
    depth, d, d3 = ada_w.shape
    rows = cs.shape[0]
    nt = d3 // d
    return pl.pallas_call(
        _ada_kernel,
        grid=(depth, nt),
        in_specs=[pl.BlockSpec((rows, d), lambda i, j: (0, 0)),
                  pl.BlockSpec((1, d, d), lambda i, j: (i, 0, j)),
                  pl.BlockSpec((1, 1, d), lambda i, j: (i, 0, j))],
        out_specs=pl.BlockSpec((1, rows, d), lambda i, j: (i, 0, j)),
        out_shape=jax.ShapeDtypeStruct((depth, rows, d3), F32),
        compiler_params=_cparams(("arbitrary", "arbitrary")),
        name="ada",
    )(cs, ada_w, ada_b.reshape(depth, 1, d3))


_QA_W, _KA_W, _VA_W, _KPE_W = 512, 512, 256, 128


def _prep_kernel(*refs, latent, d):
    if latent:
        (x_ref, ada_ref, nw_ref, w_ref, qn_ref, kn_ref, cqn_ref, ckvn_ref, wuq_ref, wukv_ref, o2q_ref, o2k_ref,
         ca_ref, sa_ref, cm_ref, sm_ref,
         qa_o, ka_o, va_o, qm_o, km_o, vm_o, sg_o) = refs
    else:
        (x_ref, ada_ref, nw_ref, w_ref, kn_ref, ckvn_ref, wukv_ref, o2k_ref,
         ka_o, va_o, km_o, vm_o) = refs
    ada = ada_ref[0]
    shift, scale = ada[:, :d], ada[:, d:2 * d]
    h = (_row_rms(x_ref[0]) * nw_ref[...]) * (1.0 + scale) + shift
    p = jnp.dot(h.astype(BF16), w_ref[...], preferred_element_type=F32)
    off = 0

    def take(width):
        nonlocal off
        off += width
        return p[:, off - width:off]

    if latent:
        qa, qa_sw = take(_QA_W), take(_QA_W)
        k, k_sw = take(LANE), take(LANE)
    else:
        k = take(LANE)
    v = take(LANE)
    if latent:
        cq = take(MLA_Q_RANK)
    ckv = take(MLA_KV_RANK)
    kpe = take(LANE)

    kn = kn_ref[...]
    rk = _head_rsqrt(k, o2k_ref[...])
    k = k * rk * kn[0:1]
    if latent:
        k = _rope(k, k_sw * rk * kn[1:2], ca_ref[...], sa_ref[...])
        kpe = _rope(kpe, take(LANE), cm_ref[...], sm_ref[...])
    low = _lane_iota(k.shape) < (LANE // 2)
    k_x = pltpu.roll(k, LANE // 2, axis=1)
    v_x = pltpu.roll(v, LANE // 2, axis=1)
    zero = jnp.zeros_like(k)
    ka = [jnp.where(low, k, zero), jnp.where(low, zero, k_x), jnp.where(low, k_x, zero), jnp.where(low, zero, k)]
    ka_o[0] = jnp.concatenate(ka, axis=-1).astype(BF16)
    one64 = (_lane_iota(k.shape) == LANE // 2).astype(F32)
    va = jnp.concatenate([jnp.where(low, v, one64), jnp.where(low, v_x, one64)], axis=-1)
    va_o[0] = va.T.astype(BF16)
    ckv_n = (_row_rms(ckv) * ckvn_ref[...]).astype(BF16)
    kv = jnp.dot(ckv_n, wukv_ref[...], preferred_element_type=F32)
    nk = MLA_HEADS * LANE
    km_o[0] = _per_chunk(lambda c: c + kpe, kv[:, :nk]).astype(BF16)
    vm_o[0] = _per_chunk(lambda c: c + one64, kv[:, nk:]).T.astype(BF16)
    if latent:
        qn = qn_ref[...]
        rq = _head_rsqrt(qa, o2q_ref[...])
        qa = _rope(qa * rq * qn[0:1], qa_sw * rq * qn[1:2], ca_ref[...], sa_ref[...])
        qa_o[0] = (qa * (GQA_HEAD_DIM ** -0.5 * LOG2E)).astype(BF16)
        cq_n = (_row_rms(cq) * cqn_ref[...]).astype(BF16)
        qm = jnp.dot(cq_n, wuq_ref[...], preferred_element_type=F32)
        qm = _rope(qm[:, :nk], qm[:, nk:], cm_ref[...], sm_ref[...])
        qm_o[0] = (qm * ((MLA_NOPE_DIM + MLA_ROPE_DIM) ** -0.5 * LOG2E)).astype(BF16)
        sg_o[0] = _silu(take(d)).astype(BF16)


def _prep(x, ada, nw, w, qn, kn, cqn, ckvn, wuq, wukv, o2q, o2k, tabs, *, latent, tl):
    b, l, d = x.shape
    grid = (l // tl, b)
    row = lambda i, j: (j, i, 0)
    const = lambda i, j: (0, 0)
    tab = lambda i, j: (i, 0)
    xspec = pl.BlockSpec((1, tl, d), row)
    adaspec = pl.BlockSpec((1, 1, ada.shape[-1]), lambda i, j: (j, 0, 0))

    def full(a):
        return pl.BlockSpec(a.shape, const)

    def out(width):
        return (pl.BlockSpec((1, tl, width), row), jax.ShapeDtypeStruct((b, l, width), BF16))

    def out_t(width):
        return (pl.BlockSpec((1, width, tl), lambda i, j: (j, 0, i)), jax.ShapeDtypeStruct((b, width, l), BF16))

    if latent:
        ca, sa, cm, sm = tabs
        ins = [x, ada, nw, w, qn, kn, cqn, ckvn, wuq, wukv, o2q, o2k, ca, sa, cm, sm]
        in_specs = [xspec, adaspec, full(nw), full(w), full(qn), full(kn), full(cqn), full(ckvn),
                    full(wuq), full(wukv), full(o2q), full(o2k)] + [pl.BlockSpec((tl, t.shape[1]), tab) for t in tabs]
        outs = [out(_QA_W), out(_KA_W), out_t(_VA_W), out(MLA_HEADS * LANE), out(MLA_HEADS * LANE),
                out_t(MLA_HEADS * LANE), out(d)]
    else:
        ins = [x, ada, nw, w, kn, ckvn, wukv, o2k]
        in_specs = [xspec, adaspec, full(nw), full(w), full(kn), full(ckvn), full(wukv), full(o2k)]
        outs = [out(_KA_W), out_t(_VA_W), out(MLA_HEADS * LANE), out_t(MLA_HEADS * LANE)]
    return pl.pallas_call(
        functools.partial(_prep_kernel, latent=latent, d=d),
        grid=grid, in_specs=in_specs,
        out_specs=[o[0] for o in outs], out_shape=[o[1] for o in outs],
        compiler_params=_cparams(("arbitrary", "arbitrary")),
        name="prep_lat" if latent else "prep_ctx",
    )(*ins)


_ATT_SUB = 256
_ATT_VROWS = 80


def _attn_kernel(q_ref, kc_ref, vc_ref, kl_ref, vl_ref, o_ref, s0, s1, p0, p1, acc_ref, *, tk, q_shared):
    tq = q_ref.shape[1]
    lk, lc = kl_ref.shape[1], kc_ref.shape[1]
    nblk = lk // tk
    t = _ATT_SUB
    dn = (((1,), (1,)), ((), ()))
    lanes = lambda e: slice(e * LANE, (e + 1) * LANE)

    def q_sub(e, qc):
        return q_ref[0, qc * t:(qc + 1) * t, lanes(0 if q_shared else e)]

    def vrows(v_ref, e, cols):
        base = 0 if q_shared else e * LANE
        return v_ref[0, base:base + _ATT_VROWS, cols]

    def mxu_phase(s_next, k_next, p_prev, v_prev, nk_prev, alpha):
        smax = [[], []]
        for qc in range(tq // t):
            cols = slice(qc * t, (qc + 1) * t)
            for e in range(2):
                cmax = None
                for kr in range(tk // t if s_next is not None else 0):
                    rows = slice(kr * t, (kr + 1) * t)
                    s_tile = lax.dot_general(k_next(e, kr), q_sub(e, qc), dn, preferred_element_type=F32)
                    s_next[e, rows, cols] = s_tile
                    tmax = jnp.max(s_tile, axis=0, keepdims=True)
                    cmax = tmax if cmax is None else jnp.maximum(cmax, tmax)
                smax[e].append(cmax)
                if p_prev is not None:
                    pv = jnp.dot(v_prev(e), p_prev(e, slice(0, nk_prev * t), cols), preferred_element_type=F32)
                    acc_ref[e, :, cols] = alpha[e][:, cols] * acc_ref[e, :, cols] + pv
        return None if s_next is None else [jnp.concatenate(c, axis=-1) for c in smax]

    def softmax_phase(s_cur, smax, p_cur, m):
        m_new, alpha, ps = [], [], []
        for e in range(2):
            s = s_cur[e]
            mn = jnp.maximum(m[e], smax[e])
            p = jnp.exp2(s - mn).astype(BF16)
            if p_cur is not None:
                p_cur[e] = p
            m_new.append(mn)
            alpha.append(jnp.exp2(m[e] - mn))
            ps.append(p)
        return m_new, alpha, ps

    def from_ref(ref):
        return lambda e, rows, cols: ref[e, rows, cols]

    def k_lat(r):
        return lambda e, kr: kl_ref[0, pl.ds(pl.multiple_of(r + kr * t, t), t), lanes(e)]

    def v_lat(r):
        return lambda e: vrows(vl_ref, e, pl.ds(pl.multiple_of(r, t), tk))

    acc_ref[...] = jnp.zeros_like(acc_ref)
    ones = [jnp.ones((1, tq), F32)] * 2
    m = [jnp.full((1, tq), -1e30, F32)] * 2

    s_ctx = [lax.dot_general(kc_ref[0, :, lanes(e)], q_ref[0, :, lanes(0 if q_shared else e)], dn,
                             preferred_element_type=F32) for e in range(2)]
    x0 = mxu_phase(s0, k_lat(0), None, None, 0, None)
    m, _, p_ctx = softmax_phase(s_ctx, [jnp.max(s, axis=0, keepdims=True) for s in s_ctx], None, m)
    x1 = mxu_phase(s1, k_lat(tk), lambda e, rows, cols: p_ctx[e][rows, cols],
                   lambda e: vrows(vc_ref, e, slice(0, lc)), lc // t, ones)
    m, alpha, _ = softmax_phase(s0, x0, p0, m)

    def body(j, carry):
        m, alpha, x1 = (list(c) for c in carry)
        r = pl.multiple_of(2 * j * tk, tk)
        x0 = mxu_phase(s0, k_lat(r + 2 * tk), from_ref(p0), v_lat(r), tk // t, alpha)
        m, alpha, _ = softmax_phase(s1, x1, p1, m)
        x1 = mxu_phase(s1, k_lat(r + 3 * tk), from_ref(p1), v_lat(r + tk), tk // t, alpha)
        m, alpha, _ = softmax_phase(s0, x0, p0, m)
        return tuple(m), tuple(alpha), tuple(x1)

    m, alpha, x1 = lax.fori_loop(0, (nblk - 2) // 2, body, (tuple(m), tuple(alpha), tuple(x1)))
    mxu_phase(None, None, from_ref(p0), v_lat((nblk - 2) * tk), tk // t, alpha)
    m, alpha, _ = softmax_phase(s1, list(x1), p1, list(m))
    mxu_phase(None, None, from_ref(p1), v_lat((nblk - 1) * tk), tk // t, alpha)
    dv = LANE // 2
    outs = [acc_ref[e, :dv, :] / acc_ref[e, dv:dv + 1, :] for e in range(2)]
    o_ref[0] = jnp.concatenate(outs, axis=0).T.astype(o_ref.dtype)


def _attention(q, kc, vc, kl, vl, *, q_shared, kv_group, tq, tk):
    b, l, _ = q.shape
    wq = LANE if q_shared else 2 * LANE
    pairs = q.shape[-1] // wq
    lc = kc.shape[1]
    assert (l // tk) % 2 == 0, "key blocks are consumed two per loop trip"
    vw = LANE if q_shared else 2 * LANE
    kv = lambda bi, j, i: (bi, 0, j // kv_group)
    vt = lambda bi, j, i: (bi, j // kv_group, 0)
    return pl.pallas_call(
        functools.partial(_attn_kernel, tk=tk, q_shared=q_shared),
        grid=(b, pairs, l // tq),
        in_specs=[pl.BlockSpec((1, tq, wq), lambda bi, j, i: (bi, i, j)),
                  pl.BlockSpec((1, lc, 2 * LANE), kv), pl.BlockSpec((1, vw, lc), vt),
                  pl.BlockSpec((1, l, 2 * LANE), kv), pl.BlockSpec((1, vw, l), vt)],
        out_specs=pl.BlockSpec((1, tq, LANE), lambda bi, j, i: (bi, i, j)),
        out_shape=jax.ShapeDtypeStruct((b, l, pairs * LANE), BF16),
        scratch_shapes=[pltpu.VMEM((2, tk, tq), F32), pltpu.VMEM((2, tk, tq), F32),
                        pltpu.VMEM((2, tk, tq), BF16), pltpu.VMEM((2, tk, tq), BF16),
                        pltpu.VMEM((2, _ATT_VROWS, tq), F32)],
        compiler_params=_cparams(("arbitrary", "arbitrary", "arbitrary")),
        name="attn_gqa" if q_shared else "attn_mla",
    )(q, kc, vc, kl, vl)


def _out0_kernel(x_ref, oa_ref, om_ref, sg_ref, w_ref, ada_ref, o_ref, *, d):
    o = jnp.concatenate([oa_ref[0], om_ref[0]], axis=-1).astype(F32) * sg_ref[0].astype(F32)
    y = jnp.dot(o.astype(BF16), w_ref[...], preferred_element_type=F32)
    o_ref[0] = x_ref[0] + ada_ref[0][:, 2 * d:3 * d] * y


def _out0(x, oa, om, sg, w, ada, *, tl):
    b, l, d = x.shape
    row = lambda bi, i: (bi, i, 0)
    return pl.pallas_call(
        functools.partial(_out0_kernel, d=d),
        grid=(b, l // tl),
        in_specs=[pl.BlockSpec((1, tl, d), row), pl.BlockSpec((1, tl, oa.shape[-1]), row),
                  pl.BlockSpec((1, tl, om.shape[-1]), row), pl.BlockSpec((1, tl, d), row),
                  pl.BlockSpec(w.shape, lambda bi, i: (0, 0)),
                  pl.BlockSpec((1, 1, ada.shape[-1]), lambda bi, i: (bi, 0, 0))],
        out_specs=pl.BlockSpec((1, tl, d), row),
        out_shape=jax.ShapeDtypeStruct((b, l, d), F32),
        compiler_params=_cparams(("arbitrary", "arbitrary")),
        name="out0",
    )(x, oa, om, sg, w, ada)


_HALO = 8
_HY_STORE = BF16


def _hyin_kernel(x_ref, xp_ref, xn_ref, ada_ref, nw_ref, w_ref, cw_ref, cb_ref, *o_refs, d):
    i = pl.program_id(1)
    tl = x_ref.shape[1]
    n_conv = len(o_refs) - 1
    ada = ada_ref[0]
    shift, scale = ada[:, :d], ada[:, d:2 * d]

    def mod(x):
        return (_row_rms(x) * nw_ref[...]) * (1.0 + scale) + shift

    hp = mod(xp_ref[0]) * (i > 0).astype(F32)
    hn = mod(xn_ref[0]) * (i < pl.num_programs(1) - 1).astype(F32)
    h = jnp.concatenate([mod(x_ref[0]), hp, hn], axis=0).astype(BF16)
    rows = lax.broadcasted_iota(jnp.int32, (tl, d), 0)
    cw = cw_ref[...]
    for n in range(n_conv + 1):
        p = jnp.dot(h, w_ref[:, n * d:(n + 1) * d], preferred_element_type=F32)
        pm = p[0:tl]
        if n < n_conv:
            prev = jnp.where(rows == 0, p[tl + _HALO - 1:tl + _HALO], pltpu.roll(pm, 1, axis=0))
            nxt = jnp.where(rows == tl - 1, p[tl + _HALO:tl + _HALO + 1], pltpu.roll(pm, tl - 1, axis=0))
            c0, c1, c2 = (cw[j:j + 1, n * d:(n + 1) * d] for j in range(HY_SHORT))
            out = prev * c0 + pm * c1 + nxt * c2 + cb_ref[:, n * d:(n + 1) * d]
        else:
            out = _silu(pm)
        o_refs[n][0] = out.astype(o_refs[n].dtype)


def _hyin(x, ada, nw, w, cw, cb, *, tl):
    b, l, d = x.shape
    ng = w.shape[1] // d
    tb = tl // _HALO
    nb = l // _HALO
    const = lambda bi, i: (0, 0)
    blk = pl.BlockSpec((1, tl, d), lambda bi, i: (bi, i, 0))
    return pl.pallas_call(
        functools.partial(_hyin_kernel, d=d),
        grid=(b, l // tl),
        in_specs=[blk,
                  pl.BlockSpec((1, _HALO, d), lambda bi, i: (bi, jnp.maximum(i * tb - 1, 0), 0)),
                  pl.BlockSpec((1, _HALO, d), lambda bi, i: (bi, jnp.minimum((i + 1) * tb, nb - 1), 0)),
                  pl.BlockSpec((1, 1, ada.shape[-1]), lambda bi, i: (bi, 0, 0)),
                  pl.BlockSpec(nw.shape, const), pl.BlockSpec(w.shape, const),
                  pl.BlockSpec(cw.shape, const), pl.BlockSpec(cb.shape, const)],
        out_specs=[blk] * ng,
        out_shape=[jax.ShapeDtypeStruct((b, l, d), _HY_STORE)] * ng,
        compiler_params=_cparams(("arbitrary", "arbitrary")),
        name="hy_in",
    )(x, x, x, ada, nw, w, cw, cb)


def _filt_kernel(emb_ref, w1_ref, b1_ref, w2_ref, b2_ref, w3_ref, b3_ref, fr_ref, dl_ref, h_o, s_o, *, reps):
    emb = emb_ref[...]
    fr = fr_ref[...]
    hid = jnp.sin(fr * (jnp.dot(emb, w1_ref[...], precision=HI, preferred_element_type=F32) + b1_ref[...]))
    hid = jnp.sin(fr * (jnp.dot(hid, w2_ref[...], precision=HI, preferred_element_type=F32) + b2_ref[...]))
    h = jnp.dot(hid, w3_ref[...], precision=HI, preferred_element_type=F32) + b3_ref[...]
    win = jnp.exp(-emb[:, 0:1] * dl_ref[...])
    hw = h * jnp.concatenate([win] * reps, axis=-1)
    rows = lax.broadcasted_iota(jnp.int32, hw.shape, 0) + pl.program_id(0) * hw.shape[0]
    hw = jnp.where((rows == 0) & (_lane_iota(hw.shape) >= hw.shape[1] // 2), 0.0, hw)
    h_o[...] = hw

    @pl.when(pl.program_id(0) == 0)
    def _():
        s_o[...] = jnp.zeros_like(s_o)

    s_o[...] += jnp.sum(jnp.abs(hw), axis=0, keepdims=True)


def _filters(emb, w1, b1, w2, b2, w3, b3, fr, dl, *, tl):
    l = emb.shape[0]
    wo = w3.shape[1]
    const = lambda i: (0, 0)
    full = lambda a: pl.BlockSpec(a.shape, const)
    return pl.pallas_call(
        functools.partial(_filt_kernel, reps=wo // dl.shape[1]),
        grid=(l // tl,),
        in_specs=[pl.BlockSpec((tl, emb.shape[1]), lambda i: (i, 0)), full(w1), full(b1), full(w2), full(b2),
                  full(w3), full(b3), full(fr), full(dl)],
        out_specs=[pl.BlockSpec((tl, wo), lambda i: (i, 0)), pl.BlockSpec((1, wo), const)],
        out_shape=[jax.ShapeDtypeStruct((l, wo), F32), jax.ShapeDtypeStruct((1, wo), F32)],
        compiler_params=_cparams(("arbitrary",)),
        name="hy_filter",
    )(emb, w1, b1, w2, b2, w3, b3, fr, dl)


def _cmul_const(a, ang):
    ar, ai = a
    q = ang / (0.5 * math.pi)
    if abs(q - round(q)) < 1e-12:
        return [(ar, ai), (-ai, ar), (-ar, -ai), (ai, -ar)][int(round(q)) % 4]
    c, s = math.cos(ang), math.sin(ang)
    return (ar * c - ai * s, ar * s + ai * c)


def _fft_dif(x):
    x = list(x)
    n = len(x)
    half = n // 2
    while half >= 1:
        for base in range(0, n, 2 * half):
            for j in range(half):
                a, b = x[base + j], x[base + j + half]
                ang = -math.pi * j / half
                if b is None:
                    x[base + j + half] = None if a is None else _cmul_const(a, ang)
                else:
                    x[base + j] = (a[0] + b[0], a[1] + b[1])
                    x[base + j + half] = _cmul_const((a[0] - b[0], a[1] - b[1]), ang)
        half //= 2
    return x


def _ifft_dit(x, keep):
    x = list(x)
    n = len(x)
    half = 1
    while half <= n // 2:
        last = half == n // 2
        for base in range(0, n, 2 * half):
            for j in range(half):
                a = x[base + j]
                b = _cmul_const(x[base + j + half], math.pi * j / half)
                x[base + j] = (a[0] + b[0], a[1] + b[1])
                if not last or base + j + half < keep:
                    x[base + j + half] = (a[0] - b[0], a[1] - b[1])
        half *= 2
    return x[:keep]


def _outer_fwd_kernel(u_ref, o_ref, *, real_input):
    n1 = o_ref.shape[2]
    if real_input:
        x = [(u_ref[i], jnp.zeros_like(u_ref[i])) for i in range(n1 // 2)]
    else:
        x = [(u_ref[0, 0, i].astype(F32), u_ref[0, 1, i].astype(F32)) for i in range(n1 // 2)]
    for s, (re, im) in enumerate(_fft_dif(x + [None] * (n1 // 2))):
        o_ref[0, 0, s] = re.astype(o_ref.dtype)
        o_ref[0, 1, s] = im.astype(o_ref.dtype)


def _outer_fwd(u, *, real_input, tn2, ct, nseq=1):
    n2 = u.shape[-2]
    if real_input:
        p, c, n1 = nseq, u.shape[-1] // nseq, 2 * u.shape[0]
        in_spec = pl.BlockSpec((n1 // 2, tn2, ct), lambda pi, r, j: (0, r, pi * (c // ct) + j))
    else:
        p, c, n1 = u.shape[0], u.shape[-1], 2 * u.shape[2]
        in_spec = pl.BlockSpec((1, 2, n1 // 2, tn2, ct), lambda pi, r, j: (pi, 0, 0, r, j))
    return pl.pallas_call(
        functools.partial(_outer_fwd_kernel, real_input=real_input),
        grid=(p, n2 // tn2, c // ct), in_specs=[in_spec],
        out_specs=pl.BlockSpec((1, 2, n1, tn2, ct), lambda pi, r, j: (pi, 0, 0, r, j)),
        out_shape=jax.ShapeDtypeStruct((p, 2, n1, n2, c), _HY_STORE),
        compiler_params=_cparams(("arbitrary", "arbitrary", "arbitrary")),
        name="hy_outer_filt" if real_input else "hy_outer_fwd",
    )(u)


def _outer_inv_kernel(a_ref, u_ref, g_ref, sk_ref, o_ref, *next_ref):
    n1 = a_ref.shape[2]
    f = lambda ref, r, i: ref[0, r, i].astype(F32)
    y = _ifft_dit([(f(a_ref, 0, s), f(a_ref, 1, s)) for s in range(n1)], n1 // 2)
    sk = sk_ref[...]
    z = [(f(g_ref, 0, i) * (re + f(u_ref, 0, i) * sk), f(g_ref, 1, i) * (im + f(u_ref, 1, i) * sk))
         for i, (re, im) in enumerate(y)]
    for i, (re, im) in enumerate(z):
        o_ref[0, 0, i] = re.astype(o_ref.dtype)
        o_ref[0, 1, i] = im.astype(o_ref.dtype)
    if next_ref:
        nxt = next_ref[0]
        for s, (re, im) in enumerate(_fft_dif(z + [None] * (n1 // 2))):
            nxt[0, 0, s] = re.astype(nxt.dtype)
            nxt[0, 1, s] = im.astype(nxt.dtype)


def _outer_inv(a, u, gate, skip, *, tn2, ct, with_next):
    p, _, n1, n2, c = a.shape
    blk = lambda rows: pl.BlockSpec((1, 2, rows, tn2, ct), lambda pi, r, j: (pi, 0, 0, r, j))
    out_specs, out_shape = [blk(n1 // 2)], [jax.ShapeDtypeStruct(u.shape, u.dtype)]
    if with_next:
        out_specs.append(blk(n1))
        out_shape.append(jax.ShapeDtypeStruct(a.shape, a.dtype))
    return pl.pallas_call(
        _outer_inv_kernel,
        grid=(p, n2 // tn2, c // ct),
        in_specs=[blk(n1), blk(n1 // 2), blk(n1 // 2), pl.BlockSpec((1, ct), lambda pi, r, j: (0, j))],
        out_specs=out_specs, out_shape=out_shape,
        compiler_params=_cparams(("arbitrary", "arbitrary", "arbitrary")),
        name="hy_outer_inv_fwd" if with_next else "hy_outer_inv",
    )(a, u, gate, skip)


def _dot3(m3, x):
    hi = x.astype(BF16)
    lo = (x - hi.astype(F32)).astype(BF16)
    return jnp.dot(m3, jnp.concatenate([hi, hi, lo], axis=0), preferred_element_type=F32)


def _mid_kernel(a_ref, g_ref, *rest, spectrum):
    np_, n2, ct = a_ref.shape[0], a_ref.shape[3], a_ref.shape[4]
    mm = lambda m, x: jnp.dot(m, x.astype(BF16), preferred_element_type=F32)
    xs = [mm(g_ref[0], a_ref[p, :, 0].reshape(2 * n2, ct)) for p in range(np_)]
    if spectrum:
        sc_ref, o_ref = rest
        f, bw = xs
        sc = sc_ref[...]
        o_ref[0, 0, 0] = (f[:n2] + bw[:n2]) * sc
        o_ref[0, 1, 0] = (f[n2:] - bw[n2:]) * sc
    else:
        gt_ref, kf_ref, o_ref = rest
        kr, ki = kf_ref[0, 0, 0], kf_ref[0, 1, 0]
        ys = [jnp.concatenate([x[:n2] * kr - x[n2:] * ki, x[:n2] * ki + x[n2:] * kr], axis=0) for x in xs]
        for p in range(np_):
            o_ref[p, :, 0] = mm(gt_ref[0], ys[p]).reshape(2, n2, ct).astype(o_ref.dtype)


def _mid(a, g, gt=None, kf=None, scale=None, *, ct, kf_col0=0):
    p, _, n1, n2, c = a.shape
    ablk = pl.BlockSpec((p, 2, 1, n2, ct), lambda k, j: (0, 0, k, 0, j))
    gblk = pl.BlockSpec((1,) + g.shape[1:], lambda k, j: (k, 0, 0))
    oblk, oshape = ablk, a.shape
    if kf is None:
        ins = [a, g, scale]
        in_specs = [ablk, gblk, pl.BlockSpec((1, ct), lambda k, j: (0, j))]
        oblk, oshape = pl.BlockSpec((1, 2, 1, n2, ct), lambda k, j: (0, 0, k, 0, j)), (1,) + a.shape[1:]
    else:
        ins = [a, g, gt, kf]
        in_specs = [ablk, gblk, gblk,
                    pl.BlockSpec((1, 2, 1, n2, ct), lambda k, j: (0, 0, k, 0, kf_col0 + j))]
    return pl.pallas_call(
        functools.partial(_mid_kernel, spectrum=kf is None),
        grid=(n1, c // ct), in_specs=in_specs, out_specs=oblk,
        out_shape=jax.ShapeDtypeStruct(oshape, F32 if kf is None else a.dtype),
        compiler_params=_cparams(("arbitrary", "arbitrary")),
        name="hy_spectrum" if kf is None else "hy_mid",
    )(*ins)


def _dft_tables(l):
    n = 2 * l
    n2 = DFT_N2
    n1 = n // n2
    bits = n1.bit_length() - 1
    k1 = np.array([int(format(s, "0%db" % bits)[::-1], 2) for s in range(n1)], dtype=np.float64)
    kk = k1[:, None, None] + n1 * np.arange(n2, dtype=np.float64)[None, :, None]
    th = 2.0 * np.pi * kk * np.arange(n2, dtype=np.float64)[None, None, :] / n
    c, s = np.cos(th), np.sin(th)
    g = np.concatenate([np.concatenate([c, s], axis=2), np.concatenate([-s, c], axis=2)], axis=1)

    def split3(m):
        m = jnp.asarray(m, F32)
        hi = m.astype(BF16)
        lo = (m - hi.astype(F32)).astype(BF16)
        return jnp.concatenate([hi, lo, hi], axis=2)

    return split3(g), jnp.asarray(g, BF16), jnp.asarray(np.transpose(g, (0, 2, 1)), BF16)


def _pair_view(u, n1h):
    b, l, c = u.shape
    return u.reshape(b // 2, 2, n1h, l // n1h, c)


def _long_convs(u, gates, skips, kf, tabs, *, ct):
    _, g, gt = tabs
    c = u.shape[-1]
    a = _outer_fwd(u, real_input=False, tn2=32, ct=256)
    z = u
    for o, (gate, skip) in enumerate(zip(gates, skips)):
        a = _mid(a, g, gt, kf, ct=ct, kf_col0=o * (c // ct))
        last = o == len(gates) - 1
        res = _outer_inv(a, z, gate, skip, tn2=32, ct=256, with_next=not last)
        z, a = (res[0], None) if last else res
    return z


def _out1_kernel(x_ref, z_ref, sg_ref, w_ref, ada_ref, fw_ref, o_ref, *, d):
    y = jnp.dot((z_ref[0].astype(F32) * sg_ref[0].astype(F32)).astype(BF16), w_ref[...],
                preferred_element_type=F32)
    x = x_ref[0] + ada_ref[0][:, 2 * d:3 * d] * y
    o_ref[0] = _row_rms(x) * fw_ref[...]


def _out1(x, z, sg, w, ada, fw, *, tl):
    b, l, d = x.shape
    row = lambda bi, i: (bi, i, 0)
    blk = pl.BlockSpec((1, tl, d), row)
    return pl.pallas_call(
        functools.partial(_out1_kernel, d=d),
        grid=(b, l // tl),
        in_specs=[blk, blk, blk, pl.BlockSpec(w.shape, lambda bi, i: (0, 0)),
                  pl.BlockSpec((1, 1, ada.shape[-1]), lambda bi, i: (bi, 0, 0)),
                  pl.BlockSpec(fw.shape, lambda bi, i: (0, 0))],
        out_specs=blk,
        out_shape=jax.ShapeDtypeStruct((b, l, d), F32),
        compiler_params=_cparams(("arbitrary", "arbitrary")),
        name="out1",
    )(x, z, sg, w, ada, fw)


def _swap_cols(w, q):
    return w[..., np.arange(w.shape[-1]) ^ q]


def _pack_attn_w_in(w):
    d = w.shape[0]
    o = 0
    wq = w[:, o:o + 512]; o += 512
    wk = w[:, o:o + 128]; o += 128
    wv = w[:, o:o + 128]; o += 128
    wcq = w[:, o:o + MLA_Q_RANK]; o += MLA_Q_RANK
    wckv = w[:, o:o + MLA_KV_RANK]; o += MLA_KV_RANK
    wkpe = w[:, o:o + MLA_ROPE_DIM]; o += MLA_ROPE_DIM
    wg = w[:, o:]
    qa, qm = GQA_HEAD_DIM // 4, MLA_ROPE_DIM // 4

    def pe_chunk(wp):
        return jnp.concatenate([jnp.zeros((d, 64), w.dtype), wp, jnp.zeros((d, 32), w.dtype)], axis=1)

    kpe, kpe_sw = pe_chunk(wkpe), pe_chunk(_swap_cols(wkpe, qm))
    kv_part = jnp.concatenate([wk, wv, wckv, kpe], axis=1)
    lat = jnp.concatenate([wq, _swap_cols(wq, qa), wk, _swap_cols(wk, qa), wv, wcq, wckv, kpe, kpe_sw, wg], axis=1)
    return lat.astype(BF16), kv_part.astype(BF16)


def _pack_mla_up(w_uq, w_ukv):
    dq = MLA_NOPE_DIM + MLA_ROPE_DIM
    r = w_uq.shape[0]
    z = lambda n: jnp.zeros((r, n), w_uq.dtype)
    uq, uq_sw = [], []
    for h in range(MLA_HEADS):
        nope, pe = w_uq[:, dq * h:dq * h + MLA_NOPE_DIM], w_uq[:, dq * h + MLA_NOPE_DIM:dq * (h + 1)]
        uq += [nope, pe, z(LANE - dq)]
        uq_sw += [z(MLA_NOPE_DIM), _swap_cols(pe, MLA_ROPE_DIM // 4), z(LANE - dq)]
    dkv = MLA_NOPE_DIM + MLA_V_DIM
    kn = jnp.concatenate(
        [jnp.concatenate([w_ukv[:, dkv * h:dkv * h + MLA_NOPE_DIM],
                          jnp.zeros((w_ukv.shape[0], LANE - MLA_NOPE_DIM), w_ukv.dtype)], axis=1)
         for h in range(MLA_HEADS)], axis=1)
    vm = jnp.concatenate(
        [jnp.concatenate([w_ukv[:, dkv * h + MLA_NOPE_DIM:dkv * (h + 1)],
                          jnp.zeros((w_ukv.shape[0], LANE - MLA_V_DIM), w_ukv.dtype)], axis=1)
         for h in range(MLA_HEADS)], axis=1)
    return jnp.concatenate(uq + uq_sw, axis=1).astype(BF16), jnp.concatenate([kn, vm], axis=1).astype(BF16)


def _head_ones2(width):
    i = np.arange(width) // GQA_HEAD_DIM
    blk = (i[:, None] == i[None, :]).astype(np.float32)
    return jnp.asarray(np.concatenate([blk, blk], axis=0), BF16)


def _rope_tables(l):
    rows = (jnp.arange(l, dtype=jnp.int32) // GRID_W).astype(F32)[:, None]
    cols = (jnp.arange(l, dtype=jnp.int32) % GRID_W).astype(F32)[:, None]

    def tab(rot_dim):
        q = rot_dim // 4
        inv = ROPE_BASE ** (-jnp.arange(q, dtype=F32) / q)
        ar, ac = rows * inv, cols * inv
        cos = jnp.concatenate([jnp.cos(ar)] * 2 + [jnp.cos(ac)] * 2, axis=1)
        sin = jnp.concatenate([-jnp.sin(ar), jnp.sin(ar), -jnp.sin(ac), jnp.sin(ac)], axis=1)
        return cos, sin

    ca, sa = tab(GQA_HEAD_DIM)
    ca = jnp.concatenate([ca] * (LANE // GQA_HEAD_DIM), axis=1)
    sa = jnp.concatenate([sa] * (LANE // GQA_HEAD_DIM), axis=1)
    cm, sm = tab(MLA_ROPE_DIM)
    one, zero = jnp.ones((l, 1), F32), jnp.zeros((l, 1), F32)
    cm = jnp.concatenate([jnp.tile(one, (1, 64)), cm, jnp.tile(one, (1, 32))], axis=1)
    sm = jnp.concatenate([jnp.tile(zero, (1, 64)), sm, jnp.tile(zero, (1, 32))], axis=1)
    return ca, sa, cm, sm


def _pad2(a, r, c):
    return jnp.pad(a, ((0, r - a.shape[0]), (0, c - a.shape[1])))


def kernel(x, c, ctx, c_ctx, ada_w, ada_b, norm_w, attn_w_in, attn_q_norm, attn_k_norm, mla_q_norm, mla_kv_norm, mla_w_uq, mla_w_ukv, attn_w_out, hy_w_in, hy_conv_w, hy_conv_b, hy_ffn_w1, hy_ffn_b1, hy_ffn_w2, hy_ffn_b2, hy_ffn_w3, hy_ffn_b3, hy_freq, hy_skip, hy_w_out, final_norm_w):
    b, l, d = x.shape
    lc = ctx.shape[1]
    tl = min(256, l)

    rows = -(-(b + 1) // 8) * 8
    cs = jnp.concatenate([c, c_ctx[None, :], jnp.zeros((rows - b - 1, d), F32)], axis=0)
    ada = _ada(cs, ada_w, ada_b)
    ada_lat = [ada[i, :b].reshape(b, 1, 3 * d) for i in range(ada.shape[0])]
    ada_ctx0 = jnp.broadcast_to(ada[0, b].reshape(1, 1, 3 * d), (b, 1, 3 * d))

    w_lat, w_kv = _pack_attn_w_in(attn_w_in[0])
    wuq, wukv = _pack_mla_up(mla_w_uq[0], mla_w_ukv[0])
    nw0 = norm_w[0].reshape(1, d)
    def norm_rows(wn, width):
        sw = _swap_cols(wn, GQA_HEAD_DIM // 4)
        return jnp.stack([jnp.tile(wn, width // GQA_HEAD_DIM), jnp.tile(sw, width // GQA_HEAD_DIM)])

    qn, kn = norm_rows(attn_q_norm[0], _QA_W), norm_rows(attn_k_norm[0], LANE)
    cqn = mla_q_norm[0].reshape(1, MLA_Q_RANK)
    ckvn = mla_kv_norm[0].reshape(1, MLA_KV_RANK)
    o2q, o2k = _head_ones2(_QA_W), _head_ones2(LANE)
    tabs = _rope_tables(l)
    qa, ka, va, qm, km, vm, sg = _prep(x, ada_lat[0], nw0, w_lat, qn, kn, cqn, ckvn, wuq, wukv, o2q, o2k, tabs,
                                       latent=True, tl=min(512, l))
    kac, vac, kmc, vmc = _prep(ctx, ada_ctx0, nw0, w_kv, None, kn, None, ckvn, None, wukv, None, o2k, None,
                               latent=False, tl=min(tl, lc))
    tq, tk = min(2048, l), min(512, l // 2)
    oa = _attention(qa, kac, vac, ka, va, q_shared=True, kv_group=2, tq=tq, tk=tk)
    om = _attention(qm, kmc, vmc, km, vm, q_shared=False, kv_group=1, tq=tq, tk=tk)
    x1 = _out0(x, oa, om, sg, attn_w_out[0].astype(BF16), ada_lat[0], tl=tl)

    nw1 = norm_w[1].reshape(1, d)
    u = _hyin(x1, ada_lat[1], nw1, hy_w_in[0].astype(BF16), hy_conv_w[0], hy_conv_b[0].reshape(1, -1),
              tl=min(512, l))
    n1 = 2 * l // DFT_N2
    n1h = n1 // 2
    tabs_d = _dft_tables(l)

    t = jnp.linspace(0.0, 1.0, l, dtype=F32)[:, None]
    wpos = (2.0 * math.pi / l) * jnp.arange(l, dtype=F32)[:, None]
    bands = jnp.linspace(1e-4, HY_BANDS - 1, HY_BANDS, dtype=F32)
    emb = jnp.concatenate([t, jnp.cos(wpos * bands), -jnp.sin(wpos * bands)], axis=-1)
    deltas = jnp.abs(jnp.linspace(math.log(HY_DECAY_TARGET) / HY_SLOW_DECAY,
                                  math.log(HY_DECAY_TARGET) / HY_FAST_DECAY, d, dtype=F32)).reshape(1, d)
    wf = hy_ffn_w3.shape[-1]
    oc = HY_ORDER * d

    def by_direction(a):
        return a.reshape(-1, HY_ORDER, 2, d).transpose(0, 2, 1, 3).reshape(-1, wf)

    hw, asum = _filters(_pad2(emb, l, LANE), _pad2(hy_ffn_w1[0], LANE, LANE), _pad2(hy_ffn_b1[0][None], 1, LANE),
                        _pad2(hy_ffn_w2[0], LANE, LANE), _pad2(hy_ffn_b2[0][None], 1, LANE),
                        _pad2(by_direction(hy_ffn_w3[0]), LANE, wf), by_direction(hy_ffn_b3[0][None]),
                        _pad2(hy_freq[0][None], 1, LANE), deltas, tl=tl)
    l1 = asum[:, :oc] + asum[:, oc:]
    ct = min(1024, d)
    af = _outer_fwd(hw.reshape(n1h, DFT_N2, wf), real_input=True, nseq=2, tn2=32, ct=256)
    kf = _mid(af, tabs_d[1], scale=1.0 / (l1 * (2 * l)), ct=oc)

    v2, x1g, x2g = (_pair_view(u[i], n1h) for i in range(3))
    z = _long_convs(v2, [x1g, x2g], [hy_skip[0, o:o + 1] for o in range(HY_ORDER)], kf, tabs_d, ct=ct)
    z = z.reshape(b, l, d)
    return _out1(x1, z, u[3], hy_w_out[0].astype(BF16), ada_lat[1], final_norm_w.reshape(1, d), tl=tl)
```

```python
import functools
import math

import numpy as np
import jax
import jax.numpy as jnp
from jax import lax
from jax.experimental import pallas as pl
from jax.experimental.pallas import tpu as pltpu

EPS = 1e-6
GRID_W = 64
ROPE_BASE = 10000.0
GQA_HEADS, GQA_KV_HEADS, GQA_HEAD_DIM = 8, 2, 64
MLA_HEADS, MLA_Q_RANK, MLA_KV_RANK = 8, 256, 128
MLA_NOPE_DIM, MLA_ROPE_DIM, MLA_V_DIM = 64, 32, 64
HY_ORDER, HY_SHORT, HY_BANDS, HY_FFN = 2, 3, 16, 64
HY_FAST_DECAY, HY_SLOW_DECAY, HY_DECAY_TARGET = 0.3, 1.5, 1e-2
LANE = 128
DFT_N2 = 128
VMEM_LIMIT = 56 * 1024 * 1024
LOG2E = 1.4426950408889634
HI = lax.Precision.HIGHEST
F32 = jnp.float32
BF16 = jnp.bfloat16


def _cparams(sem):
    return pltpu.CompilerParams(dimension_semantics=sem, vmem_limit_bytes=VMEM_LIMIT)


def _per_chunk(fn, *arrs):
    width = arrs[0].shape[-1]
    outs = [fn(*[a[:, c:c + LANE] for a in arrs]) for c in range(0, width, LANE)]
    return outs[0] if len(outs) == 1 else jnp.concatenate(outs, axis=-1)


def _lane_iota(shape):
    return lax.broadcasted_iota(jnp.int32, shape, len(shape) - 1)


def _head_rsqrt(x, ones2):
    ss = x * x
    hi = ss.astype(BF16)
    lo = (ss - hi.astype(F32)).astype(BF16)
    tot = jnp.dot(jnp.concatenate([hi, lo], axis=-1), ones2, preferred_element_type=F32)
    return lax.rsqrt(tot * (1.0 / GQA_HEAD_DIM) + EPS)


def _rope(x, x_sw, cos, sin_signed):
    return _per_chunk(lambda c, w: c * cos + w * sin_signed, x, x_sw)


def _row_rms(x):
    return x * lax.rsqrt(jnp.mean(x * x, axis=-1, keepdims=True) + EPS)


def _silu(x):
    return x * (1.0 / (1.0 + jnp.exp(-x)))


def _ada_kernel(c_ref, w_ref, b_ref, o_ref):
    s = _silu(c_ref[...])
    o_ref[0] = jnp.dot(s, w_ref[0], precision=HI, preferred_element_type=F32) + b_ref[0]


def _ada(cs, ada_w, ada_b):
    depth, d, d3 = ada_w.shape
    rows = cs.shape[0]
    nt = d3 // d
    return pl.pallas_call(
        _ada_kernel,
        grid=(depth, nt),
        in_specs=[pl.BlockSpec((rows, d), lambda i, j: (0, 0)),
                  pl.BlockSpec((1, d, d), lambda i, j: (i, 0, j)),
                  pl.BlockSpec((1, 1, d), lambda i, j: (i, 0, j))],
        out_specs=pl.BlockSpec((1, rows, d), lambda i, j: (i, 0, j)),
        out_shape=jax.ShapeDtypeStruct((depth, rows, d3), F32),
        compiler_params=_cparams(("arbitrary", "arbitrary")),
        name="ada",
    )(cs, ada_w, ada_b.reshape(depth, 1, d3))


_QA_W, _KA_W, _VA_W, _KPE_W = 512, 512, 256, 128


def _prep_kernel(*refs, latent, d):
    if latent:
        (x_ref, ada_ref, nw_ref, w_ref, qn_ref, kn_ref, cqn_ref, ckvn_ref, wuq_ref, wukv_ref, o2q_ref, o2k_ref,
         ca_ref, sa_ref, cm_ref, sm_ref,
         qa_o, ka_o, va_o, qm_o, km_o, vm_o, sg_o) = refs
    else:
        (x_ref, ada_ref, nw_ref, w_ref, kn_ref, ckvn_ref, wukv_ref, o2k_ref,
         ka_o, va_o, km_o, vm_o) = refs
    ada = ada_ref[0]
    shift, scale = ada[:, :d], ada[:, d:2 * d]
    h = (_row_rms(x_ref[0]) * nw_ref[...]) * (1.0 + scale) + shift
    p = jnp.dot(h.astype(BF16), w_ref[...], preferred_element_type=F32)
    off = 0

    def take(width):
        nonlocal off
        off += width
        return p[:, off - width:off]

    if latent:
        qa, qa_sw = take(_QA_W), take(_QA_W)
        k, k_sw = take(LANE), take(LANE)
    else:
        k = take(LANE)
    v = take(LANE)
    if latent:
        cq = take(MLA_Q_RANK)
    ckv = take(MLA_KV_RANK)
    kpe = take(LANE)

    kn = kn_ref[...]
    rk = _head_rsqrt(k, o2k_ref[...])
    k = k * rk * kn[0:1]
    if latent:
        k = _rope(k, k_sw * rk * kn[1:2], ca_ref[...], sa_ref[...])
        kpe = _rope(kpe, take(LANE), cm_ref[...], sm_ref[...])
    low = _lane_iota(k.shape) < (LANE // 2)
    k_x = pltpu.roll(k, LANE // 2, axis=1)
    v_x = pltpu.roll(v, LANE // 2, axis=1)
    zero = jnp.zeros_like(k)
    ka = [jnp.where(low, k, zero), jnp.where(low, zero, k_x), jnp.where(low, k_x, zero), jnp.where(low, zero, k)]
    ka_o[0] = jnp.concatenate(ka, axis=-1).astype(BF16)
    one64 = (_lane_iota(k.shape) == LANE // 2).astype(F32)
    va = jnp.concatenate([jnp.where(low, v, one64), jnp.where(low, v_x, one64)], axis=-1)
    va_o[0] = va.T.astype(BF16)
    ckv_n = (_row_rms(ckv) * ckvn_ref[...]).astype(BF16)
    kv = jnp.dot(ckv_n, wukv_ref[...], preferred_element_type=F32)
    nk = MLA_HEADS * LANE
    km_o[0] = _per_chunk(lambda c: c + kpe, kv[:, :nk]).astype(BF16)
    vm_o[0] = _per_chunk(lambda c: c + one64, kv[:, nk:]).T.astype(BF16)
    if latent:
        qn = qn_ref[...]
        rq = _head_rsqrt(qa, o2q_ref[...])
        qa = _rope(qa * rq * qn[0:1], qa_sw * rq * qn[1:2], ca_ref[...], sa_ref[...])
        qa_o[0] = (qa * (GQA_HEAD_DIM ** -0.5 * LOG2E)).astype(BF16)
        cq_n = (_row_rms(cq) * cqn_ref[...]).astype(BF16)
        qm = jnp.dot(cq_n, wuq_ref[...], preferred_element_type=F32)
        qm = _rope(qm[:, :nk], qm[:, nk:], cm_ref[...], sm_ref[...])
        qm_o[0] = (qm * ((MLA_NOPE_DIM + MLA_ROPE_DIM) ** -0.5 * LOG2E)).astype(BF16)
        sg_o[0] = _silu(take(d)).astype(BF16)


def _prep(x, ada, nw, w, qn, kn, cqn, ckvn, wuq, wukv, o2q, o2k, tabs, *, latent, tl):
    b, l, d = x.shape
    grid = (l // tl, b)
    row = lambda i, j: (j, i, 0)
    const = lambda i, j: (0, 0)
    tab = lambda i, j: (i, 0)
    xspec = pl.BlockSpec((1, tl, d), row)
    adaspec = pl.BlockSpec((1, 1, ada.shape[-1]), lambda i, j: (j, 0, 0))

    def full(a):
        return pl.BlockSpec(a.shape, const)

    def out(width):
        return (pl.BlockSpec((1, tl, width), row), jax.ShapeDtypeStruct((b, l, width), BF16))

    def out_t(width):
        return (pl.BlockSpec((1, width, tl), lambda i, j: (j, 0, i)), jax.ShapeDtypeStruct((b, width, l), BF16))

    if latent:
        ca, sa, cm, sm = tabs
        ins = [x, ada, nw, w, qn, kn, cqn, ckvn, wuq, wukv, o2q, o2k, ca, sa, cm, sm]
        in_specs = [xspec, adaspec, full(nw), full(w), full(qn), full(kn), full(cqn), full(ckvn),
                    full(wuq), full(wukv), full(o2q), full(o2k)] + [pl.BlockSpec((tl, t.shape[1]), tab) for t in tabs]
        outs = [out(_QA_W), out(_KA_W), out_t(_VA_W), out(MLA_HEADS * LANE), out(MLA_HEADS * LANE),
                out_t(MLA_HEADS * LANE), out(d)]
    else:
        ins = [x, ada, nw, w, kn, ckvn, wukv, o2k]
        in_specs = [xspec, adaspec, full(nw), full(w), full(kn), full(ckvn), full(wukv), full(o2k)]
        outs = [out(_KA_W), out_t(_VA_W), out(MLA_HEADS * LANE), out_t(MLA_HEADS * LANE)]
    return pl.pallas_call(
        functools.partial(_prep_kernel, latent=latent, d=d),
        grid=grid, in_specs=in_specs,
        out_specs=[o[0] for o in outs], out_shape=[o[1] for o in outs],
        compiler_params=_cparams(("arbitrary", "arbitrary")),
        name="prep_lat" if latent else "prep_ctx",
    )(*ins)


_ATT_SUB = 256
_ATT_VROWS = 80


def _attn_kernel(q_ref, kc_ref, vc_ref, kl_ref, vl_ref, o_ref, s0, s1, p0, p1, acc_ref, *, tk, q_shared):
    tq = q_ref.shape[1]
    lk, lc = kl_ref.shape[1], kc_ref.shape[1]
    nblk = lk // tk
    t = _ATT_SUB
    dn = (((1,), (1,)), ((), ()))
    lanes = lambda e: slice(e * LANE, (e + 1) * LANE)

    def q_sub(e, qc):
        return q_ref[0, qc * t:(qc + 1) * t, lanes(0 if q_shared else e)]

    def vrows(v_ref, e, cols):
        base = 0 if q_shared else e * LANE
        return v_ref[0, base:base + _ATT_VROWS, cols]

    def mxu_phase(s_next, k_next, p_prev, v_prev, nk_prev, alpha):
        smax = [[], []]
        for qc in range(tq // t):
            cols = slice(qc * t, (qc + 1) * t)
            for e in range(2):
                cmax = None
                for kr in range(tk // t if s_next is not None else 0):
                    rows = slice(kr * t, (kr + 1) * t)
                    s_tile = lax.dot_general(k_next(e, kr), q_sub(e, qc), dn, preferred_element_type=F32)
                    s_next[e, rows, cols] = s_tile
                    tmax = jnp.max(s_tile, axis=0, keepdims=True)
                    cmax = tmax if cmax is None else jnp.maximum(cmax, tmax)
                smax[e].append(cmax)
                if p_prev is not None:
                    pv = jnp.dot(v_prev(e), p_prev(e, slice(0, nk_prev * t), cols), preferred_element_type=F32)
                    acc_ref[e, :, cols] = alpha[e][:, cols] * acc_ref[e, :, cols] + pv
        return None if s_next is None else [jnp.concatenate(c, axis=-1) for c in smax]

    def softmax_phase(s_cur, smax, p_cur, m):
        m_new, alpha, ps = [], [], []
        for e in range(2):
            mn = jnp.maximum(m[e], smax[e])
            if p_cur is None:
                ps.append(jnp.exp2(s_cur[e] - mn).astype(BF16))
            else:
                for qc in range(tq // t):
                    cols = slice(qc * t, (qc + 1) * t)
                    p_cur[e, :, cols] = jnp.exp2(s_cur[e, :, cols] - mn[:, cols]).astype(BF16)
            m_new.append(mn)
            alpha.append(jnp.exp2(m[e] - mn))
        return m_new, alpha, ps

    def from_ref(ref):
        return lambda e, rows, cols: ref[e, rows, cols]

    def k_lat(r):
        return lambda e, kr: kl_ref[0, pl.ds(pl.multiple_of(r + kr * t, t), t), lanes(e)]

    def v_lat(r):
        return lambda e: vrows(vl_ref, e, pl.ds(pl.multiple_of(r, t), tk))

    acc_ref[...] = jnp.zeros_like(acc_ref)
    ones = [jnp.ones((1, tq), F32)] * 2
    m = [jnp.full((1, tq), -1e30, F32)] * 2

    s_ctx = [lax.dot_general(kc_ref[0, :, lanes(e)], q_ref[0, :, lanes(0 if q_shared else e)], dn,
                             preferred_element_type=F32) for e in range(2)]
    x0 = mxu_phase(s0, k_lat(0), None, None, 0, None)
    m, _, p_ctx = softmax_phase(s_ctx, [jnp.max(s, axis=0, keepdims=True) for s in s_ctx], None, m)
    x1 = mxu_phase(s1, k_lat(tk), lambda e, rows, cols: p_ctx[e][rows, cols],
                   lambda e: vrows(vc_ref, e, slice(0, lc)), lc // t, ones)
    m, alpha, _ = softmax_phase(s0, x0, p0, m)

    def body(j, carry):
        m, alpha, x1 = (list(c) for c in carry)
        r = pl.multiple_of(2 * j * tk, tk)
        x0 = mxu_phase(s0, k_lat(r + 2 * tk), from_ref(p0), v_lat(r), tk // t, alpha)
        m, alpha, _ = softmax_phase(s1, x1, p1, m)
        x1 = mxu_phase(s1, k_lat(r + 3 * tk), from_ref(p1), v_lat(r + tk), tk // t, alpha)
        m, alpha, _ = softmax_phase(s0, x0, p0, m)
        return tuple(m), tuple(alpha), tuple(x1)

    m, alpha, x1 = lax.fori_loop(0, (nblk - 2) // 2, body, (tuple(m), tuple(alpha), tuple(x1)))
    mxu_phase(None, None, from_ref(p0), v_lat((nblk - 2) * tk), tk // t, alpha)
    m, alpha, _ = softmax_phase(s1, list(x1), p1, list(m))
    mxu_phase(None, None, from_ref(p1), v_lat((nblk - 1) * tk), tk // t, alpha)
    dv = LANE // 2
    outs = [acc_ref[e, :dv, :] / acc_ref[e, dv:dv + 1, :] for e in range(2)]
    o_ref[0] = jnp.concatenate(outs, axis=0).T.astype(o_ref.dtype)


def _attention(q, kc, vc, kl, vl, *, q_shared, kv_group, tq, tk):
    b, l, _ = q.shape
    wq = LANE if q_shared else 2 * LANE
    pairs = q.shape[-1] // wq
    lc = kc.shape[1]
    assert (l // tk) % 2 == 0, "key blocks are consumed two per loop trip"
    vw = LANE if q_shared else 2 * LANE
    kv = lambda bi, j, i: (bi, 0, j // kv_group)
    vt = lambda bi, j, i: (bi, j // kv_group, 0)
    return pl.pallas_call(
        functools.partial(_attn_kernel, tk=tk, q_shared=q_shared),
        grid=(b, pairs, l // tq),
        in_specs=[pl.BlockSpec((1, tq, wq), lambda bi, j, i: (bi, i, j)),
                  pl.BlockSpec((1, lc, 2 * LANE), kv), pl.BlockSpec((1, vw, lc), vt),
                  pl.BlockSpec((1, l, 2 * LANE), kv), pl.BlockSpec((1, vw, l), vt)],
        out_specs=pl.BlockSpec((1, tq, LANE), lambda bi, j, i: (bi, i, j)),
        out_shape=jax.ShapeDtypeStruct((b, l, pairs * LANE), BF16),
        scratch_shapes=[pltpu.VMEM((2, tk, tq), F32), pltpu.VMEM((2, tk, tq), F32),
                        pltpu.VMEM((2, tk, tq), BF16), pltpu.VMEM((2, tk, tq), BF16),
                        pltpu.VMEM((2, _ATT_VROWS, tq), F32)],
        compiler_params=_cparams(("arbitrary", "arbitrary", "arbitrary")),
        name="attn_gqa" if q_shared else "attn_mla",
    )(q, kc, vc, kl, vl)


def _out0_kernel(x_ref, oa_ref, om_ref, sg_ref, w_ref, ada_ref, o_ref, *, d):
    o = jnp.concatenate([oa_ref[0], om_ref[0]], axis=-1).astype(F32) * sg_ref[0].astype(F32)
    y = jnp.dot(o.astype(BF16), w_ref[...], preferred_element_type=F32)
    o_ref[0] = x_ref[0] + ada_ref[0][:, 2 * d:3 * d] * y


def _out0(x, oa, om, sg, w, ada, *, tl):
    b, l, d = x.shape
    row = lambda bi, i: (bi, i, 0)
    return pl.pallas_call(
        functools.partial(_out0_kernel, d=d),
        grid=(b, l // tl),
        in_specs=[pl.BlockSpec((1, tl, d), row), pl.BlockSpec((1, tl, oa.shape[-1]), row),
                  pl.BlockSpec((1, tl, om.shape[-1]), row), pl.BlockSpec((1, tl, d), row),
                  pl.BlockSpec(w.shape, lambda bi, i: (0, 0)),
                  pl.BlockSpec((1, 1, ada.shape[-1]), lambda bi, i: (bi, 0, 0))],
        out_specs=pl.BlockSpec((1, tl, d), row),
        out_shape=jax.ShapeDtypeStruct((b, l, d), F32),
        compiler_params=_cparams(("arbitrary", "arbitrary")),
        name="out0",
    )(x, oa, om, sg, w, ada)


_HALO = 8
_HY_STORE = BF16


def _hyin_kernel(x_ref, xp_ref, xn_ref, ada_ref, nw_ref, w_ref, cw_ref, cb_ref, *o_refs, d):
    i = pl.program_id(1)
    tl = x_ref.shape[1]
    n_conv = len(o_refs) - 1
    ada = ada_ref[0]
    shift, scale = ada[:, :d], ada[:, d:2 * d]

    def mod(x):
        return (_row_rms(x) * nw_ref[...]) * (1.0 + scale) + shift

    hp = mod(xp_ref[0]) * (i > 0).astype(F32)
    hn = mod(xn_ref[0]) * (i < pl.num_programs(1) - 1).astype(F32)
    h = jnp.concatenate([mod(x_ref[0]), hp, hn], axis=0).astype(BF16)
    rows = lax.broadcasted_iota(jnp.int32, (tl, d), 0)
    cw = cw_ref[...]
    for n in range(n_conv + 1):
        p = jnp.dot(h, w_ref[:, n * d:(n + 1) * d], preferred_element_type=F32)
        pm = p[0:tl]
        if n < n_conv:
            prev = jnp.where(rows == 0, p[tl + _HALO - 1:tl + _HALO], pltpu.roll(pm, 1, axis=0))
            nxt = jnp.where(rows == tl - 1, p[tl + _HALO:tl + _HALO + 1], pltpu.roll(pm, tl - 1, axis=0))
            c0, c1, c2 = (cw[j:j + 1, n * d:(n + 1) * d] for j in range(HY_SHORT))
            out = prev * c0 + pm * c1 + nxt * c2 + cb_ref[:, n * d:(n + 1) * d]
        else:
            out = _silu(pm)
        o_refs[n][0] = out.astype(o_refs[n].dtype)


def _hyin(x, ada, nw, w, cw, cb, *, tl):
    b, l, d = x.shape
    ng = w.shape[1] // d
    tb = tl // _HALO
    nb = l // _HALO
    const = lambda bi, i: (0, 0)
    blk = pl.BlockSpec((1, tl, d), lambda bi, i: (bi, i, 0))
    return pl.pallas_call(
        functools.partial(_hyin_kernel, d=d),
        grid=(b, l // tl),
        in_specs=[blk,
                  pl.BlockSpec((1, _HALO, d), lambda bi, i: (bi, jnp.maximum(i * tb - 1, 0), 0)),
                  pl.BlockSpec((1, _HALO, d), lambda bi, i: (bi, jnp.minimum((i + 1) * tb, nb - 1), 0)),
                  pl.BlockSpec((1, 1, ada.shape[-1]), lambda bi, i: (bi, 0, 0)),
                  pl.BlockSpec(nw.shape, const), pl.BlockSpec(w.shape, const),
                  pl.BlockSpec(cw.shape, const), pl.BlockSpec(cb.shape, const)],
        out_specs=[blk] * ng,
        out_shape=[jax.ShapeDtypeStruct((b, l, d), _HY_STORE)] * ng,
        compiler_params=_cparams(("arbitrary", "arbitrary")),
        name="hy_in",
    )(x, x, x, ada, nw, w, cw, cb)


def _filt_kernel(emb_ref, w1_ref, b1_ref, w2_ref, b2_ref, w3_ref, b3_ref, fr_ref, dl_ref, h_o, s_o, *, reps):
    emb = emb_ref[...]
    fr = fr_ref[...]
    hid = jnp.sin(fr * (jnp.dot(emb, w1_ref[...], precision=HI, preferred_element_type=F32) + b1_ref[...]))
    hid = jnp.sin(fr * (jnp.dot(hid, w2_ref[...], precision=HI, preferred_element_type=F32) + b2_ref[...]))
    h = jnp.dot(hid, w3_ref[...], precision=HI, preferred_element_type=F32) + b3_ref[...]
    win = jnp.exp(-emb[:, 0:1] * dl_ref[...])
    hw = h * jnp.concatenate([win] * reps, axis=-1)
    rows = lax.broadcasted_iota(jnp.int32, hw.shape, 0) + pl.program_id(0) * hw.shape[0]
    hw = jnp.where((rows == 0) & (_lane_iota(hw.shape) >= hw.shape[1] // 2), 0.0, hw)
    h_o[...] = hw

    @pl.when(pl.program_id(0) == 0)
    def _():
        s_o[...] = jnp.zeros_like(s_o)

    s_o[...] += jnp.sum(jnp.abs(hw), axis=0, keepdims=True)


def _filters(emb, w1, b1, w2, b2, w3, b3, fr, dl, *, tl):
    l = emb.shape[0]
    wo = w3.shape[1]
    const = lambda i: (0, 0)
    full = lambda a: pl.BlockSpec(a.shape, const)
    return pl.pallas_call(
        functools.partial(_filt_kernel, reps=wo // dl.shape[1]),
        grid=(l // tl,),
        in_specs=[pl.BlockSpec((tl, emb.shape[1]), lambda i: (i, 0)), full(w1), full(b1), full(w2), full(b2),
                  full(w3), full(b3), full(fr), full(dl)],
        out_specs=[pl.BlockSpec((tl, wo), lambda i: (i, 0)), pl.BlockSpec((1, wo), const)],
        out_shape=[jax.ShapeDtypeStruct((l, wo), F32), jax.ShapeDtypeStruct((1, wo), F32)],
        compiler_params=_cparams(("arbitrary",)),
        name="hy_filter",
    )(emb, w1, b1, w2, b2, w3, b3, fr, dl)


def _cmul_const(a, ang):
    ar, ai = a
    q = ang / (0.5 * math.pi)
    if abs(q - round(q)) < 1e-12:
        return [(ar, ai), (-ai, ar), (-ar, -ai), (ai, -ar)][int(round(q)) % 4]
    c, s = math.cos(ang), math.sin(ang)
    return (ar * c - ai * s, ar * s + ai * c)


def _fft_dif(x):
    x = list(x)
    n = len(x)
    half = n // 2
    while half >= 1:
        for base in range(0, n, 2 * half):
            for j in range(half):
                a, b = x[base + j], x[base + j + half]
                ang = -math.pi * j / half
                if b is None:
                    x[base + j + half] = None if a is None else _cmul_const(a, ang)
                else:
                    x[base + j] = (a[0] + b[0], a[1] + b[1])
                    x[base + j + half] = _cmul_const((a[0] - b[0], a[1] - b[1]), ang)
        half //= 2
    return x


def _ifft_dit(x, keep):
    x = list(x)
    n = len(x)
    half = 1
    while half <= n // 2:
        last = half == n // 2
        for base in range(0, n, 2 * half):
            for j in range(half):
                a = x[base + j]
                b = _cmul_const(x[base + j + half], math.pi * j / half)
                x[base + j] = (a[0] + b[0], a[1] + b[1])
                if not last or base + j + half < keep:
                    x[base + j + half] = (a[0] - b[0], a[1] - b[1])
        half *= 2
    return x[:keep]


def _outer_fwd_kernel(u_ref, o_ref, *, real_input):
    n1 = o_ref.shape[2]
    if real_input:
        x = [(u_ref[i], jnp.zeros_like(u_ref[i])) for i in range(n1 // 2)]
    else:
        x = [(u_ref[0, 0, i].astype(F32), u_ref[0, 1, i].astype(F32)) for i in range(n1 // 2)]
    for s, (re, im) in enumerate(_fft_dif(x + [None] * (n1 // 2))):
        o_ref[0, 0, s] = re.astype(o_ref.dtype)
        o_ref[0, 1, s] = im.astype(o_ref.dtype)


def _outer_fwd(u, *, real_input, tn2, ct, nseq=1):
    n2 = u.shape[-2]
    if real_input:
        p, c, n1 = nseq, u.shape[-1] // nseq, 2 * u.shape[0]
        in_spec = pl.BlockSpec((n1 // 2, tn2, ct), lambda pi, r, j: (0, r, pi * (c // ct) + j))
    else:
        p, c, n1 = u.shape[0], u.shape[-1], 2 * u.shape[2]
        in_spec = pl.BlockSpec((1, 2, n1 // 2, tn2, ct), lambda pi, r, j: (pi, 0, 0, r, j))
    return pl.pallas_call(
        functools.partial(_outer_fwd_kernel, real_input=real_input),
        grid=(p, n2 // tn2, c // ct), in_specs=[in_spec],
        out_specs=pl.BlockSpec((1, 2, n1, tn2, ct), lambda pi, r, j: (pi, 0, 0, r, j)),
        out_shape=jax.ShapeDtypeStruct((p, 2, n1, n2, c), _HY_STORE),
        compiler_params=_cparams(("arbitrary", "arbitrary", "arbitrary")),
        name="hy_outer_filt" if real_input else "hy_outer_fwd",
    )(u)


def _outer_inv_kernel(a_ref, u_ref, g_ref, sk_ref, o_ref, *next_ref):
    n1 = a_ref.shape[2]
    f = lambda ref, r, i: ref[0, r, i].astype(F32)
    y = _ifft_dit([(f(a_ref, 0, s), f(a_ref, 1, s)) for s in range(n1)], n1 // 2)
    sk = sk_ref[...]
    z = [(f(g_ref, 0, i) * (re + f(u_ref, 0, i) * sk), f(g_ref, 1, i) * (im + f(u_ref, 1, i) * sk))
         for i, (re, im) in enumerate(y)]
    for i, (re, im) in enumerate(z):
        o_ref[0, 0, i] = re.astype(o_ref.dtype)
        o_ref[0, 1, i] = im.astype(o_ref.dtype)
    if next_ref:
        nxt = next_ref[0]
        for s, (re, im) in enumerate(_fft_dif(z + [None] * (n1 // 2))):
            nxt[0, 0, s] = re.astype(nxt.dtype)
            nxt[0, 1, s] = im.astype(nxt.dtype)


def _outer_inv(a, u, gate, skip, *, tn2, ct, with_next):
    p, _, n1, n2, c = a.shape
    blk = lambda rows: pl.BlockSpec((1, 2, rows, tn2, ct), lambda pi, r, j: (pi, 0, 0, r, j))
    out_specs, out_shape = [blk(n1 // 2)], [jax.ShapeDtypeStruct(u.shape, u.dtype)]
    if with_next:
        out_specs.append(blk(n1))
        out_shape.append(jax.ShapeDtypeStruct(a.shape, a.dtype))
    return pl.pallas_call(
        _outer_inv_kernel,
        grid=(p, n2 // tn2, c // ct),
        in_specs=[blk(n1), blk(n1 // 2), blk(n1 // 2), pl.BlockSpec((1, ct), lambda pi, r, j: (0, j))],
        out_specs=out_specs, out_shape=out_shape,
        compiler_params=_cparams(("arbitrary", "arbitrary", "arbitrary")),
        name="hy_outer_inv_fwd" if with_next else "hy_outer_inv",
    )(a, u, gate, skip)


def _dot3(m3, x):
    hi = x.astype(BF16)
    lo = (x - hi.astype(F32)).astype(BF16)
    return jnp.dot(m3, jnp.concatenate([hi, hi, lo], axis=0), preferred_element_type=F32)


def _mid_kernel(a_ref, g_ref, *rest, spectrum):
    np_, n2, ct = a_ref.shape[0], a_ref.shape[3], a_ref.shape[4]
    mm = lambda m, x: jnp.dot(m, x.astype(BF16), preferred_element_type=F32)
    xs = [mm(g_ref[0], a_ref[p, :, 0].reshape(2 * n2, ct)) for p in range(np_)]
    if spectrum:
        sc_ref, o_ref = rest
        f, bw = xs
        sc = sc_ref[...]
        o_ref[0, 0, 0] = (f[:n2] + bw[:n2]) * sc
        o_ref[0, 1, 0] = (f[n2:] - bw[n2:]) * sc
    else:
        gt_ref, kf_ref, o_ref = rest
        kr, ki = kf_ref[0, 0, 0], kf_ref[0, 1, 0]
        ys = [jnp.concatenate([x[:n2] * kr - x[n2:] * ki, x[:n2] * ki + x[n2:] * kr], axis=0) for x in xs]
        for p in range(np_):
            o_ref[p, :, 0] = mm(gt_ref[0], ys[p]).reshape(2, n2, ct).astype(o_ref.dtype)


def _mid(a, g, gt=None, kf=None, scale=None, *, ct, kf_col0=0):
    p, _, n1, n2, c = a.shape
    ablk = pl.BlockSpec((p, 2, 1, n2, ct), lambda k, j: (0, 0, k, 0, j))
    gblk = pl.BlockSpec((1,) + g.shape[1:], lambda k, j: (k, 0, 0))
    oblk, oshape = ablk, a.shape
    if kf is None:
        ins = [a, g, scale]
        in_specs = [ablk, gblk, pl.BlockSpec((1, ct), lambda k, j: (0, j))]
        oblk, oshape = pl.BlockSpec((1, 2, 1, n2, ct), lambda k, j: (0, 0, k, 0, j)), (1,) + a.shape[1:]
    else:
        ins = [a, g, gt, kf]
        in_specs = [ablk, gblk, gblk,
                    pl.BlockSpec((1, 2, 1, n2, ct), lambda k, j: (0, 0, k, 0, kf_col0 + j))]
    return pl.pallas_call(
        functools.partial(_mid_kernel, spectrum=kf is None),
        grid=(n1, c // ct), in_specs=in_specs, out_specs=oblk,
        out_shape=jax.ShapeDtypeStruct(oshape, F32 if kf is None else a.dtype),
        compiler_params=_cparams(("arbitrary", "arbitrary")),
        name="hy_spectrum" if kf is None else "hy_mid",
    )(*ins)


def _dft_tables(l):
    n = 2 * l
    n2 = DFT_N2
    n1 = n // n2
    bits = n1.bit_length() - 1
    k1 = np.array([int(format(s, "0%db" % bits)[::-1], 2) for s in range(n1)], dtype=np.float64)
    kk = k1[:, None, None] + n1 * np.arange(n2, dtype=np.float64)[None, :, None]
    th = 2.0 * np.pi * kk * np.arange(n2, dtype=np.float64)[None, None, :] / n
    c, s = np.cos(th), np.sin(th)
    g = np.concatenate([np.concatenate([c, s], axis=2), np.concatenate([-s, c], axis=2)], axis=1)

    def split3(m):
        m = jnp.asarray(m, F32)
        hi = m.astype(BF16)
        lo = (m - hi.astype(F32)).astype(BF16)
        return jnp.concatenate([hi, lo, hi], axis=2)

    return split3(g), jnp.asarray(g, BF16), jnp.asarray(np.transpose(g, (0, 2, 1)), BF16)


def _pair_view(u, n1h):
    b, l, c = u.shape
    return u.reshape(b // 2, 2, n1h, l // n1h, c)


def _long_convs(u, gates, skips, kf, tabs, *, ct):
    _, g, gt = tabs
    c = u.shape[-1]
    a = _outer_fwd(u, real_input=False, tn2=32, ct=256)
    z = u
    for o, (gate, skip) in enumerate(zip(gates, skips)):
        a = _mid(a, g, gt, kf, ct=ct, kf_col0=o * (c // ct))
        last = o == len(gates) - 1
        res = _outer_inv(a, z, gate, skip, tn2=32, ct=256, with_next=not last)
        z, a = (res[0], None) if last else res
    return z


def _out1_kernel(x_ref, z_ref, sg_ref, w_ref, ada_ref, fw_ref, o_ref, *, d):
    y = jnp.dot((z_ref[0].astype(F32) * sg_ref[0].astype(F32)).astype(BF16), w_ref[...],
                preferred_element_type=F32)
    x = x_ref[0] + ada_ref[0][:, 2 * d:3 * d] * y
    o_ref[0] = _row_rms(x) * fw_ref[...]


def _out1(x, z, sg, w, ada, fw, *, tl):
    b, l, d = x.shape
    row = lambda bi, i: (bi, i, 0)
    blk = pl.BlockSpec((1, tl, d), row)
    return pl.pallas_call(
        functools.partial(_out1_kernel, d=d),
        grid=(b, l // tl),
        in_specs=[blk, blk, blk, pl.BlockSpec(w.shape, lambda bi, i: (0, 0)),
                  pl.BlockSpec((1, 1, ada.shape[-1]), lambda bi, i: (bi, 0, 0)),
                  pl.BlockSpec(fw.shape, lambda bi, i: (0, 0))],
        out_specs=blk,
        out_shape=jax.ShapeDtypeStruct((b, l, d), F32),
        compiler_params=_cparams(("arbitrary", "arbitrary")),
        name="out1",
    )(x, z, sg, w, ada, fw)


def _swap_cols(w, q):
    return w[..., np.arange(w.shape[-1]) ^ q]


def _pack_attn_w_in(w):
    d = w.shape[0]
    o = 0
    wq = w[:, o:o + 512]; o += 512
    wk = w[:, o:o + 128]; o += 128
    wv = w[:, o:o + 128]; o += 128
    wcq = w[:, o:o + MLA_Q_RANK]; o += MLA_Q_RANK
    wckv = w[:, o:o + MLA_KV_RANK]; o += MLA_KV_RANK
    wkpe = w[:, o:o + MLA_ROPE_DIM]; o += MLA_ROPE_DIM
    wg = w[:, o:]
    qa, qm = GQA_HEAD_DIM // 4, MLA_ROPE_DIM // 4

    def pe_chunk(wp):
        return jnp.concatenate([jnp.zeros((d, 64), w.dtype), wp, jnp.zeros((d, 32), w.dtype)], axis=1)

    kpe, kpe_sw = pe_chunk(wkpe), pe_chunk(_swap_cols(wkpe, qm))
    kv_part = jnp.concatenate([wk, wv, wckv, kpe], axis=1)
    lat = jnp.concatenate([wq, _swap_cols(wq, qa), wk, _swap_cols(wk, qa), wv, wcq, wckv, kpe, kpe_sw, wg], axis=1)
    return lat.astype(BF16), kv_part.astype(BF16)


def _pack_mla_up(w_uq, w_ukv):
    dq = MLA_NOPE_DIM + MLA_ROPE_DIM
    r = w_uq.shape[0]
    z = lambda n: jnp.zeros((r, n), w_uq.dtype)
    uq, uq_sw = [], []
    for h in range(MLA_HEADS):
        nope, pe = w_uq[:, dq * h:dq * h + MLA_NOPE_DIM], w_uq[:, dq * h + MLA_NOPE_DIM:dq * (h + 1)]
        uq += [nope, pe, z(LANE - dq)]
        uq_sw += [z(MLA_NOPE_DIM), _swap_cols(pe, MLA_ROPE_DIM // 4), z(LANE - dq)]
    dkv = MLA_NOPE_DIM + MLA_V_DIM
    kn = jnp.concatenate(
        [jnp.concatenate([w_ukv[:, dkv * h:dkv * h + MLA_NOPE_DIM],
                          jnp.zeros((w_ukv.shape[0], LANE - MLA_NOPE_DIM), w_ukv.dtype)], axis=1)
         for h in range(MLA_HEADS)], axis=1)
    vm = jnp.concatenate(
        [jnp.concatenate([w_ukv[:, dkv * h + MLA_NOPE_DIM:dkv * (h + 1)],
                          jnp.zeros((w_ukv.shape[0], LANE - MLA_V_DIM), w_ukv.dtype)], axis=1)
         for h in range(MLA_HEADS)], axis=1)
    return jnp.concatenate(uq + uq_sw, axis=1).astype(BF16), jnp.concatenate([kn, vm], axis=1).astype(BF16)


def _head_ones2(width):
    i = np.arange(width) // GQA_HEAD_DIM
    blk = (i[:, None] == i[None, :]).astype(np.float32)
    return jnp.asarray(np.concatenate([blk, blk], axis=0), BF16)


def _rope_tables(l):
    rows = (jnp.arange(l, dtype=jnp.int32) // GRID_W).astype(F32)[:, None]
    cols = (jnp.arange(l, dtype=jnp.int32) % GRID_W).astype(F32)[:, None]

    def tab(rot_dim):
        q = rot_dim // 4
        inv = ROPE_BASE ** (-jnp.arange(q, dtype=F32) / q)
        ar, ac = rows * inv, cols * inv
        cos = jnp.concatenate([jnp.cos(ar)] * 2 + [jnp.cos(ac)] * 2, axis=1)
        sin = jnp.concatenate([-jnp.sin(ar), jnp.sin(ar), -jnp.sin(ac), jnp.sin(ac)], axis=1)
        return cos, sin

    ca, sa = tab(GQA_HEAD_DIM)
    ca = jnp.concatenate([ca] * (LANE // GQA_HEAD_DIM), axis=1)
    sa = jnp.concatenate([sa] * (LANE // GQA_HEAD_DIM), axis=1)
    cm, sm = tab(MLA_ROPE_DIM)
    one, zero = jnp.ones((l, 1), F32), jnp.zeros((l, 1), F32)
    cm = jnp.concatenate([jnp.tile(one, (1, 64)), cm, jnp.tile(one, (1, 32))], axis=1)
    sm = jnp.concatenate([jnp.tile(zero, (1, 64)), sm, jnp.tile(zero, (1, 32))], axis=1)
    return ca, sa, cm, sm


def _pad2(a, r, c):
    return jnp.pad(a, ((0, r - a.shape[0]), (0, c - a.shape[1])))


def kernel(x, c, ctx, c_ctx, ada_w, ada_b, norm_w, attn_w_in, attn_q_norm, attn_k_norm, mla_q_norm, mla_kv_norm, mla_w_uq, mla_w_ukv, attn_w_out, hy_w_in, hy_conv_w, hy_conv_b, hy_ffn_w1, hy_ffn_b1, hy_ffn_w2, hy_ffn_b2, hy_ffn_w3, hy_ffn_b3, hy_freq, hy_skip, hy_w_out, final_norm_w):
    b, l, d = x.shape
    lc = ctx.shape[1]
    tl = min(256, l)

    rows = -(-(b + 1) // 8) * 8
    cs = jnp.concatenate([c, c_ctx[None, :], jnp.zeros((rows - b - 1, d), F32)], axis=0)
    ada = _ada(cs, ada_w, ada_b)
    ada_lat = [ada[i, :b].reshape(b, 1, 3 * d) for i in range(ada.shape[0])]
    ada_ctx0 = jnp.broadcast_to(ada[0, b].reshape(1, 1, 3 * d), (b, 1, 3 * d))

    w_lat, w_kv = _pack_attn_w_in(attn_w_in[0])
    wuq, wukv = _pack_mla_up(mla_w_uq[0], mla_w_ukv[0])
    nw0 = norm_w[0].reshape(1, d)
    def norm_rows(wn, width):
        sw = _swap_cols(wn, GQA_HEAD_DIM // 4)
        return jnp.stack([jnp.tile(wn, width // GQA_HEAD_DIM), jnp.tile(sw, width // GQA_HEAD_DIM)])

    qn, kn = norm_rows(attn_q_norm[0], _QA_W), norm_rows(attn_k_norm[0], LANE)
    cqn = mla_q_norm[0].reshape(1, MLA_Q_RANK)
    ckvn = mla_kv_norm[0].reshape(1, MLA_KV_RANK)
    o2q, o2k = _head_ones2(_QA_W), _head_ones2(LANE)
    tabs = _rope_tables(l)
    qa, ka, va, qm, km, vm, sg = _prep(x, ada_lat[0], nw0, w_lat, qn, kn, cqn, ckvn, wuq, wukv, o2q, o2k, tabs,
                                       latent=True, tl=min(512, l))
    kac, vac, kmc, vmc = _prep(ctx, ada_ctx0, nw0, w_kv, None, kn, None, ckvn, None, wukv, None, o2k, None,
                               latent=False, tl=min(tl, lc))
    tq, tk = min(2048, l), min(512, l // 2)
    oa = _attention(qa, kac, vac, ka, va, q_shared=True, kv_group=2, tq=tq, tk=tk)
    om = _attention(qm, kmc, vmc, km, vm, q_shared=False, kv_group=1, tq=tq, tk=tk)
    x1 = _out0(x, oa, om, sg, attn_w_out[0].astype(BF16), ada_lat[0], tl=tl)

    nw1 = norm_w[1].reshape(1, d)
    u = _hyin(x1, ada_lat[1], nw1, hy_w_in[0].astype(BF16), hy_conv_w[0], hy_conv_b[0].reshape(1, -1),
              tl=min(512, l))
    n1 = 2 * l // DFT_N2
    n1h = n1 // 2
    tabs_d = _dft_tables(l)

    t = jnp.linspace(0.0, 1.0, l, dtype=F32)[:, None]
    wpos = (2.0 * math.pi / l) * jnp.arange(l, dtype=F32)[:, None]
    bands = jnp.linspace(1e-4, HY_BANDS - 1, HY_BANDS, dtype=F32)
    emb = jnp.concatenate([t, jnp.cos(wpos * bands), -jnp.sin(wpos * bands)], axis=-1)
    deltas = jnp.abs(jnp.linspace(math.log(HY_DECAY_TARGET) / HY_SLOW_DECAY,
                                  math.log(HY_DECAY_TARGET) / HY_FAST_DECAY, d, dtype=F32)).reshape(1, d)
    wf = hy_ffn_w3.shape[-1]
    oc = HY_ORDER * d

    def by_direction(a):
        return a.reshape(-1, HY_ORDER, 2, d).transpose(0, 2, 1, 3).reshape(-1, wf)

    hw, asum = _filters(_pad2(emb, l, LANE), _pad2(hy_ffn_w1[0], LANE, LANE), _pad2(hy_ffn_b1[0][None], 1, LANE),
                        _pad2(hy_ffn_w2[0], LANE, LANE), _pad2(hy_ffn_b2[0][None], 1, LANE),
                        _pad2(by_direction(hy_ffn_w3[0]), LANE, wf), by_direction(hy_ffn_b3[0][None]),
                        _pad2(hy_freq[0][None], 1, LANE), deltas, tl=tl)
    l1 = asum[:, :oc] + asum[:, oc:]
    ct = min(1024, d)
    af = _outer_fwd(hw.reshape(n1h, DFT_N2, wf), real_input=True, nseq=2, tn2=32, ct=256)
    kf = _mid(af, tabs_d[1], scale=1.0 / (l1 * (2 * l)), ct=oc)

    v2, x1g, x2g = (_pair_view(u[i], n1h) for i in range(3))
    z = _long_convs(v2, [x1g, x2g], [hy_skip[0, o:o + 1] for o in range(HY_ORDER)], kf, tabs_d, ct=ct)
    z = z.reshape(b, l, d)
    return _out1(x1, z, u[3], hy_w_out[0].astype(BF16), ada_lat[1], final_norm_w.reshape(1, d), tl=tl)
```
